```python
import math
import jax, jax.numpy as jnp
from jax import lax
import numpy as np

D_MODEL = 1024
BATCH = 32
SEQ = 256
DEPTH = 2
DEC_BATCH = 8
DEC_SEQ = 1024
PAST_LEN = 256

GRID_W = 64
N_LRU_LAYERS = (DEPTH + 1) // 2
N_CM_LAYERS = DEPTH // 2
LRU_WIDTH = D_MODEL
LRU_HEADS = 16
LRU_HEAD_DIM = LRU_WIDTH // LRU_HEADS
CONV_WIDTH = 4
LRU_C = 8.0
CHUNK = 128
CM_WIDTH = D_MODEL
CM_GROUPS = 8
CM_GROUP_DIM = CM_WIDTH // CM_GROUPS
D_FF = int(math.ceil(8 * D_MODEL / 3 / 256) * 256)
N_MOD = 6
EPS = 1e-6

kernel_name = "hybrid_rglru_chunkmlp_diffusion_step"


def rmsnorm(x, g):
    xf = x.astype(jnp.float32)
    y = xf * lax.rsqrt(jnp.mean(xf * xf, axis=-1, keepdims=True) + EPS)
    return (y * g.astype(jnp.float32)).astype(x.dtype)


def layernorm(x, g, b):
    xf = x.astype(jnp.float32)
    mu = jnp.mean(xf, axis=-1, keepdims=True)
    var = jnp.mean(jnp.square(xf - mu), axis=-1, keepdims=True)
    y = (xf - mu) * lax.rsqrt(var + EPS)
    return (y * g.astype(jnp.float32) + b.astype(jnp.float32)).astype(x.dtype)


def adaln(cond, w, b):
    m = jax.nn.silu(cond) @ w + b
    return m.reshape(cond.shape[0], N_MOD, D_MODEL)


def conv_centred(x, w, b):
    y = lax.conv_general_dilated(
        x, w[:, None, :].astype(x.dtype), window_strides=(1,),
        padding=[((CONV_WIDTH - 1) // 2, CONV_WIDTH - 1 - (CONV_WIDTH - 1) // 2)],
        dimension_numbers=("NWC", "WIO", "NWC"), feature_group_count=x.shape[-1])
    return y + b


def linear_scan(a, bx, h0, reverse):
    def step(h, ab):
        h = ab[0] * h + ab[1]
        return h, h
    h_last, hs = lax.scan(step, h0, (jnp.swapaxes(a, 0, 1), jnp.swapaxes(bx, 0, 1)), reverse=reverse)
    return h_last, jnp.swapaxes(hs, 0, 1)


def rglru_mixer(h, h0, w_in, conv_w, conv_b, ga_w, ga_b, gx_w, gx_b, lam, w_out):
    B, S, _ = h.shape
    x, y = jnp.split(h @ w_in, 2, axis=-1)
    y = jax.nn.gelu(y)
    x = conv_centred(x, conv_w, conv_b)
    xh = x.reshape(B, S, LRU_HEADS, LRU_HEAD_DIM)
    r = jax.nn.sigmoid(jnp.einsum('bshd,zhde->zbshe', xh, ga_w).reshape(2, B, S, LRU_WIDTH)
                       + ga_b[:, None, None, :])
    ig = jax.nn.sigmoid(jnp.einsum('bshd,zhde->zbshe', xh, gx_w).reshape(2, B, S, LRU_WIDTH)
                        + gx_b[:, None, None, :])
    log_a = -LRU_C * r.astype(jnp.float32) * jax.nn.softplus(-lam.astype(jnp.float32))[:, None, None, :]
    a = jnp.exp(log_a)
    bx = jnp.sqrt(-jnp.expm1(2.0 * log_a)) * (ig * x[None]).astype(jnp.float32)
    h0 = h0.astype(jnp.float32)
    hf_last, hf = linear_scan(a[0], bx[0], h0[:, 0], reverse=False)
    hb_last, hb = linear_scan(a[1], bx[1], h0[:, 1], reverse=True)
    out = ((hf + hb).astype(h.dtype) * y) @ w_out
    return out, jnp.stack([hf_last, hb_last], axis=1)


def chunk_mlp_mixer(h, n_chunks, w_in, b_in, ln_g, ln_b, w_s, b_s, w_out):
    B, S, _ = h.shape
    u, v = jnp.split(jax.nn.gelu(h @ w_in + b_in), 2, axis=-1)
    v = layernorm(v, ln_g, ln_b)
    vc = v.reshape(B, n_chunks, CHUNK, CM_GROUPS, CM_GROUP_DIM)
    mixed = jnp.einsum('gpq,bnqgd->bnpgd', w_s, vc) + jnp.swapaxes(b_s, 0, 1)[None, None, :, :, None]
    return (u * mixed.reshape(B, S, CM_WIDTH)) @ w_out


def swiglu(h, w_gate, w_up, w_down):
    return (jax.nn.silu(h @ w_gate) * (h @ w_up)) @ w_down


def setup_inputs(seed: int = 0) -> dict:
    key = jax.random.key(seed)
    ks = iter(jax.random.split(key, 40))

    def nrm(shape, scale):
        return jax.random.normal(next(ks), shape, jnp.float32) * scale

    def gain(shape):
        return 1.0 + nrm(shape, 0.05)

    D, W, F = D_MODEL, LRU_WIDTH, D_FF
    u = jax.random.uniform(next(ks), (N_LRU_LAYERS, 2, W), jnp.float32, 0.9, 0.999)
    s = u ** (1.0 / LRU_C)
    lam = jnp.log(s) - jnp.log1p(-s)
    return {
        "x_prompt": nrm((BATCH, SEQ, D), 1.0),
        "x_sample": nrm((DEC_BATCH, DEC_SEQ, D), 1.0),
        "state_lru": nrm((DEC_BATCH, N_LRU_LAYERS, 2, W), 1.0),
        "c": nrm((DEC_BATCH, D), 1.0),
        "c_ctx": nrm((D,), 1.0),
        "mod_w": nrm((DEPTH, D, N_MOD * D), 0.5 * D ** -0.5),
        "mod_b": nrm((DEPTH, N_MOD * D), 0.02),
        "norm_mix": gain((DEPTH, D)),
        "norm_ffn": gain((DEPTH, D)),
        "lru_w_in": nrm((N_LRU_LAYERS, D, 2 * W), D ** -0.5),
        "lru_conv_w": nrm((N_LRU_LAYERS, CONV_WIDTH, W), CONV_WIDTH ** -0.5),
        "lru_conv_b": nrm((N_LRU_LAYERS, W), 0.02),
        "lru_ga_w": nrm((N_LRU_LAYERS, 2, LRU_HEADS, LRU_HEAD_DIM, LRU_HEAD_DIM), LRU_HEAD_DIM ** -0.5),
        "lru_ga_b": nrm((N_LRU_LAYERS, 2, W), 0.02),
        "lru_gx_w": nrm((N_LRU_LAYERS, 2, LRU_HEADS, LRU_HEAD_DIM, LRU_HEAD_DIM), LRU_HEAD_DIM ** -0.5),
        "lru_gx_b": nrm((N_LRU_LAYERS, 2, W), 0.02),
        "lru_lambda": lam,
        "lru_w_out": nrm((N_LRU_LAYERS, W, D), W ** -0.5),
        "cm_w_in": nrm((N_CM_LAYERS, D, 2 * CM_WIDTH), D ** -0.5),
        "cm_b_in": nrm((N_CM_LAYERS, 2 * CM_WIDTH), 0.02),
        "cm_ln_g": gain((N_CM_LAYERS, CM_WIDTH)),
        "cm_ln_b": nrm((N_CM_LAYERS, CM_WIDTH), 0.02),
        "cm_w_s": nrm((N_CM_LAYERS, CM_GROUPS, CHUNK, CHUNK), CHUNK ** -0.5),
        "cm_b_s": gain((N_CM_LAYERS, CM_GROUPS, CHUNK)),
        "cm_w_out": nrm((N_CM_LAYERS, CM_WIDTH, D), CM_WIDTH ** -0.5),
        "ffn_w_gate": nrm((DEPTH, D, F), D ** -0.5),
        "ffn_w_up": nrm((DEPTH, D, F), D ** -0.5),
        "ffn_w_down": nrm((DEPTH, F, D), F ** -0.5),
        "final_norm": gain((D,)),
    }


def reference(x_prompt, x_sample, state_lru, c, c_ctx, mod_w, mod_b, norm_mix, norm_ffn,
              lru_w_in, lru_conv_w, lru_conv_b, lru_ga_w, lru_ga_b, lru_gx_w, lru_gx_b,
              lru_lambda, lru_w_out, cm_w_in, cm_b_in, cm_ln_g, cm_ln_b, cm_w_s, cm_b_s,
              cm_w_out, ffn_w_gate, ffn_w_up, ffn_w_down, final_norm):
    xp, xs = x_prompt, x_sample
    ctx_len = xp.shape[1]
    rows = xs.shape[1] // GRID_W
    n_chunks_ctx = ctx_len // CHUNK
    n_chunks_lat = rows * GRID_W // CHUNK
    new_lru_states = []

    for i in range(DEPTH):
        mp = adaln(c_ctx[None, :], mod_w[i], mod_b[i])[:, :, None, :]
        ms = adaln(c, mod_w[i], mod_b[i])[:, :, None, :]
        hp = rmsnorm(xp, norm_mix[i]) * (1.0 + mp[:, 1]) + mp[:, 0]
        hs = rmsnorm(xs, norm_mix[i]) * (1.0 + ms[:, 1]) + ms[:, 0]
        if i % 2 == 0:
            j = i // 2
            p = (lru_w_in[j], lru_conv_w[j], lru_conv_b[j], lru_ga_w[j], lru_ga_b[j],
                 lru_gx_w[j], lru_gx_b[j], lru_lambda[j], lru_w_out[j])
            h0 = jnp.zeros((xp.shape[0], 2, LRU_WIDTH), jnp.float32)
            op, st = rglru_mixer(hp, h0, *p)
            os_, _ = rglru_mixer(hs, state_lru[:, j], *p)
            new_lru_states.append(st)
        else:
            j = i // 2
            p = (cm_w_in[j], cm_b_in[j], cm_ln_g[j], cm_ln_b[j], cm_w_s[j], cm_b_s[j], cm_w_out[j])
            op = chunk_mlp_mixer(hp, n_chunks_ctx, *p)
            os_ = chunk_mlp_mixer(hs, n_chunks_lat, *p)
        xp = xp + mp[:, 2] * op
        xs = xs + ms[:, 2] * os_
        hp = rmsnorm(xp, norm_ffn[i]) * (1.0 + mp[:, 4]) + mp[:, 3]
        hs = rmsnorm(xs, norm_ffn[i]) * (1.0 + ms[:, 4]) + ms[:, 3]
        xp = xp + mp[:, 5] * swiglu(hp, ffn_w_gate[i], ffn_w_up[i], ffn_w_down[i])
        xs = xs + ms[:, 5] * swiglu(hs, ffn_w_gate[i], ffn_w_up[i], ffn_w_down[i])

    y_prompt = rmsnorm(xp, final_norm)
    y_sample = rmsnorm(xs, final_norm)
    new_state_lru = jnp.stack(new_lru_states, axis=1)
    return (y_prompt, y_sample, new_state_lru)
```

```python
import functools

import jax
import jax.numpy as jnp
from jax import lax
from jax.experimental import pallas as pl
from jax.experimental.pallas import tpu as pltpu

F32 = jnp.float32
BF16 = jnp.bfloat16

EPS = 1e-6
LRU_C = 8.0
N_MOD = 6
COND_ROWS = 16
LANES = 128
CHUNK = 128
VMEM_LIMIT = 52 * 1024 * 1024


def _rmsnorm(x, g):
    y = x * lax.rsqrt(jnp.mean(x * x, axis=-1, keepdims=True) + EPS)
    return y * g


def _modulated_norm(x, g, shift, scale):
    return _rmsnorm(x, g) * (1.0 + scale) + shift


def _dot(a, b):
    return jnp.dot(a.astype(BF16), b, preferred_element_type=F32)


def _adaln_kernel(c_ref, w_ref, b_ref, o_ref):
    s = jax.nn.silu(c_ref[...])
    o_ref[...] = _dot(s, w_ref[...].astype(BF16)) + b_ref[...]


def _adaln(cond, mod_w, mod_b):
    depth, d, n = mod_w.shape
    tn = 1024
    return pl.pallas_call(
        _adaln_kernel,
        grid=(depth, n // tn),
        in_specs=[
            pl.BlockSpec((COND_ROWS, d), lambda l, j: (0, 0)),
            pl.BlockSpec((None, d, tn), lambda l, j: (l, 0, j)),
            pl.BlockSpec((None, 1, tn), lambda l, j: (l, 0, j)),
        ],
        out_specs=pl.BlockSpec((None, COND_ROWS, tn), lambda l, j: (l, 0, j)),
        out_shape=jax.ShapeDtypeStruct((depth, COND_ROWS, n), F32),
        compiler_params=pltpu.CompilerParams(
            dimension_semantics=("arbitrary", "arbitrary"), vmem_limit_bytes=VMEM_LIMIT),
        name="adaln",
    )(cond, mod_w, mod_b.reshape(depth, 1, n))


def _mod_spec(layer, tm, d, n_prompt, s_sample):
    assert n_prompt % tm == 0 and s_sample % tm == 0

    def index_map(i):
        row = jnp.where(i * tm < n_prompt, 0, 1 + (i * tm - n_prompt) // s_sample)
        return (layer * COND_ROWS + row, 0, 0)
    return pl.BlockSpec((None, N_MOD, d), index_map)


def _const_spec(shape):
    nd = len(shape)
    return pl.BlockSpec(shape, lambda i: (0,) * nd, pipeline_mode=pl.Buffered(1))


def _lru_in_kernel(x_ref, mod_ref, g_ref, w_ref, xb_ref, y_ref):
    m = mod_ref[...]
    h = _modulated_norm(x_ref[...], g_ref[...], m[0:1], m[1:2])
    z = _dot(h, w_ref[...])
    w = xb_ref.shape[-1]
    xb_ref[...] = z[:, :w]
    y_ref[...] = jax.nn.gelu(z[:, w:])


def _lru_in(x, mods, g, w_in, layer, n_prompt, s_sample):
    n, d = x.shape
    w2 = w_in.shape[1]
    w = w2 // 2
    tm = 512
    row = pl.BlockSpec((tm, d), lambda i: (i, 0))
    return pl.pallas_call(
        _lru_in_kernel,
        grid=(n // tm,),
        in_specs=[row, _mod_spec(layer, tm, d, n_prompt, s_sample),
                  _const_spec((1, d)), _const_spec((d, w2))],
        out_specs=[pl.BlockSpec((tm, w), lambda i: (i, 0)), pl.BlockSpec((tm, w), lambda i: (i, 0))],
        out_shape=[jax.ShapeDtypeStruct((n, w), F32), jax.ShapeDtypeStruct((n, w), F32)],
        compiler_params=pltpu.CompilerParams(
            dimension_semantics=("parallel",), vmem_limit_bytes=VMEM_LIMIT),
        name="lru_in",
    )(x, mods, g, w_in)


def _lru_rec_group(B, S, h0, x_ref, cw_ref, cb_ref, wg_ref, gb_ref, lam_ref,
                   out_ref, st_ref, xt, a_f, a_b, bx_f, bx_b):
    rows = B * S
    chunk = 512

    xt[pl.ds(0, B), :] = jnp.zeros((B, LANES), F32)
    xt[pl.ds(rows + B, 2 * B), :] = jnp.zeros((2 * B, LANES), F32)

    def gather(t, _):
        xt[pl.ds(pl.multiple_of((t + 1) * B, B), B), :] = x_ref[pl.ds(t, B, stride=S), :]
        return 0
    lax.fori_loop(0, S, gather, 0)

    cw = cw_ref[...]
    cb = cb_ref[...]
    gb = gb_ref[...]
    lam = lam_ref[...]
    neg = -lam
    sp = jnp.maximum(neg, 0.0) + jnp.log1p(jnp.exp(-jnp.abs(neg)))

    def gates(c, _):
        r0 = pl.multiple_of(c * chunk, chunk)
        xc = (cw[0:1] * xt[pl.ds(r0, chunk), :]
              + cw[1:2] * xt[pl.ds(r0 + B, chunk), :]
              + cw[2:3] * xt[pl.ds(r0 + 2 * B, chunk), :]
              + cw[3:4] * xt[pl.ds(r0 + 3 * B, chunk), :]) + cb
        g = _dot(xc, wg_ref[...]) + gb
        for z, (a_ref, bx_ref) in enumerate(((a_f, bx_f), (a_b, bx_b))):
            r = jax.nn.sigmoid(g[:, z * LANES:(z + 1) * LANES])
            ig = jax.nn.sigmoid(g[:, (2 + z) * LANES:(3 + z) * LANES])
            log_a = -LRU_C * r * sp[z:z + 1]
            a = jnp.exp(log_a)
            a_ref[pl.ds(r0, chunk), :] = a
            one_minus_a2 = -jnp.tanh(log_a) * (a * a + 1.0)
            bx_ref[pl.ds(r0, chunk), :] = jnp.sqrt(one_minus_a2) * (ig * xc)
        return 0
    lax.fori_loop(0, rows // chunk, gates, 0)

    def scan(t, carry):
        hf, hb = carry
        rf = pl.multiple_of(t * B, B)
        hf = a_f[pl.ds(rf, B), :] * hf + bx_f[pl.ds(rf, B), :]
        bx_f[pl.ds(rf, B), :] = hf
        rb = pl.multiple_of((S - 1 - t) * B, B)
        hb = a_b[pl.ds(rb, B), :] * hb + bx_b[pl.ds(rb, B), :]
        bx_b[pl.ds(rb, B), :] = hb
        return hf, hb
    hf, hb = lax.fori_loop(0, S, scan, h0, unroll=4)
    if st_ref is not None:
        st_ref[0] = hf
        st_ref[1] = hb

    def scatter(t, _):
        r = pl.multiple_of(t * B, B)
        out_ref[pl.ds(t, B, stride=S), :] = bx_f[pl.ds(r, B), :] + bx_b[pl.ds(r, B), :]
        return 0
    lax.fori_loop(0, S, scatter, 0)


def _lru_rec_kernel(dims, x_ref, cw_ref, cb_ref, wg_ref, gb_ref, lam_ref, h0s_ref,
                    out_ref, st_ref, xt, a_f, a_b, bx_f, bx_b):
    (bp, sp_), (bs, ss) = dims
    common = (x_ref, cw_ref, cb_ref, wg_ref, gb_ref, lam_ref)
    scratch = (xt, a_f, a_b, bx_f, bx_b)

    @pl.when(pl.program_id(1) == 0)
    def _():
        zero = jnp.zeros((bp, LANES), F32)
        _lru_rec_group(bp, sp_, (zero, zero), *common, out_ref, st_ref, *scratch)

    @pl.when(pl.program_id(1) == 1)
    def _():
        _lru_rec_group(bs, ss, (h0s_ref[0], h0s_ref[1]), *common, out_ref, None, *scratch)


def _lru_rec(xb, conv_w, conv_b, wg, gb, lam, h0s, dims):
    n, w = xb.shape
    (bp, sp_), (bs, ss) = dims
    rows = bp * sp_
    assert rows == bs * ss and n == 2 * rows
    nj = w // LANES
    pad = 3 * max(bp, bs)
    big = pl.BlockSpec((rows, LANES), lambda j, g: (g, j))
    return pl.pallas_call(
        functools.partial(_lru_rec_kernel, dims),
        grid=(nj, 2),
        in_specs=[
            big,
            pl.BlockSpec((4, LANES), lambda j, g: (0, j)),
            pl.BlockSpec((1, LANES), lambda j, g: (0, j)),
            pl.BlockSpec((None, LANES, 4 * LANES), lambda j, g: (j, 0, 0)),
            pl.BlockSpec((None, 1, 4 * LANES), lambda j, g: (j, 0, 0)),
            pl.BlockSpec((2, LANES), lambda j, g: (0, j)),
            pl.BlockSpec((2, bs, LANES), lambda j, g: (0, 0, j)),
        ],
        out_specs=[big, pl.BlockSpec((2, bp, LANES), lambda j, g: (0, 0, j))],
        out_shape=[jax.ShapeDtypeStruct((n, w), F32), jax.ShapeDtypeStruct((2, bp, w), F32)],
        scratch_shapes=[pltpu.VMEM((rows + pad, LANES), F32)] + [pltpu.VMEM((rows, LANES), F32)] * 4,
        compiler_params=pltpu.CompilerParams(
            dimension_semantics=("arbitrary", "arbitrary"), vmem_limit_bytes=VMEM_LIMIT),
        name="lru_rec",
    )(xb, conv_w, conv_b, wg, gb, lam, h0s)


def _ffn(x1, m, g, wg_ref, wu_ref, wd_ref):
    h = _modulated_norm(x1, g, m[3:4], m[4:5]).astype(BF16)
    gate = jnp.dot(h, wg_ref[...], preferred_element_type=F32)
    up = jnp.dot(h, wu_ref[...], preferred_element_type=F32)
    return x1 + m[5:6] * _dot(jax.nn.silu(gate) * up, wd_ref[...])


def _lru_out_kernel(x_ref, hs_ref, y_ref, mod_ref, g_ref, wo_ref, wg_ref, wu_ref, wd_ref, o_ref):
    m = mod_ref[...]
    x1 = x_ref[...] + m[2:3] * _dot(hs_ref[...] * y_ref[...], wo_ref[...])
    o_ref[...] = _ffn(x1, m, g_ref[...], wg_ref, wu_ref, wd_ref)


def _lru_out(x, hs, y, mods, g, w_out, w_gate, w_up, w_down, layer, n_prompt, s_sample):
    n, d = x.shape
    w = hs.shape[1]
    f = w_gate.shape[1]
    tm = 256
    row = lambda c: pl.BlockSpec((tm, c), lambda i: (i, 0))
    return pl.pallas_call(
        _lru_out_kernel,
        grid=(n // tm,),
        in_specs=[row(d), row(w), row(w), _mod_spec(layer, tm, d, n_prompt, s_sample),
                  _const_spec((1, d)), _const_spec((w, d)),
                  _const_spec((d, f)), _const_spec((d, f)), _const_spec((f, d))],
        out_specs=row(d),
        out_shape=jax.ShapeDtypeStruct((n, d), F32),
        compiler_params=pltpu.CompilerParams(
            dimension_semantics=("parallel",), vmem_limit_bytes=VMEM_LIMIT),
        name="lru_out_ffn",
    )(x, hs, y, mods, g, w_out, w_gate, w_up, w_down)


def _cm_kernel(x_ref, mod_ref, gm_ref, gf_ref, gfin_ref, wi_ref, bi_ref, lng_ref, lnb_ref,
               ws_ref, bs_ref, wo_ref, wg_ref, wu_ref, wd_ref, o_ref):
    m = mod_ref[...]
    x = x_ref[...]
    tm = x.shape[0]
    cw = lng_ref.shape[-1]
    n_groups = ws_ref.shape[0]
    gd = cw // n_groups
    h = _modulated_norm(x, gm_ref[...], m[0:1], m[1:2])
    uv = jax.nn.gelu(_dot(h, wi_ref[...]) + bi_ref[...])
    u = uv[:, :cw]
    v = uv[:, cw:]
    mu = jnp.mean(v, axis=-1, keepdims=True)
    var = jnp.mean(jnp.square(v - mu), axis=-1, keepdims=True)
    v = ((v - mu) * lax.rsqrt(var + EPS) * lng_ref[...] + lnb_ref[...]).astype(BF16)
    mixed_rows = []
    for c in range(tm // CHUNK):
        vc = v[c * CHUNK:(c + 1) * CHUNK]
        cols = [jnp.dot(ws_ref[g], vc[:, g * gd:(g + 1) * gd], preferred_element_type=F32)
                for g in range(n_groups)]
        mixed_rows.append(jnp.concatenate(cols, axis=1) + bs_ref[...])
    mixed = jnp.concatenate(mixed_rows, axis=0)
    x1 = x + m[2:3] * _dot(u * mixed, wo_ref[...])
    x2 = _ffn(x1, m, gf_ref[...], wg_ref, wu_ref, wd_ref)
    o_ref[...] = _rmsnorm(x2, gfin_ref[...])


def _cm_layer(x, mods, g_mix, g_ffn, g_final, w_in, b_in, ln_g, ln_b, w_s, b_s, w_out,
              w_gate, w_up, w_down, layer, n_prompt, s_sample):
    n, d = x.shape
    cw = w_out.shape[0]
    f = w_gate.shape[1]
    tm = 256
    row = pl.BlockSpec((tm, d), lambda i: (i, 0))
    return pl.pallas_call(
        _cm_kernel,
        grid=(n // tm,),
        in_specs=[row, _mod_spec(layer, tm, d, n_prompt, s_sample),
                  _const_spec((1, d)), _const_spec((1, d)), _const_spec((1, d)),
                  _const_spec((d, 2 * cw)), _const_spec((1, 2 * cw)),
                  _const_spec((1, cw)), _const_spec((1, cw)),
                  _const_spec(w_s.shape), _const_spec(b_s.shape), _const_spec((cw, d)),
                  _const_spec((d, f)), _const_spec((d, f)), _const_spec((f, d))],
        out_specs=row,
        out_shape=jax.ShapeDtypeStruct((n, d), F32),
        compiler_params=pltpu.CompilerParams(
            dimension_semantics=("parallel",), vmem_limit_bytes=VMEM_LIMIT),
        name="cm_layer",
    )(x, mods, g_mix, g_ffn, g_final, w_in, b_in, ln_g, ln_b, w_s, b_s, w_out,
      w_gate, w_up, w_down)


def _gate_weights(ga_w, gx_w, ga_b, gx_b):
    _, heads, hd, _ = ga_w.shape
    per = LANES // hd
    nj = heads // per
    w4 = jnp.concatenate([ga_w, gx_w], 0).reshape(4, nj, per, hd, hd)
    eye = jnp.eye(per, dtype=w4.dtype)
    wg = jnp.einsum('kjhde,hg->jhdkge', w4, eye).reshape(nj, LANES, 4 * LANES)
    b4 = jnp.concatenate([ga_b, gx_b], 0).reshape(4, nj, LANES)
    gb = jnp.transpose(b4, (1, 0, 2)).reshape(nj, 1, 4 * LANES)
    return wg.astype(BF16), gb


def kernel(x_prompt, x_sample, state_lru, c, c_ctx, mod_w, mod_b, norm_mix, norm_ffn, lru_w_in, lru_conv_w, lru_conv_b, lru_ga_w, lru_ga_b, lru_gx_w, lru_gx_b, lru_lambda, lru_w_out, cm_w_in, cm_b_in, cm_ln_g, cm_ln_b, cm_w_s, cm_b_s, cm_w_out, ffn_w_gate, ffn_w_up, ffn_w_down, final_norm):
    bp, sp_, d = x_prompt.shape
    bs, ss, _ = x_sample.shape
    n_prompt = bp * sp_
    dims = ((bp, sp_), (bs, ss))
    x = jnp.concatenate([x_prompt.reshape(n_prompt, d), x_sample.reshape(bs * ss, d)], 0)

    cond = jnp.concatenate([c_ctx[None], c, jnp.zeros((COND_ROWS - 1 - bs, d), F32)], 0)
    mods = _adaln(cond, mod_w, mod_b).reshape(-1, N_MOD, d)

    xb, y = _lru_in(x, mods, norm_mix[0:1], lru_w_in[0].astype(BF16), 0, n_prompt, ss)
    wg, gb = _gate_weights(lru_ga_w[0], lru_gx_w[0], lru_ga_b[0], lru_gx_b[0])
    h0s = jnp.transpose(state_lru[:, 0], (1, 0, 2))
    hsum, st = _lru_rec(xb, lru_conv_w[0], lru_conv_b[0:1], wg, gb, lru_lambda[0], h0s, dims)
    x = _lru_out(x, hsum, y, mods, norm_ffn[0:1], lru_w_out[0].astype(BF16),
                 ffn_w_gate[0].astype(BF16), ffn_w_up[0].astype(BF16),
                 ffn_w_down[0].astype(BF16), 0, n_prompt, ss)

    n_groups, chunk, _ = cm_w_s[0].shape
    gd = cm_w_out.shape[1] // n_groups
    bs_tile = jnp.repeat(cm_b_s[0].T, gd, axis=1)
    out = _cm_layer(x, mods, norm_mix[1:2], norm_ffn[1:2], final_norm[None],
                    cm_w_in[0].astype(BF16), cm_b_in[0:1], cm_ln_g[0:1], cm_ln_b[0:1],
                    cm_w_s[0].astype(BF16), bs_tile, cm_w_out[0].astype(BF16),
                    ffn_w_gate[1].astype(BF16), ffn_w_up[1].astype(BF16),
                    ffn_w_down[1].astype(BF16), 1, n_prompt, ss)

    y_prompt = out[:n_prompt].reshape(bp, sp_, d)
    y_sample = out[n_prompt:].reshape(bs, ss, d)
    new_state = jnp.transpose(st, (1, 0, 2))[:, None]
    return (y_prompt, y_sample, new_state)
```

```python
import functools

import jax
import jax.numpy as jnp
from jax import lax
from jax.experimental import pallas as pl
from jax.experimental.pallas import tpu as pltpu

F32 = jnp.float32
BF16 = jnp.bfloat16

EPS = 1e-6
LRU_C = 8.0
N_MOD = 6
COND_ROWS = 16
LANES = 128
SUBLANES = 8
CHUNK = 128
TM = 256
GATE_ROWS = 512
VMEM_LIMIT = 52 * 1024 * 1024


def _rmsnorm(x, g):
    y = x * lax.rsqrt(jnp.mean(x * x, axis=-1, keepdims=True) + EPS)
    return y * g


def _modulated_norm(x, g, shift, scale):
    return _rmsnorm(x, g) * (1.0 + scale) + shift


def _dot(a, b):
    return jnp.dot(a.astype(BF16), b, preferred_element_type=F32)


def _adaln_kernel(c_ref, w_ref, b_ref, o_ref):
    s = jax.nn.silu(c_ref[...])
    o_ref[...] = _dot(s, w_ref[...].astype(BF16)) + b_ref[...]


def _adaln(cond, mod_w, mod_b):
    depth, d, n = mod_w.shape
    tn = 1024
    return pl.pallas_call(
        _adaln_kernel,
        grid=(depth, n // tn),
        in_specs=[
            pl.BlockSpec((COND_ROWS, d), lambda l, j: (0, 0)),
            pl.BlockSpec((None, d, tn), lambda l, j: (l, 0, j)),
            pl.BlockSpec((None, 1, tn), lambda l, j: (l, 0, j)),
        ],
        out_specs=pl.BlockSpec((None, COND_ROWS, tn), lambda l, j: (l, 0, j)),
        out_shape=jax.ShapeDtypeStruct((depth, COND_ROWS, n), F32),
        compiler_params=pltpu.CompilerParams(
            dimension_semantics=("arbitrary", "arbitrary"), vmem_limit_bytes=VMEM_LIMIT),
        name="adaln",
    )(cond, mod_w, mod_b.reshape(depth, 1, n))


class _Tiles:
    def __init__(self, dims):
        (self.bp, self.sp), (self.bs, self.ss) = dims
        assert self.sp % TM == 0 and self.ss % TM == 0
        self.npt = self.bp * self.sp // TM
        self.nst = self.bs * self.ss // TM
        self.n = self.npt + self.nst

    def is_prompt(self):
        return pl.program_id(0) < self.npt

    def combined(self, cols):
        return pl.BlockSpec((TM, cols), lambda i: (i, 0))

    def _pi(self, i):
        return jnp.minimum(i, self.npt - 1)

    def _si(self, i):
        return jnp.maximum(i - self.npt, 0)

    def flat_pair(self, cols):
        return [pl.BlockSpec((TM, cols), lambda i: (self._pi(i), 0)),
                pl.BlockSpec((TM, cols), lambda i: (self._si(i), 0))]

    def padded_pair(self, cols):
        tp, ts = self.sp // TM, self.ss // TM
        return [pl.BlockSpec((None, TM, cols), lambda i: (self._pi(i) // tp, self._pi(i) % tp, 0)),
                pl.BlockSpec((None, TM, cols), lambda i: (self._si(i) // ts, self._si(i) % ts, 0))]

    def mod(self, layer, d):
        def index_map(i):
            row = jnp.where(i < self.npt, 0, 1 + self._si(i) * TM // self.ss)
            return (layer * COND_ROWS + row, 0, 0)
        return pl.BlockSpec((None, N_MOD, d), index_map)

    def read(self, p_ref, s_ref):
        return jnp.where(self.is_prompt(), p_ref[...], s_ref[...])

    def write(self, p_ref, s_ref, value):
        @pl.when(self.is_prompt())
        def _():
            p_ref[...] = value

        @pl.when(jnp.logical_not(self.is_prompt()))
        def _():
            s_ref[...] = value


def _const_spec(shape):
    nd = len(shape)
    return pl.BlockSpec(shape, lambda i: (0,) * nd, pipeline_mode=pl.Buffered(1))


_TOKEN_PARAMS = pltpu.CompilerParams(
    dimension_semantics=("arbitrary",), vmem_limit_bytes=VMEM_LIMIT)


def _lru_in_kernel(tiles, xp_ref, xs_ref, mod_ref, g_ref, w_ref, xbp_ref, xbs_ref, y_ref):
    m = mod_ref[...]
    h = _modulated_norm(tiles.read(xp_ref, xs_ref), g_ref[...], m[0:1], m[1:2])
    z = _dot(h, w_ref[...])
    w = y_ref.shape[-1]
    tiles.write(xbp_ref, xbs_ref, z[:, :w])
    y_ref[...] = jax.nn.gelu(z[:, w:])


def _lru_in(tiles, xp, xs, mods, g, w_in, layer, pad):
    d = xp.shape[1]
    w2 = w_in.shape[1]
    w = w2 // 2
    return pl.pallas_call(
        functools.partial(_lru_in_kernel, tiles),
        grid=(tiles.n,),
        in_specs=tiles.flat_pair(d) + [tiles.mod(layer, d), _const_spec((1, d)),
                                       _const_spec((d, w2))],
        out_specs=tiles.padded_pair(w) + [tiles.combined(w)],
        out_shape=[jax.ShapeDtypeStruct((tiles.bp, tiles.sp + pad, w), F32),
                   jax.ShapeDtypeStruct((tiles.bs, tiles.ss + pad, w), F32),
                   jax.ShapeDtypeStruct((tiles.n * TM, w), F32)],
        compiler_params=_TOKEN_PARAMS,
        name="lru_in",
    )(xp, xs, mods, g, w_in)


def _lru_rec_kernel(B, S, P, has_h0, *refs):
    if has_h0:
        x_ref, cw_ref, cb_ref, wg_ref, gb_ref, lam_ref, h0_ref, out_ref, st_ref = refs[:9]
    else:
        x_ref, cw_ref, cb_ref, wg_ref, gb_ref, lam_ref, out_ref, st_ref = refs[:8]
    xt, a_f, a_b, bx_f, bx_b = refs[-5:]
    rows = B * S

    xt[pl.ds(0, B), :] = jnp.zeros((B, LANES), F32)
    xt[pl.ds(rows + B, 2 * B), :] = jnp.zeros((2 * B, LANES), F32)

    def gather(t, _):
        xt[pl.ds(pl.multiple_of((t + 1) * B, B), B), :] = x_ref[pl.ds(t, B, stride=P), :]
        return 0
    lax.fori_loop(0, S, gather, 0, unroll=4)

    cw = cw_ref[...]
    cb = cb_ref[...]
    gb = gb_ref[...]
    neg = -lam_ref[...]
    sp = jnp.maximum(neg, 0.0) + jnp.log1p(jnp.exp(-jnp.abs(neg)))
    c_neg = (-0.5 * LRU_C) * sp
    c_pos = (0.5 * LRU_C) * sp

    def gates(c, _):
        r0 = pl.multiple_of(c * GATE_ROWS, GATE_ROWS)
        xc = (cw[0:1] * xt[pl.ds(r0, GATE_ROWS), :]
              + cw[1:2] * xt[pl.ds(r0 + B, GATE_ROWS), :]
              + cw[2:3] * xt[pl.ds(r0 + 2 * B, GATE_ROWS), :]
              + cw[3:4] * xt[pl.ds(r0 + 3 * B, GATE_ROWS), :]) + cb
        xh = 0.5 * xc
        g = _dot(xc, wg_ref[...]) + gb
        for z, (a_ref, bx_ref) in enumerate(((a_f, bx_f), (a_b, bx_b))):
            r2 = jnp.tanh(g[:, z * LANES:(z + 1) * LANES]) + 1.0
            i2 = jnp.tanh(g[:, (2 + z) * LANES:(3 + z) * LANES]) + 1.0
            a = jnp.exp(r2 * c_neg[z:z + 1])
            om = jnp.tanh(r2 * c_pos[z:z + 1]) * (a * a + 1.0)
            mult = jnp.where(om > 0.0, om * lax.rsqrt(om), 0.0)
            a_ref[pl.ds(r0, GATE_ROWS), :] = a
            bx_ref[pl.ds(r0, GATE_ROWS), :] = mult * (i2 * xh)
        return 0
    lax.fori_loop(0, rows // GATE_ROWS, gates, 0)

    def scan(t, carry):
        hf, hb = carry
        rf = pl.multiple_of(t * B, B)
        hf = a_f[pl.ds(rf, B), :] * hf + bx_f[pl.ds(rf, B), :]
        bx_f[pl.ds(rf, B), :] = hf
        rb = pl.multiple_of((S - 1 - t) * B, B)
        hb = a_b[pl.ds(rb, B), :] * hb + bx_b[pl.ds(rb, B), :]
        bx_b[pl.ds(rb, B), :] = hb
        return hf, hb
    if has_h0:
        h0 = (h0_ref[0], h0_ref[1])
    else:
        h0 = (jnp.zeros((B, LANES), F32),) * 2
    hf, hb = lax.fori_loop(0, S, scan, h0, unroll=4)
    st_ref[0] = hf
    st_ref[1] = hb

    def scatter(t, _):
        r = pl.multiple_of(t * B, B)
        out_ref[pl.ds(t, B, stride=P), :] = bx_f[pl.ds(r, B), :] + bx_b[pl.ds(r, B), :]
        return 0
    lax.fori_loop(0, S, scatter, 0, unroll=4)


def _lru_rec(xb, conv_w, conv_b, wg, gb, lam, h0):
    B, P, w = xb.shape
    S = P - SUBLANES
    rows = B * S
    nj = w // LANES
    has_h0 = h0 is not None
    slab = pl.BlockSpec((B * P, LANES), lambda j: (0, j))
    state = pl.BlockSpec((2, B, LANES), lambda j: (0, 0, j))
    in_specs = [
        slab,
        pl.BlockSpec((4, LANES), lambda j: (0, j)),
        pl.BlockSpec((1, LANES), lambda j: (0, j)),
        pl.BlockSpec((None, LANES, 4 * LANES), lambda j: (j, 0, 0)),
        pl.BlockSpec((None, 1, 4 * LANES), lambda j: (j, 0, 0)),
        pl.BlockSpec((2, LANES), lambda j: (0, j)),
    ]
    args = [xb.reshape(B * P, w), conv_w, conv_b, wg, gb, lam]
    if has_h0:
        in_specs.append(state)
        args.append(h0)
    out, st = pl.pallas_call(
        functools.partial(_lru_rec_kernel, B, S, P, has_h0),
        grid=(nj,),
        in_specs=in_specs,
        out_specs=[slab, state],
        out_shape=[jax.ShapeDtypeStruct((B * P, w), F32), jax.ShapeDtypeStruct((2, B, w), F32)],
        scratch_shapes=[pltpu.VMEM((rows + 3 * B, LANES), F32)] + [pltpu.VMEM((rows, LANES), F32)] * 4,
        compiler_params=pltpu.CompilerParams(
            dimension_semantics=("arbitrary",), vmem_limit_bytes=VMEM_LIMIT),
        name="lru_rec",
    )(*args)
    return out.reshape(B, P, w), st


def _ffn(x1, m, g, wg_ref, wu_ref, wd_ref):
    h = _modulated_norm(x1, g, m[3:4], m[4:5]).astype(BF16)
    gate = jnp.dot(h, wg_ref[...], preferred_element_type=F32)
    up = jnp.dot(h, wu_ref[...], preferred_element_type=F32)
    return x1 + m[5:6] * _dot(jax.nn.silu(gate) * up, wd_ref[...])


def _lru_out_kernel(tiles, xp_ref, xs_ref, hp_ref, hs_ref, y_ref, mod_ref, g_ref,
                    wo_ref, wg_ref, wu_ref, wd_ref, o_ref):
    m = mod_ref[...]
    hsum = tiles.read(hp_ref, hs_ref)
    x1 = tiles.read(xp_ref, xs_ref) + m[2:3] * _dot(hsum * y_ref[...], wo_ref[...])
    o_ref[...] = _ffn(x1, m, g_ref[...], wg_ref, wu_ref, wd_ref)


def _lru_out(tiles, xp, xs, hp, hs, y, mods, g, w_out, w_gate, w_up, w_down, layer):
    d = xp.shape[1]
    w = y.shape[1]
    f = w_gate.shape[1]
    return pl.pallas_call(
        functools.partial(_lru_out_kernel, tiles),
        grid=(tiles.n,),
        in_specs=tiles.flat_pair(d) + tiles.padded_pair(w) + [
            tiles.combined(w), tiles.mod(layer, d), _const_spec((1, d)), _const_spec((w, d)),
            _const_spec((d, f)), _const_spec((d, f)), _const_spec((f, d))],
        out_specs=tiles.combined(d),
        out_shape=jax.ShapeDtypeStruct((tiles.n * TM, d), F32),
        compiler_params=_TOKEN_PARAMS,
        name="lru_out_ffn",
    )(xp, xs, hp, hs, y, mods, g, w_out, w_gate, w_up, w_down)


def _cm_kernel(tiles, x_ref, mod_ref, gm_ref, gf_ref, gfin_ref, wi_ref, bi_ref, lng_ref, lnb_ref,
               ws_ref, bs_ref, wo_ref, wg_ref, wu_ref, wd_ref, op_ref, os_ref):
    m = mod_ref[...]
    x = x_ref[...]
    cw = lng_ref.shape[-1]
    n_groups = ws_ref.shape[0]
    gd = cw // n_groups
    h = _modulated_norm(x, gm_ref[...], m[0:1], m[1:2])
    uv = jax.nn.gelu(_dot(h, wi_ref[...]) + bi_ref[...])
    u = uv[:, :cw]
    v = uv[:, cw:]
    mu = jnp.mean(v, axis=-1, keepdims=True)
    var = jnp.mean(jnp.square(v - mu), axis=-1, keepdims=True)
    v = ((v - mu) * lax.rsqrt(var + EPS) * lng_ref[...] + lnb_ref[...]).astype(BF16)
    mixed_rows = []
    for c in range(TM // CHUNK):
        vc = v[c * CHUNK:(c + 1) * CHUNK]
        cols = [jnp.dot(ws_ref[g], vc[:, g * gd:(g + 1) * gd], preferred_element_type=F32)
                for g in range(n_groups)]
        mixed_rows.append(jnp.concatenate(cols, axis=1) + bs_ref[...])
    mixed = jnp.concatenate(mixed_rows, axis=0)
    x1 = x + m[2:3] * _dot(u * mixed, wo_ref[...])
    x2 = _ffn(x1, m, gf_ref[...], wg_ref, wu_ref, wd_ref)
    tiles.write(op_ref, os_ref, _rmsnorm(x2, gfin_ref[...]))


def _cm_layer(tiles, x, mods, g_mix, g_ffn, g_final, w_in, b_in, ln_g, ln_b, w_s, b_s, w_out,
              w_gate, w_up, w_down, layer):
    d = x.shape[1]
    cw = w_out.shape[0]
    f = w_gate.shape[1]
    return pl.pallas_call(
        functools.partial(_cm_kernel, tiles),
        grid=(tiles.n,),
        in_specs=[tiles.combined(d), tiles.mod(layer, d),
                  _const_spec((1, d)), _const_spec((1, d)), _const_spec((1, d)),
                  _const_spec((d, 2 * cw)), _const_spec((1, 2 * cw)),
                  _const_spec((1, cw)), _const_spec((1, cw)),
                  _const_spec(w_s.shape), _const_spec(b_s.shape), _const_spec((cw, d)),
                  _const_spec((d, f)), _const_spec((d, f)), _const_spec((f, d))],
        out_specs=tiles.flat_pair(d),
        out_shape=[jax.ShapeDtypeStruct((tiles.npt * TM, d), F32),
                   jax.ShapeDtypeStruct((tiles.nst * TM, d), F32)],
        compiler_params=_TOKEN_PARAMS,
        name="cm_layer",
    )(x, mods, g_mix, g_ffn, g_final, w_in, b_in, ln_g, ln_b, w_s, b_s, w_out,
      w_gate, w_up, w_down)


def _gate_weights(ga_w, gx_w, ga_b, gx_b):
    _, heads, hd, _ = ga_w.shape
    per = LANES // hd
    nj = heads // per
    w4 = 0.5 * jnp.concatenate([ga_w, gx_w], 0).reshape(4, nj, per, hd, hd)
    eye = jnp.eye(per, dtype=w4.dtype)
    wg = jnp.einsum('kjhde,hg->jhdkge', w4, eye).reshape(nj, LANES, 4 * LANES)
    b4 = 0.5 * jnp.concatenate([ga_b, gx_b], 0).reshape(4, nj, LANES)
    gb = jnp.transpose(b4, (1, 0, 2)).reshape(nj, 1, 4 * LANES)
    return wg.astype(BF16), gb


def kernel(x_prompt, x_sample, state_lru, c, c_ctx, mod_w, mod_b, norm_mix, norm_ffn, lru_w_in, lru_conv_w, lru_conv_b, lru_ga_w, lru_ga_b, lru_gx_w, lru_gx_b, lru_lambda, lru_w_out, cm_w_in, cm_b_in, cm_ln_g, cm_ln_b, cm_w_s, cm_b_s, cm_w_out, ffn_w_gate, ffn_w_up, ffn_w_down, final_norm):
    bp, sp, d = x_prompt.shape
    bs, ss, _ = x_sample.shape
    tiles = _Tiles(((bp, sp), (bs, ss)))
    xp = x_prompt.reshape(bp * sp, d)
    xs = x_sample.reshape(bs * ss, d)

    cond = jnp.concatenate([c_ctx[None], c, jnp.zeros((COND_ROWS - 1 - bs, d), F32)], 0)
    mods = _adaln(cond, mod_w, mod_b).reshape(-1, N_MOD, d)

    xbp, xbs, y = _lru_in(tiles, xp, xs, mods, norm_mix[0:1], lru_w_in[0].astype(BF16), 0, SUBLANES)
    wg, gb = _gate_weights(lru_ga_w[0], lru_gx_w[0], lru_ga_b[0], lru_gx_b[0])
    rec = (lru_conv_w[0], lru_conv_b[0:1], wg, gb, lru_lambda[0])
    hp, st = _lru_rec(xbp, *rec, None)
    hs, _ = _lru_rec(xbs, *rec, jnp.transpose(state_lru[:, 0], (1, 0, 2)))
    x = _lru_out(tiles, xp, xs, hp, hs, y, mods, norm_ffn[0:1], lru_w_out[0].astype(BF16),
                 ffn_w_gate[0].astype(BF16), ffn_w_up[0].astype(BF16),
                 ffn_w_down[0].astype(BF16), 0)

    n_groups = cm_w_s.shape[1]
    gd = cm_w_out.shape[1] // n_groups
    bs_tile = jnp.repeat(cm_b_s[0].T, gd, axis=1)
    yp, ys = _cm_layer(tiles, x, mods, norm_mix[1:2], norm_ffn[1:2], final_norm[None],
                       cm_w_in[0].astype(BF16), cm_b_in[0:1], cm_ln_g[0:1], cm_ln_b[0:1],
                       cm_w_s[0].astype(BF16), bs_tile, cm_w_out[0].astype(BF16),
                       ffn_w_gate[1].astype(BF16), ffn_w_up[1].astype(BF16),
                       ffn_w_down[1].astype(BF16), 1)

    new_state = jnp.transpose(st, (1, 0, 2))[:, None]
    return (yp.reshape(bp, sp, d), ys.reshape(bs, ss, d), new_state)
```

```python
import functools

import jax
import jax.numpy as jnp
from jax import lax
from jax.experimental import pallas as pl
from jax.experimental.pallas import tpu as pltpu

F32 = jnp.float32
BF16 = jnp.bfloat16

EPS = 1e-6
LRU_C = 8.0
N_MOD = 6
COND_ROWS = 16
LANES = 128
SUBLANES = 8
CHUNK = 128
TM = 512
SUB = 256
GATE_ROWS = 512
VMEM_LIMIT = 52 * 1024 * 1024


def _rmsnorm(x, g):
    y = x * lax.rsqrt(jnp.mean(x * x, axis=-1, keepdims=True) + EPS)
    return y * g


def _modulated_norm(x, g, shift, scale):
    return _rmsnorm(x, g) * (1.0 + scale) + shift


def _dot(a, b):
    return jnp.dot(a.astype(BF16), b, preferred_element_type=F32)


def _adaln_kernel(c_ref, w_ref, b_ref, o_ref):
    s = jax.nn.silu(c_ref[...])
    o_ref[...] = _dot(s, w_ref[...].astype(BF16)) + b_ref[...]


def _adaln(cond, mod_w, mod_b):
    depth, d, n = mod_w.shape
    tn = 1024
    return pl.pallas_call(
        _adaln_kernel,
        grid=(depth, n // tn),
        in_specs=[
            pl.BlockSpec((COND_ROWS, d), lambda l, j: (0, 0)),
            pl.BlockSpec((None, d, tn), lambda l, j: (l, 0, j)),
            pl.BlockSpec((None, 1, tn), lambda l, j: (l, 0, j)),
        ],
        out_specs=pl.BlockSpec((None, COND_ROWS, tn), lambda l, j: (l, 0, j)),
        out_shape=jax.ShapeDtypeStruct((depth, COND_ROWS, n), F32),
        compiler_params=pltpu.CompilerParams(
            dimension_semantics=("arbitrary", "arbitrary"), vmem_limit_bytes=VMEM_LIMIT),
        name="adaln",
    )(cond, mod_w, mod_b.reshape(depth, 1, n))


class _Tiles:
    def __init__(self, dims):
        (self.bp, self.sp), (self.bs, self.ss) = dims
        assert TM % self.sp == 0 and self.ss % TM == 0 and (self.bp * self.sp) % TM == 0
        self.npt = self.bp * self.sp // TM
        self.nst = self.bs * self.ss // TM
        self.n = self.npt + self.nst

    def is_prompt(self):
        return pl.program_id(0) < self.npt

    def combined(self, cols):
        return pl.BlockSpec((TM, cols), lambda i: (i, 0))

    def _pi(self, i):
        return jnp.minimum(i, self.npt - 1)

    def _si(self, i):
        return jnp.maximum(i - self.npt, 0)

    def flat_pair(self, cols):
        return [pl.BlockSpec((TM, cols), lambda i: (self._pi(i), 0)),
                pl.BlockSpec((TM, cols), lambda i: (self._si(i), 0))]

    def padded_pair(self, cols):
        ts = self.ss // TM
        return [pl.BlockSpec((TM // self.sp, self.sp, cols), lambda i: (self._pi(i), 0, 0)),
                pl.BlockSpec((None, TM, cols), lambda i: (self._si(i) // ts, self._si(i) % ts, 0))]

    def mod(self, layer, d):
        def index_map(i):
            row = jnp.where(i < self.npt, 0, 1 + self._si(i) * TM // self.ss)
            return (layer * COND_ROWS + row, 0, 0)
        return pl.BlockSpec((None, N_MOD, d), index_map)

    def read(self, p_ref, s_ref):
        return jnp.where(self.is_prompt(), p_ref[...].reshape(s_ref.shape), s_ref[...])

    def write(self, p_ref, s_ref, value):
        @pl.when(self.is_prompt())
        def _():
            p_ref[...] = value.reshape(p_ref.shape)

        @pl.when(jnp.logical_not(self.is_prompt()))
        def _():
            s_ref[...] = value


def _const_spec(shape):
    nd = len(shape)
    return pl.BlockSpec(shape, lambda i: (0,) * nd, pipeline_mode=pl.Buffered(1))


_TOKEN_PARAMS = pltpu.CompilerParams(
    dimension_semantics=("arbitrary",), vmem_limit_bytes=VMEM_LIMIT)


def _lru_in_kernel(tiles, xp_ref, xs_ref, mod_ref, g_ref, w_ref, xbp_ref, xbs_ref, y_ref):
    m = mod_ref[...]
    x = tiles.read(xp_ref, xs_ref)
    w = y_ref.shape[-1]
    xb = []
    for s0 in range(0, TM, SUB):
        h = _modulated_norm(x[s0:s0 + SUB], g_ref[...], m[0:1], m[1:2])
        z = _dot(h, w_ref[...])
        xb.append(z[:, :w])
        y_ref[s0:s0 + SUB, :] = jax.nn.gelu(z[:, w:])
    tiles.write(xbp_ref, xbs_ref, jnp.concatenate(xb, axis=0))


def _lru_in(tiles, xp, xs, mods, g, w_in, layer, pad):
    d = xp.shape[1]
    w2 = w_in.shape[1]
    w = w2 // 2
    return pl.pallas_call(
        functools.partial(_lru_in_kernel, tiles),
        grid=(tiles.n,),
        in_specs=tiles.flat_pair(d) + [tiles.mod(layer, d), _const_spec((1, d)),
                                       _const_spec((d, w2))],
        out_specs=tiles.padded_pair(w) + [tiles.combined(w)],
        out_shape=[jax.ShapeDtypeStruct((tiles.bp, tiles.sp + pad, w), F32),
                   jax.ShapeDtypeStruct((tiles.bs, tiles.ss + pad, w), F32),
                   jax.ShapeDtypeStruct((tiles.n * TM, w), F32)],
        compiler_params=_TOKEN_PARAMS,
        name="lru_in",
    )(xp, xs, mods, g, w_in)


def _lru_rec_kernel(B, S, P, has_h0, *refs):
    if has_h0:
        x_ref, cw_ref, cb_ref, wg_ref, gb_ref, lam_ref, h0_ref, out_ref, st_ref = refs[:9]
    else:
        x_ref, cw_ref, cb_ref, wg_ref, gb_ref, lam_ref, out_ref, st_ref = refs[:8]
    xt, a_f, a_b, bx_f, bx_b = refs[-5:]
    rows = B * S

    xt[pl.ds(0, B), :] = jnp.zeros((B, LANES), F32)
    xt[pl.ds(rows + B, 2 * B), :] = jnp.zeros((2 * B, LANES), F32)

    def gather(t, _):
        xt[pl.ds(pl.multiple_of((t + 1) * B, B), B), :] = x_ref[pl.ds(t, B, stride=P), :]
        return 0
    lax.fori_loop(0, S, gather, 0, unroll=4)

    cw = cw_ref[...]
    cb = cb_ref[...]
    gb = gb_ref[...]
    neg = -lam_ref[...]
    sp = jnp.maximum(neg, 0.0) + jnp.log1p(jnp.exp(-jnp.abs(neg)))
    c_neg = (-0.5 * LRU_C) * sp
    c_pos = (0.5 * LRU_C) * sp

    def gates(c, _):
        r0 = pl.multiple_of(c * GATE_ROWS, GATE_ROWS)
        xc = (cw[0:1] * xt[pl.ds(r0, GATE_ROWS), :]
              + cw[1:2] * xt[pl.ds(r0 + B, GATE_ROWS), :]
              + cw[2:3] * xt[pl.ds(r0 + 2 * B, GATE_ROWS), :]
              + cw[3:4] * xt[pl.ds(r0 + 3 * B, GATE_ROWS), :]) + cb
        xh = 0.5 * xc
        g = _dot(xc, wg_ref[...]) + gb
        for z, (a_ref, bx_ref) in enumerate(((a_f, bx_f), (a_b, bx_b))):
            r2 = jnp.tanh(g[:, z * LANES:(z + 1) * LANES]) + 1.0
            i2 = jnp.tanh(g[:, (2 + z) * LANES:(3 + z) * LANES]) + 1.0
            a = jnp.exp(r2 * c_neg[z:z + 1])
            om = jnp.tanh(r2 * c_pos[z:z + 1]) * (a * a + 1.0)
            mult = jnp.where(om > 0.0, om * lax.rsqrt(om), 0.0)
            a_ref[pl.ds(r0, GATE_ROWS), :] = a
            bx_ref[pl.ds(r0, GATE_ROWS), :] = mult * (i2 * xh)
        return 0
    lax.fori_loop(0, rows // GATE_ROWS, gates, 0)

    def scan(t, carry):
        hf, hb = carry
        rf = pl.multiple_of(t * B, B)
        hf = a_f[pl.ds(rf, B), :] * hf + bx_f[pl.ds(rf, B), :]
        bx_f[pl.ds(rf, B), :] = hf
        rb = pl.multiple_of((S - 1 - t) * B, B)
        hb = a_b[pl.ds(rb, B), :] * hb + bx_b[pl.ds(rb, B), :]
        bx_b[pl.ds(rb, B), :] = hb
        return hf, hb
    if has_h0:
        h0 = (h0_ref[0], h0_ref[1])
    else:
        h0 = (jnp.zeros((B, LANES), F32),) * 2
    hf, hb = lax.fori_loop(0, S, scan, h0, unroll=4)
    st_ref[0] = hf
    st_ref[1] = hb

    def scatter(t, _):
        r = pl.multiple_of(t * B, B)
        out_ref[pl.ds(t, B, stride=P), :] = bx_f[pl.ds(r, B), :] + bx_b[pl.ds(r, B), :]
        return 0
    lax.fori_loop(0, S, scatter, 0, unroll=4)


def _lru_rec(xb, conv_w, conv_b, wg, gb, lam, h0):
    B, P, w = xb.shape
    S = P - SUBLANES
    rows = B * S
    nj = w // LANES
    has_h0 = h0 is not None
    slab = pl.BlockSpec((B * P, LANES), lambda j: (0, j))
    state = pl.BlockSpec((2, B, LANES), lambda j: (0, 0, j))
    in_specs = [
        slab,
        pl.BlockSpec((4, LANES), lambda j: (0, j)),
        pl.BlockSpec((1, LANES), lambda j: (0, j)),
        pl.BlockSpec((None, LANES, 4 * LANES), lambda j: (j, 0, 0)),
        pl.BlockSpec((None, 1, 4 * LANES), lambda j: (j, 0, 0)),
        pl.BlockSpec((2, LANES), lambda j: (0, j)),
    ]
    args = [xb.reshape(B * P, w), conv_w, conv_b, wg, gb, lam]
    if has_h0:
        in_specs.append(state)
        args.append(h0)
    out, st = pl.pallas_call(
        functools.partial(_lru_rec_kernel, B, S, P, has_h0),
        grid=(nj,),
        in_specs=in_specs,
        out_specs=[slab, state],
        out_shape=[jax.ShapeDtypeStruct((B * P, w), F32), jax.ShapeDtypeStruct((2, B, w), F32)],
        scratch_shapes=[pltpu.VMEM((rows + 3 * B, LANES), F32)] + [pltpu.VMEM((rows, LANES), F32)] * 4,
        compiler_params=pltpu.CompilerParams(
            dimension_semantics=("arbitrary",), vmem_limit_bytes=VMEM_LIMIT),
        name="lru_rec",
    )(*args)
    return out.reshape(B, P, w), st


def _ffn(x1, m, g, wg_ref, wu_ref, wd_ref):
    h = _modulated_norm(x1, g, m[3:4], m[4:5]).astype(BF16)
    gate = jnp.dot(h, wg_ref[...], preferred_element_type=F32)
    up = jnp.dot(h, wu_ref[...], preferred_element_type=F32)
    return x1 + m[5:6] * _dot(jax.nn.silu(gate) * up, wd_ref[...])


def _lru_out_kernel(tiles, xp_ref, xs_ref, hp_ref, hs_ref, y_ref, mod_ref, g_ref,
                    wo_ref, wg_ref, wu_ref, wd_ref, o_ref):
    m = mod_ref[...]
    hsum = tiles.read(hp_ref, hs_ref)
    x = tiles.read(xp_ref, xs_ref)
    for s0 in range(0, TM, SUB):
        rows = slice(s0, s0 + SUB)
        x1 = x[rows] + m[2:3] * _dot(hsum[rows] * y_ref[rows, :], wo_ref[...])
        o_ref[rows, :] = _ffn(x1, m, g_ref[...], wg_ref, wu_ref, wd_ref)


def _lru_out(tiles, xp, xs, hp, hs, y, mods, g, w_out, w_gate, w_up, w_down, layer):
    d = xp.shape[1]
    w = y.shape[1]
    f = w_gate.shape[1]
    return pl.pallas_call(
        functools.partial(_lru_out_kernel, tiles),
        grid=(tiles.n,),
        in_specs=tiles.flat_pair(d) + tiles.padded_pair(w) + [
            tiles.combined(w), tiles.mod(layer, d), _const_spec((1, d)), _const_spec((w, d)),
            _const_spec((d, f)), _const_spec((d, f)), _const_spec((f, d))],
        out_specs=tiles.combined(d),
        out_shape=jax.ShapeDtypeStruct((tiles.n * TM, d), F32),
        compiler_params=_TOKEN_PARAMS,
        name="lru_out_ffn",
    )(xp, xs, hp, hs, y, mods, g, w_out, w_gate, w_up, w_down)


def _cm_kernel(tiles, x_ref, mod_ref, gm_ref, gf_ref, gfin_ref, wi_ref, bi_ref, lng_ref, lnb_ref,
               ws_ref, bs_ref, wo_ref, wg_ref, wu_ref, wd_ref, op_ref, os_ref):
    m = mod_ref[...]
    cw = lng_ref.shape[-1]
    n_groups = ws_ref.shape[0]
    gd = cw // n_groups
    outs = []
    for s0 in range(0, TM, SUB):
        x = x_ref[s0:s0 + SUB, :]
        h = _modulated_norm(x, gm_ref[...], m[0:1], m[1:2])
        uv = jax.nn.gelu(_dot(h, wi_ref[...]) + bi_ref[...])
        u = uv[:, :cw]
        v = uv[:, cw:]
        mu = jnp.mean(v, axis=-1, keepdims=True)
        var = jnp.mean(jnp.square(v - mu), axis=-1, keepdims=True)
        v = ((v - mu) * lax.rsqrt(var + EPS) * lng_ref[...] + lnb_ref[...]).astype(BF16)
        mixed_rows = []
        for c in range(SUB // CHUNK):
            vc = v[c * CHUNK:(c + 1) * CHUNK]
            cols = [jnp.dot(ws_ref[g], vc[:, g * gd:(g + 1) * gd], preferred_element_type=F32)
                    for g in range(n_groups)]
            mixed_rows.append(jnp.concatenate(cols, axis=1) + bs_ref[...])
        mixed = jnp.concatenate(mixed_rows, axis=0)
        x1 = x + m[2:3] * _dot(u * mixed, wo_ref[...])
        x2 = _ffn(x1, m, gf_ref[...], wg_ref, wu_ref, wd_ref)
        outs.append(_rmsnorm(x2, gfin_ref[...]))
    tiles.write(op_ref, os_ref, jnp.concatenate(outs, axis=0))


def _cm_layer(tiles, x, mods, g_mix, g_ffn, g_final, w_in, b_in, ln_g, ln_b, w_s, b_s, w_out,
              w_gate, w_up, w_down, layer):
    d = x.shape[1]
    cw = w_out.shape[0]
    f = w_gate.shape[1]
    return pl.pallas_call(
        functools.partial(_cm_kernel, tiles),
        grid=(tiles.n,),
        in_specs=[tiles.combined(d), tiles.mod(layer, d),
                  _const_spec((1, d)), _const_spec((1, d)), _const_spec((1, d)),
                  _const_spec((d, 2 * cw)), _const_spec((1, 2 * cw)),
                  _const_spec((1, cw)), _const_spec((1, cw)),
                  _const_spec(w_s.shape), _const_spec(b_s.shape), _const_spec((cw, d)),
                  _const_spec((d, f)), _const_spec((d, f)), _const_spec((f, d))],
        out_specs=tiles.flat_pair(d),
        out_shape=[jax.ShapeDtypeStruct((tiles.npt * TM, d), F32),
                   jax.ShapeDtypeStruct((tiles.nst * TM, d), F32)],
        compiler_params=_TOKEN_PARAMS,
        name="cm_layer",
    )(x, mods, g_mix, g_ffn, g_final, w_in, b_in, ln_g, ln_b, w_s, b_s, w_out,
      w_gate, w_up, w_down)


def _gate_weights(ga_w, gx_w, ga_b, gx_b):
    _, heads, hd, _ = ga_w.shape
    per = LANES // hd
    nj = heads // per
    w4 = 0.5 * jnp.concatenate([ga_w, gx_w], 0).reshape(4, nj, per, hd, hd)
    eye = jnp.eye(per, dtype=w4.dtype)
    wg = jnp.einsum('kjhde,hg->jhdkge', w4, eye).reshape(nj, LANES, 4 * LANES)
    b4 = 0.5 * jnp.concatenate([ga_b, gx_b], 0).reshape(4, nj, LANES)
    gb = jnp.transpose(b4, (1, 0, 2)).reshape(nj, 1, 4 * LANES)
    return wg.astype(BF16), gb


def kernel(x_prompt, x_sample, state_lru, c, c_ctx, mod_w, mod_b, norm_mix, norm_ffn, lru_w_in, lru_conv_w, lru_conv_b, lru_ga_w, lru_ga_b, lru_gx_w, lru_gx_b, lru_lambda, lru_w_out, cm_w_in, cm_b_in, cm_ln_g, cm_ln_b, cm_w_s, cm_b_s, cm_w_out, ffn_w_gate, ffn_w_up, ffn_w_down, final_norm):
    bp, sp, d = x_prompt.shape
    bs, ss, _ = x_sample.shape
    tiles = _Tiles(((bp, sp), (bs, ss)))
    xp = x_prompt.reshape(bp * sp, d)
    xs = x_sample.reshape(bs * ss, d)

    cond = jnp.concatenate([c_ctx[None], c, jnp.zeros((COND_ROWS - 1 - bs, d), F32)], 0)
    mods = _adaln(cond, mod_w, mod_b).reshape(-1, N_MOD, d)

    xbp, xbs, y = _lru_in(tiles, xp, xs, mods, norm_mix[0:1], lru_w_in[0].astype(BF16), 0, SUBLANES)
    wg, gb = _gate_weights(lru_ga_w[0], lru_gx_w[0], lru_ga_b[0], lru_gx_b[0])
    rec = (lru_conv_w[0], lru_conv_b[0:1], wg, gb, lru_lambda[0])
    hp, st = _lru_rec(xbp, *rec, None)
    hs, _ = _lru_rec(xbs, *rec, jnp.transpose(state_lru[:, 0], (1, 0, 2)))
    x = _lru_out(tiles, xp, xs, hp, hs, y, mods, norm_ffn[0:1], lru_w_out[0].astype(BF16),
                 ffn_w_gate[0].astype(BF16), ffn_w_up[0].astype(BF16),
                 ffn_w_down[0].astype(BF16), 0)

    n_groups = cm_w_s.shape[1]
    gd = cm_w_out.shape[1] // n_groups
    bs_tile = jnp.repeat(cm_b_s[0].T, gd, axis=1)
    yp, ys = _cm_layer(tiles, x, mods, norm_mix[1:2], norm_ffn[1:2], final_norm[None],
                       cm_w_in[0].astype(BF16), cm_b_in[0:1], cm_ln_g[0:1], cm_ln_b[0:1],
                       cm_w_s[0].astype(BF16), bs_tile, cm_w_out[0].astype(BF16),
                       ffn_w_gate[1].astype(BF16), ffn_w_up[1].astype(BF16),
                       ffn_w_down[1].astype(BF16), 1)

    new_state = jnp.transpose(st, (1, 0, 2))[:, None]
    return (yp.reshape(bp, sp, d), ys.reshape(bs, ss, d), new_state)
```

```python
import functools

import jax
import jax.numpy as jnp
from jax import lax
from jax.experimental import pallas as pl
from jax.experimental.pallas import tpu as pltpu

F32 = jnp.float32
BF16 = jnp.bfloat16

EPS = 1e-6
LRU_C = 8.0
LOG2_E = 1.4426950408889634
N_MOD = 6
COND_ROWS = 16
LANES = 128
SUBLANES = 8
CHUNK = 128
TM = 512
SUB = 256
GATE_ROWS = 512
VMEM_LIMIT = 52 * 1024 * 1024


def _rmsnorm(x, g):
    y = x * lax.rsqrt(jnp.mean(x * x, axis=-1, keepdims=True) + EPS)
    return y * g


def _modulated_norm(x, g, shift, scale):
    return _rmsnorm(x, g) * (1.0 + scale) + shift


def _dot(a, b):
    return jnp.dot(a.astype(BF16), b, preferred_element_type=F32)


def _adaln_kernel(c_ref, w_ref, b_ref, o_ref):
    s = jax.nn.silu(c_ref[...])
    o_ref[...] = _dot(s, w_ref[...].astype(BF16)) + b_ref[...]


def _adaln(cond, mod_w, mod_b):
    depth, d, n = mod_w.shape
    tn = 1024
    return pl.pallas_call(
        _adaln_kernel,
        grid=(depth, n // tn),
        in_specs=[
            pl.BlockSpec((COND_ROWS, d), lambda l, j: (0, 0)),
            pl.BlockSpec((None, d, tn), lambda l, j: (l, 0, j)),
            pl.BlockSpec((None, 1, tn), lambda l, j: (l, 0, j)),
        ],
        out_specs=pl.BlockSpec((None, COND_ROWS, tn), lambda l, j: (l, 0, j)),
        out_shape=jax.ShapeDtypeStruct((depth, COND_ROWS, n), F32),
        compiler_params=pltpu.CompilerParams(
            dimension_semantics=("arbitrary", "arbitrary"), vmem_limit_bytes=VMEM_LIMIT),
        name="adaln",
    )(cond, mod_w, mod_b.reshape(depth, 1, n))


class _Tiles:
    def __init__(self, dims):
        (self.bp, self.sp), (self.bs, self.ss) = dims
        assert TM % self.sp == 0 and self.ss % TM == 0 and (self.bp * self.sp) % TM == 0
        self.npt = self.bp * self.sp // TM
        self.nst = self.bs * self.ss // TM
        self.n = self.npt + self.nst

    def is_prompt(self):
        return pl.program_id(0) < self.npt

    def combined(self, cols):
        return pl.BlockSpec((TM, cols), lambda i: (i, 0))

    def _pi(self, i):
        return jnp.minimum(i, self.npt - 1)

    def _si(self, i):
        return jnp.maximum(i - self.npt, 0)

    def flat_pair(self, cols):
        return [pl.BlockSpec((TM, cols), lambda i: (self._pi(i), 0)),
                pl.BlockSpec((TM, cols), lambda i: (self._si(i), 0))]

    def padded_pair(self, cols):
        ts = self.ss // TM
        return [pl.BlockSpec((TM // self.sp, self.sp, cols), lambda i: (self._pi(i), 0, 0)),
                pl.BlockSpec((None, TM, cols), lambda i: (self._si(i) // ts, self._si(i) % ts, 0))]

    def padded_out_pair(self, cols, pad):
        ts = self.ss // TM
        return [pl.BlockSpec((TM // self.sp, self.sp + pad, cols), lambda i: (self._pi(i), 0, 0)),
                pl.BlockSpec((None, self.ss + pad, cols), lambda i: (self._si(i) // ts, 0, 0))]

    def write_padded(self, p_ref, s_ref, value):
        cols = value.shape[-1]
        pad = p_ref.shape[1] - self.sp

        @pl.when(self.is_prompt())
        def _():
            p_ref[:, :self.sp, :] = value.reshape(TM // self.sp, self.sp, cols)
            p_ref[:, self.sp:, :] = jnp.zeros((TM // self.sp, pad, cols), value.dtype)

        @pl.when(jnp.logical_not(self.is_prompt()))
        def _():
            part = self._si(pl.program_id(0)) % (self.ss // TM)
            s_ref[pl.ds(pl.multiple_of(part * TM, TM), TM), :] = value
            s_ref[self.ss:, :] = jnp.zeros((pad, cols), value.dtype)

    def mod(self, layer, d):
        def index_map(i):
            row = jnp.where(i < self.npt, 0, 1 + self._si(i) * TM // self.ss)
            return (layer * COND_ROWS + row, 0, 0)
        return pl.BlockSpec((None, N_MOD, d), index_map)

    def read(self, p_ref, s_ref):
        return jnp.where(self.is_prompt(), p_ref[...].reshape(s_ref.shape), s_ref[...])

    def write(self, p_ref, s_ref, value):
        @pl.when(self.is_prompt())
        def _():
            p_ref[...] = value.reshape(p_ref.shape)

        @pl.when(jnp.logical_not(self.is_prompt()))
        def _():
            s_ref[...] = value


def _const_spec(shape):
    nd = len(shape)
    return pl.BlockSpec(shape, lambda i: (0,) * nd, pipeline_mode=pl.Buffered(1))


_TOKEN_PARAMS = pltpu.CompilerParams(
    dimension_semantics=("arbitrary",), vmem_limit_bytes=VMEM_LIMIT)


def _lru_in_kernel(tiles, xp_ref, xs_ref, mod_ref, g_ref, w_ref, xbp_ref, xbs_ref, y_ref):
    m = mod_ref[...]
    x = tiles.read(xp_ref, xs_ref)
    w = y_ref.shape[-1]
    xb = []
    for s0 in range(0, TM, SUB):
        h = _modulated_norm(x[s0:s0 + SUB], g_ref[...], m[0:1], m[1:2])
        z = _dot(h, w_ref[...])
        xb.append(z[:, :w])
        y_ref[s0:s0 + SUB, :] = jax.nn.gelu(z[:, w:])
    tiles.write_padded(xbp_ref, xbs_ref, jnp.concatenate(xb, axis=0))


def _lru_in(tiles, xp, xs, mods, g, w_in, layer, pad):
    d = xp.shape[1]
    w2 = w_in.shape[1]
    w = w2 // 2
    return pl.pallas_call(
        functools.partial(_lru_in_kernel, tiles),
        grid=(tiles.n,),
        in_specs=tiles.flat_pair(d) + [tiles.mod(layer, d), _const_spec((1, d)),
                                       _const_spec((d, w2))],
        out_specs=tiles.padded_out_pair(w, pad) + [tiles.combined(w)],
        out_shape=[jax.ShapeDtypeStruct((tiles.bp, tiles.sp + pad, w), F32),
                   jax.ShapeDtypeStruct((tiles.bs, tiles.ss + pad, w), F32),
                   jax.ShapeDtypeStruct((tiles.n * TM, w), F32)],
        compiler_params=_TOKEN_PARAMS,
        name="lru_in",
    )(xp, xs, mods, g, w_in)


def _lru_rec_kernel(B, S, P, has_h0, n_cast, *refs):
    n_in = 6 + int(has_h0)
    x_ref, cw_ref, cb_ref, wg_ref, gb_ref, lam_ref = refs[:6]
    h0_ref = refs[6] if has_h0 else None
    cast_src = refs[n_in:n_in + n_cast]
    out_ref, st_ref = refs[n_in + n_cast:n_in + n_cast + 2]
    cast_dst = refs[n_in + n_cast + 2:n_in + 2 * n_cast + 2]
    xt, xhb, hfb, a_e, bx_e, a_o, bx_o, g_e, g_o = refs[n_in + 2 * n_cast + 2:]
    rows = B * S
    R = GATE_ROWS
    n = rows // R
    steps = R // B
    assert n % 2 == 0 and n >= 4 and R % B == 0

    for src, dst in zip(cast_src, cast_dst):
        dst[...] = src[...].astype(BF16)

    for b in range(B):
        out_ref[pl.ds(b * P + S, P - S), :] = jnp.zeros((P - S, LANES), F32)

    xt[pl.ds(0, B), :] = jnp.zeros((B, LANES), F32)
    xt[pl.ds(rows + B, 2 * B), :] = jnp.zeros((2 * B, LANES), F32)

    def gather(t, _):
        xt[pl.ds(pl.multiple_of((t + 1) * B, B), B), :] = x_ref[pl.ds(t, B, stride=P), :]
        return 0
    lax.fori_loop(0, S, gather, 0, unroll=4)

    cw = 0.5 * cw_ref[...]
    cb = 0.5 * cb_ref[...]
    gb = gb_ref[...]
    neg = -lam_ref[...]
    sp = jnp.maximum(neg, 0.0) + jnp.log1p(jnp.exp(-jnp.abs(neg)))
    c_pos = (0.5 * LRU_C) * sp
    c_exp2 = -LOG2_E * c_pos

    def preact(z, c, g_dst):
        r0 = pl.multiple_of(c * R, R)
        if z == 0:
            xh = (cw[0:1] * xt[pl.ds(r0, R), :]
                  + cw[1:2] * xt[pl.ds(r0 + B, R), :]
                  + cw[2:3] * xt[pl.ds(r0 + 2 * B, R), :]
                  + cw[3:4] * xt[pl.ds(r0 + 3 * B, R), :]) + cb
            xhb[pl.ds(r0, R), :] = xh
        else:
            xh = xhb[pl.ds(r0, R), :]
        cols = slice(z * 2 * LANES, (z + 1) * 2 * LANES)
        g_dst[...] = _dot(xh, wg_ref[:, cols]) + gb[:, cols]

    def gates(z, c, g_src, a_dst, bx_dst):
        xh = xhb[pl.ds(pl.multiple_of(c * R, R), R), :]
        r2 = jnp.tanh(g_src[:, :LANES]) + 1.0
        i2 = jnp.tanh(g_src[:, LANES:]) + 1.0
        a = jnp.exp2(r2 * c_exp2[z:z + 1])
        om = jnp.tanh(r2 * c_pos[z:z + 1]) * (a * a + 1.0)
        mult = jnp.where(om > 0.0, om * lax.rsqrt(om), 0.0)
        a_dst[...] = a
        bx_dst[...] = mult * (i2 * xh)

    def scan_f(c, a_src, bx_src, h):
        for k in range(steps):
            h = a_src[k * B:(k + 1) * B, :] * h + bx_src[k * B:(k + 1) * B, :]
            hfb[pl.ds(pl.multiple_of(c * R + k * B, B), B), :] = h
        return h

    def scan_b(c, a_src, bx_src, h):
        for k in reversed(range(steps)):
            h = a_src[k * B:(k + 1) * B, :] * h + bx_src[k * B:(k + 1) * B, :]
            hf = hfb[pl.ds(pl.multiple_of(c * R + k * B, B), B), :]
            out_ref[pl.ds(c * steps + k, B, stride=P), :] = hf + h
        return h

    def run_pass(z, chunk, scan, h):
        even, odd = (g_e, a_e, bx_e), (g_o, a_o, bx_o)
        preact(z, chunk(0), g_e)
        gates(z, chunk(0), *even)
        preact(z, chunk(1), g_o)

        def body(k, h):
            i = 2 * k + 1
            preact(z, chunk(i + 1), g_e)
            gates(z, chunk(i), *odd)
            h = scan(chunk(i - 1), a_e, bx_e, h)
            preact(z, chunk(i + 2), g_o)
            gates(z, chunk(i + 1), *even)
            return scan(chunk(i), a_o, bx_o, h)
        h = lax.fori_loop(0, n // 2 - 1, body, h)
        gates(z, chunk(n - 1), *odd)
        h = scan(chunk(n - 2), a_e, bx_e, h)
        return scan(chunk(n - 1), a_o, bx_o, h)

    if has_h0:
        h0f, h0b = h0_ref[0], h0_ref[1]
    else:
        h0f = h0b = jnp.zeros((B, LANES), F32)
    st_ref[0] = run_pass(0, lambda i: i, scan_f, h0f)
    st_ref[1] = run_pass(1, lambda i: n - 1 - i, scan_b, h0b)


def _lru_rec(xb, conv_w, conv_b, wg, gb, lam, h0, casts):
    B, P, w = xb.shape
    S = P - SUBLANES
    rows = B * S
    nj = w // LANES
    has_h0 = h0 is not None
    slab = pl.BlockSpec((B * P, LANES), lambda j: (0, j))
    state = pl.BlockSpec((2, B, LANES), lambda j: (0, 0, j))
    in_specs = [
        slab,
        pl.BlockSpec((4, LANES), lambda j: (0, j)),
        pl.BlockSpec((1, LANES), lambda j: (0, j)),
        pl.BlockSpec((None, LANES, 4 * LANES), lambda j: (j, 0, 0)),
        pl.BlockSpec((None, 1, 4 * LANES), lambda j: (j, 0, 0)),
        pl.BlockSpec((2, LANES), lambda j: (0, j)),
    ]
    args = [xb.reshape(B * P, w), conv_w, conv_b, wg, gb, lam]
    if has_h0:
        in_specs.append(state)
        args.append(h0)
    out_specs = [slab, state]
    out_shape = [jax.ShapeDtypeStruct((B * P, w), F32), jax.ShapeDtypeStruct((2, B, w), F32)]
    for wt, layer in casts:
        _, r, c = wt.shape
        assert r % (nj * 2 * SUBLANES) == 0
        in_specs.append(pl.BlockSpec((None, r // nj, c), functools.partial(
            lambda j, layer: (layer, j, 0), layer=layer)))
        args.append(wt)
        out_specs.append(pl.BlockSpec((r // nj, c), lambda j: (j, 0)))
        out_shape.append(jax.ShapeDtypeStruct((r, c), BF16))
    res = pl.pallas_call(
        functools.partial(_lru_rec_kernel, B, S, P, has_h0, len(casts)),
        grid=(nj,),
        in_specs=in_specs,
        out_specs=out_specs,
        out_shape=out_shape,
        scratch_shapes=[pltpu.VMEM((rows + 3 * B, LANES), F32)]
        + [pltpu.VMEM((rows, LANES), F32)] * 2 + [pltpu.VMEM((GATE_ROWS, LANES), F32)] * 4
        + [pltpu.VMEM((GATE_ROWS, 2 * LANES), F32)] * 2,
        compiler_params=pltpu.CompilerParams(
            dimension_semantics=("arbitrary",), vmem_limit_bytes=VMEM_LIMIT),
        name="lru_rec",
    )(*args)
    return res[0].reshape(B, P, w), res[1], res[2:]


def _ffn(x1, m, g, wg_ref, wu_ref, wd_ref):
    h = _modulated_norm(x1, g, m[3:4], m[4:5]).astype(BF16)
    gate = jnp.dot(h, wg_ref[...], preferred_element_type=F32)
    up = jnp.dot(h, wu_ref[...], preferred_element_type=F32)
    return x1 + m[5:6] * _dot(jax.nn.silu(gate) * up, wd_ref[...])


def _lru_out_kernel(tiles, xp_ref, xs_ref, hp_ref, hs_ref, y_ref, mod_ref, g_ref,
                    wo_ref, wg_ref, wu_ref, wd_ref, o_ref):
    m = mod_ref[...]
    hsum = tiles.read(hp_ref, hs_ref)
    x = tiles.read(xp_ref, xs_ref)
    for s0 in range(0, TM, SUB):
        rows = slice(s0, s0 + SUB)
        x1 = x[rows] + m[2:3] * _dot(hsum[rows] * y_ref[rows, :], wo_ref[...])
        o_ref[rows, :] = _ffn(x1, m, g_ref[...], wg_ref, wu_ref, wd_ref)


def _lru_out(tiles, xp, xs, hp, hs, y, mods, g, w_out, w_gate, w_up, w_down, layer):
    d = xp.shape[1]
    w = y.shape[1]
    f = w_gate.shape[1]
    return pl.pallas_call(
        functools.partial(_lru_out_kernel, tiles),
        grid=(tiles.n,),
        in_specs=tiles.flat_pair(d) + tiles.padded_pair(w) + [
            tiles.combined(w), tiles.mod(layer, d), _const_spec((1, d)), _const_spec((w, d)),
            _const_spec((d, f)), _const_spec((d, f)), _const_spec((f, d))],
        out_specs=tiles.combined(d),
        out_shape=jax.ShapeDtypeStruct((tiles.n * TM, d), F32),
        compiler_params=_TOKEN_PARAMS,
        name="lru_out_ffn",
    )(xp, xs, hp, hs, y, mods, g, w_out, w_gate, w_up, w_down)


def _cm_kernel(tiles, x_ref, mod_ref, gm_ref, gf_ref, gfin_ref, wi_ref, bi_ref, lng_ref, lnb_ref,
               ws_ref, bs_ref, wo_ref, wg_ref, wu_ref, wd_ref, op_ref, os_ref):
    m = mod_ref[...]
    cw = lng_ref.shape[-1]
    n_groups = ws_ref.shape[0]
    gd = cw // n_groups
    outs = []
    for s0 in range(0, TM, SUB):
        x = x_ref[s0:s0 + SUB, :]
        h = _modulated_norm(x, gm_ref[...], m[0:1], m[1:2])
        uv = jax.nn.gelu(_dot(h, wi_ref[...]) + bi_ref[...])
        u = uv[:, :cw]
        v = uv[:, cw:]
        mu = jnp.mean(v, axis=-1, keepdims=True)
        var = jnp.mean(jnp.square(v - mu), axis=-1, keepdims=True)
        v = ((v - mu) * lax.rsqrt(var + EPS) * lng_ref[...] + lnb_ref[...]).astype(BF16)
        mixed_rows = []
        for c in range(SUB // CHUNK):
            vc = v[c * CHUNK:(c + 1) * CHUNK]
            cols = [jnp.dot(ws_ref[g], vc[:, g * gd:(g + 1) * gd], preferred_element_type=F32)
                    for g in range(n_groups)]
            mixed_rows.append(jnp.concatenate(cols, axis=1) + bs_ref[...])
        mixed = jnp.concatenate(mixed_rows, axis=0)
        x1 = x + m[2:3] * _dot(u * mixed, wo_ref[...])
        x2 = _ffn(x1, m, gf_ref[...], wg_ref, wu_ref, wd_ref)
        outs.append(_rmsnorm(x2, gfin_ref[...]))
    tiles.write(op_ref, os_ref, jnp.concatenate(outs, axis=0))


def _cm_layer(tiles, x, mods, g_mix, g_ffn, g_final, w_in, b_in, ln_g, ln_b, w_s, b_s, w_out,
              w_gate, w_up, w_down, layer):
    d = x.shape[1]
    cw = w_out.shape[0]
    f = w_gate.shape[1]
    return pl.pallas_call(
        functools.partial(_cm_kernel, tiles),
        grid=(tiles.n,),
        in_specs=[tiles.combined(d), tiles.mod(layer, d),
                  _const_spec((1, d)), _const_spec((1, d)), _const_spec((1, d)),
                  _const_spec((d, 2 * cw)), _const_spec((1, 2 * cw)),
                  _const_spec((1, cw)), _const_spec((1, cw)),
                  _const_spec(w_s.shape), _const_spec(b_s.shape), _const_spec((cw, d)),
                  _const_spec((d, f)), _const_spec((d, f)), _const_spec((f, d))],
        out_specs=tiles.flat_pair(d),
        out_shape=[jax.ShapeDtypeStruct((tiles.npt * TM, d), F32),
                   jax.ShapeDtypeStruct((tiles.nst * TM, d), F32)],
        compiler_params=_TOKEN_PARAMS,
        name="cm_layer",
    )(x, mods, g_mix, g_ffn, g_final, w_in, b_in, ln_g, ln_b, w_s, b_s, w_out,
      w_gate, w_up, w_down)


def _gate_weights(ga_w, gx_w, ga_b, gx_b):
    _, heads, hd, _ = ga_w.shape
    per = LANES // hd
    nj = heads // per
    w4 = jnp.stack([ga_w[0], gx_w[0], ga_w[1], gx_w[1]]).reshape(4, nj, per, hd, hd)
    eye = jnp.eye(per, dtype=w4.dtype)
    wg = jnp.einsum('kjhde,hg->jhdkge', w4, eye).reshape(nj, LANES, 4 * LANES)
    b4 = 0.5 * jnp.stack([ga_b[0], gx_b[0], ga_b[1], gx_b[1]]).reshape(4, nj, LANES)
    gb = jnp.transpose(b4, (1, 0, 2)).reshape(nj, 1, 4 * LANES)
    return wg.astype(BF16), gb


def kernel(x_prompt, x_sample, state_lru, c, c_ctx, mod_w, mod_b, norm_mix, norm_ffn, lru_w_in, lru_conv_w, lru_conv_b, lru_ga_w, lru_ga_b, lru_gx_w, lru_gx_b, lru_lambda, lru_w_out, cm_w_in, cm_b_in, cm_ln_g, cm_ln_b, cm_w_s, cm_b_s, cm_w_out, ffn_w_gate, ffn_w_up, ffn_w_down, final_norm):
    bp, sp, d = x_prompt.shape
    bs, ss, _ = x_sample.shape
    tiles = _Tiles(((bp, sp), (bs, ss)))
    xp = x_prompt.reshape(bp * sp, d)
    xs = x_sample.reshape(bs * ss, d)

    cond = jnp.concatenate([c_ctx[None], c, jnp.zeros((COND_ROWS - 1 - bs, d), F32)], 0)
    mods = _adaln(cond, mod_w, mod_b).reshape(-1, N_MOD, d)

    xbp, xbs, y = _lru_in(tiles, xp, xs, mods, norm_mix[0:1], lru_w_in[0].astype(BF16), 0, SUBLANES)
    wg, gb = _gate_weights(lru_ga_w[0], lru_gx_w[0], lru_ga_b[0], lru_gx_b[0])
    rec = (lru_conv_w[0], lru_conv_b[0:1], wg, gb, lru_lambda[0])
    hp, st, (wo0, wg0, wu0, wd0) = _lru_rec(
        xbp, *rec, None, [(lru_w_out, 0), (ffn_w_gate, 0), (ffn_w_up, 0), (ffn_w_down, 0)])
    hs, _, (wi1, wo1, wg1, wu1, wd1) = _lru_rec(
        xbs, *rec, jnp.transpose(state_lru[:, 0], (1, 0, 2)),
        [(cm_w_in, 0), (cm_w_out, 0), (ffn_w_gate, 1), (ffn_w_up, 1), (ffn_w_down, 1)])
    x = _lru_out(tiles, xp, xs, hp, hs, y, mods, norm_ffn[0:1], wo0, wg0, wu0, wd0, 0)

    n_groups = cm_w_s.shape[1]
    gd = cm_w_out.shape[1] // n_groups
    bs_tile = jnp.repeat(cm_b_s[0].T, gd, axis=1)
    yp, ys = _cm_layer(tiles, x, mods, norm_mix[1:2], norm_ffn[1:2], final_norm[None],
                       wi1, cm_b_in[0:1], cm_ln_g[0:1], cm_ln_b[0:1],
                       cm_w_s[0].astype(BF16), bs_tile, wo1, wg1, wu1, wd1, 1)

    new_state = jnp.transpose(st, (1, 0, 2))[:, None]
    return (yp.reshape(bp, sp, d), ys.reshape(bs, ss, d), new_state)
```

```python
import functools

import jax
import jax.numpy as jnp
from jax import lax
from jax.experimental import pallas as pl
from jax.experimental.pallas import tpu as pltpu

F32 = jnp.float32
BF16 = jnp.bfloat16

EPS = 1e-6
LRU_C = 8.0
LOG2_E = 1.4426950408889634
N_MOD = 6
COND_ROWS = 16
LANES = 128
SUBLANES = 8
CHUNK = 128
TM = 512
SUB = 256
GATE_ROWS = 512
VMEM_LIMIT = 52 * 1024 * 1024


def _rmsnorm(x, g):
    y = x * lax.rsqrt(jnp.mean(x * x, axis=-1, keepdims=True) + EPS)
    return y * g


def _modulated_norm(x, g, shift, scale):
    return _rmsnorm(x, g) * (1.0 + scale) + shift


def _dot(a, b):
    return jnp.dot(a.astype(BF16), b, preferred_element_type=F32)


def _adaln_kernel(c_ref, w_ref, b_ref, o_ref):
    s = jax.nn.silu(c_ref[...])
    o_ref[...] = _dot(s, w_ref[...].astype(BF16)) + b_ref[...]


def _adaln(cond, mod_w, mod_b):
    depth, d, n = mod_w.shape
    tn = 1024
    return pl.pallas_call(
        _adaln_kernel,
        grid=(depth, n // tn),
        in_specs=[
            pl.BlockSpec((COND_ROWS, d), lambda l, j: (0, 0)),
            pl.BlockSpec((None, d, tn), lambda l, j: (l, 0, j)),
            pl.BlockSpec((None, 1, tn), lambda l, j: (l, 0, j)),
        ],
        out_specs=pl.BlockSpec((None, COND_ROWS, tn), lambda l, j: (l, 0, j)),
        out_shape=jax.ShapeDtypeStruct((depth, COND_ROWS, n), F32),
        compiler_params=pltpu.CompilerParams(
            dimension_semantics=("arbitrary", "arbitrary"), vmem_limit_bytes=VMEM_LIMIT),
        name="adaln",
    )(cond, mod_w, mod_b.reshape(depth, 1, n))


class _Tiles:
    def __init__(self, dims):
        (self.bp, self.sp), (self.bs, self.ss) = dims
        assert TM % self.sp == 0 and self.ss % TM == 0 and (self.bp * self.sp) % TM == 0
        self.npt = self.bp * self.sp // TM
        self.nst = self.bs * self.ss // TM
        self.n = self.npt + self.nst

    def is_prompt(self):
        return pl.program_id(0) < self.npt

    def combined(self, cols):
        return pl.BlockSpec((TM, cols), lambda i: (i, 0))

    def _pi(self, i):
        return jnp.minimum(i, self.npt - 1)

    def _si(self, i):
        return jnp.maximum(i - self.npt, 0)

    def flat_pair(self, cols):
        return [pl.BlockSpec((TM, cols), lambda i: (self._pi(i), 0)),
                pl.BlockSpec((TM, cols), lambda i: (self._si(i), 0))]

    def padded_pair(self, cols):
        ts = self.ss // TM
        return [pl.BlockSpec((TM // self.sp, self.sp, cols), lambda i: (self._pi(i), 0, 0)),
                pl.BlockSpec((None, TM, cols), lambda i: (self._si(i) // ts, self._si(i) % ts, 0))]

    def padded_out_pair(self, cols, pad):
        ts = self.ss // TM
        return [pl.BlockSpec((TM // self.sp, self.sp + pad, cols), lambda i: (self._pi(i), 0, 0)),
                pl.BlockSpec((None, self.ss + pad, cols), lambda i: (self._si(i) // ts, 0, 0))]

    def write_padded(self, p_ref, s_ref, value):
        cols = value.shape[-1]
        pad = p_ref.shape[1] - self.sp

        @pl.when(self.is_prompt())
        def _():
            p_ref[:, :self.sp, :] = value.reshape(TM // self.sp, self.sp, cols)
            p_ref[:, self.sp:, :] = jnp.zeros((TM // self.sp, pad, cols), value.dtype)

        @pl.when(jnp.logical_not(self.is_prompt()))
        def _():
            part = self._si(pl.program_id(0)) % (self.ss // TM)
            s_ref[pl.ds(pl.multiple_of(part * TM, TM), TM), :] = value
            s_ref[self.ss:, :] = jnp.zeros((pad, cols), value.dtype)

    def mod(self, layer, d):
        def index_map(i):
            row = jnp.where(i < self.npt, 0, 1 + self._si(i) * TM // self.ss)
            return (layer * COND_ROWS + row, 0, 0)
        return pl.BlockSpec((None, N_MOD, d), index_map)

    def read(self, p_ref, s_ref):
        return jnp.where(self.is_prompt(), p_ref[...].reshape(s_ref.shape), s_ref[...])

    def write(self, p_ref, s_ref, value):
        @pl.when(self.is_prompt())
        def _():
            p_ref[...] = value.reshape(p_ref.shape)

        @pl.when(jnp.logical_not(self.is_prompt()))
        def _():
            s_ref[...] = value


def _const_spec(shape):
    nd = len(shape)
    return pl.BlockSpec(shape, lambda i: (0,) * nd, pipeline_mode=pl.Buffered(1))


_TOKEN_PARAMS = pltpu.CompilerParams(
    dimension_semantics=("arbitrary",), vmem_limit_bytes=VMEM_LIMIT)


def _lru_in_kernel(tiles, xp_ref, xs_ref, mod_ref, g_ref, w_ref, xbp_ref, xbs_ref, y_ref):
    m = mod_ref[...]
    x = tiles.read(xp_ref, xs_ref)
    w = y_ref.shape[-1]
    xb = []
    for s0 in range(0, TM, SUB):
        h = _modulated_norm(x[s0:s0 + SUB], g_ref[...], m[0:1], m[1:2])
        z = _dot(h, w_ref[...])
        xb.append(z[:, :w])
        y_ref[s0:s0 + SUB, :] = jax.nn.gelu(z[:, w:])
    tiles.write_padded(xbp_ref, xbs_ref, jnp.concatenate(xb, axis=0))


def _lru_in(tiles, xp, xs, mods, g, w_in, layer, pad):
    d = xp.shape[1]
    w2 = w_in.shape[1]
    w = w2 // 2
    return pl.pallas_call(
        functools.partial(_lru_in_kernel, tiles),
        grid=(tiles.n,),
        in_specs=tiles.flat_pair(d) + [tiles.mod(layer, d), _const_spec((1, d)),
                                       _const_spec((d, w2))],
        out_specs=tiles.padded_out_pair(w, pad) + [tiles.combined(w)],
        out_shape=[jax.ShapeDtypeStruct((tiles.bp, tiles.sp + pad, w), F32),
                   jax.ShapeDtypeStruct((tiles.bs, tiles.ss + pad, w), F32),
                   jax.ShapeDtypeStruct((tiles.n * TM, w), F32)],
        compiler_params=_TOKEN_PARAMS,
        name="lru_in",
    )(xp, xs, mods, g, w_in)


def _lru_rec_kernel(B, S, P, has_h0, n_cast, *refs):
    n_in = 6 + int(has_h0)
    x_ref, cw_ref, cb_ref, wg_ref, gb_ref, lam_ref = refs[:6]
    h0_ref = refs[6] if has_h0 else None
    cast_src = refs[n_in:n_in + n_cast]
    out_ref, st_ref = refs[n_in + n_cast:n_in + n_cast + 2]
    cast_dst = refs[n_in + n_cast + 2:n_in + 2 * n_cast + 2]
    xt, xhb, hfb, a_e, bx_e, a_o, bx_o, g_e, g_o = refs[n_in + 2 * n_cast + 2:]
    rows = B * S
    R = GATE_ROWS
    n = rows // R
    steps = R // B
    assert n % 2 == 0 and n >= 4 and R % B == 0

    for src, dst in zip(cast_src, cast_dst):
        dst[...] = src[...].astype(BF16)

    for b in range(B):
        out_ref[pl.ds(b * P + S, P - S), :] = jnp.zeros((P - S, LANES), F32)

    xt[pl.ds(0, B), :] = jnp.zeros((B, LANES), F32)
    xt[pl.ds(rows + B, 2 * B), :] = jnp.zeros((2 * B, LANES), F32)

    def gather(t, _):
        xt[pl.ds(pl.multiple_of((t + 1) * B, B), B), :] = x_ref[pl.ds(t, B, stride=P), :]
        return 0
    lax.fori_loop(0, S, gather, 0, unroll=4)

    cw = 0.5 * cw_ref[...]
    cb = 0.5 * cb_ref[...]
    gb = gb_ref[...]
    neg = -lam_ref[...]
    sp = jnp.maximum(neg, 0.0) + jnp.log1p(jnp.exp(-jnp.abs(neg)))
    c_pos = (0.5 * LRU_C) * sp
    c_exp2 = -LOG2_E * c_pos

    def preact(z, c, g_dst):
        r0 = pl.multiple_of(c * R, R)
        if z == 0:
            xh = (cw[0:1] * xt[pl.ds(r0, R), :]
                  + cw[1:2] * xt[pl.ds(r0 + B, R), :]
                  + cw[2:3] * xt[pl.ds(r0 + 2 * B, R), :]
                  + cw[3:4] * xt[pl.ds(r0 + 3 * B, R), :]) + cb
            xhb[pl.ds(r0, R), :] = xh
        else:
            xh = xhb[pl.ds(r0, R), :]
        cols = slice(z * 2 * LANES, (z + 1) * 2 * LANES)
        g_dst[...] = _dot(xh, wg_ref[:, cols]) + gb[:, cols]

    def gates(z, c, g_src, a_dst, bx_dst):
        xh = xhb[pl.ds(pl.multiple_of(c * R, R), R), :]
        r2 = jnp.tanh(g_src[:, :LANES]) + 1.0
        i2 = jnp.tanh(g_src[:, LANES:]) + 1.0
        a = jnp.exp2(r2 * c_exp2[z:z + 1])
        om = jnp.tanh(r2 * c_pos[z:z + 1]) * (a * a + 1.0)
        mult = jnp.where(om > 0.0, om * lax.rsqrt(om), 0.0)
        a_dst[...] = a
        bx_dst[...] = mult * (i2 * xh)

    def scan_f(c, a_src, bx_src, h):
        for k in range(steps):
            h = a_src[k * B:(k + 1) * B, :] * h + bx_src[k * B:(k + 1) * B, :]
            hfb[pl.ds(pl.multiple_of(c * R + k * B, B), B), :] = h
        return h

    def scan_b(c, a_src, bx_src, h):
        for k in reversed(range(steps)):
            h = a_src[k * B:(k + 1) * B, :] * h + bx_src[k * B:(k + 1) * B, :]
            hf = hfb[pl.ds(pl.multiple_of(c * R + k * B, B), B), :]
            out_ref[pl.ds(c * steps + k, B, stride=P), :] = hf + h
        return h

    def run_pass(z, chunk, scan, h):
        even, odd = (g_e, a_e, bx_e), (g_o, a_o, bx_o)
        preact(z, chunk(0), g_e)
        gates(z, chunk(0), *even)
        preact(z, chunk(1), g_o)

        def body(k, h):
            i = 2 * k + 1
            preact(z, chunk(i + 1), g_e)
            gates(z, chunk(i), *odd)
            h = scan(chunk(i - 1), a_e, bx_e, h)
            preact(z, chunk(i + 2), g_o)
            gates(z, chunk(i + 1), *even)
            return scan(chunk(i), a_o, bx_o, h)
        h = lax.fori_loop(0, n // 2 - 1, body, h)
        gates(z, chunk(n - 1), *odd)
        h = scan(chunk(n - 2), a_e, bx_e, h)
        return scan(chunk(n - 1), a_o, bx_o, h)

    if has_h0:
        h0f, h0b = h0_ref[0], h0_ref[1]
    else:
        h0f = h0b = jnp.zeros((B, LANES), F32)
    st_ref[0] = run_pass(0, lambda i: i, scan_f, h0f)
    st_ref[1] = run_pass(1, lambda i: n - 1 - i, scan_b, h0b)


def _lru_rec(xb, conv_w, conv_b, wg, gb, lam, h0, casts):
    B, P, w = xb.shape
    S = P - SUBLANES
    rows = B * S
    nj = w // LANES
    has_h0 = h0 is not None
    slab = pl.BlockSpec((B * P, LANES), lambda j: (0, j))
    state = pl.BlockSpec((2, B, LANES), lambda j: (0, 0, j))
    in_specs = [
        slab,
        pl.BlockSpec((4, LANES), lambda j: (0, j)),
        pl.BlockSpec((1, LANES), lambda j: (0, j)),
        pl.BlockSpec((None, LANES, 4 * LANES), lambda j: (j, 0, 0)),
        pl.BlockSpec((None, 1, 4 * LANES), lambda j: (j, 0, 0)),
        pl.BlockSpec((2, LANES), lambda j: (0, j)),
    ]
    args = [xb.reshape(B * P, w), conv_w, conv_b, wg, gb, lam]
    if has_h0:
        in_specs.append(state)
        args.append(h0)
    out_specs = [slab, state]
    out_shape = [jax.ShapeDtypeStruct((B * P, w), F32), jax.ShapeDtypeStruct((2, B, w), F32)]
    for wt, layer in casts:
        _, r, c = wt.shape
        assert r % (nj * 2 * SUBLANES) == 0
        in_specs.append(pl.BlockSpec((None, r // nj, c), functools.partial(
            lambda j, layer: (layer, j, 0), layer=layer)))
        args.append(wt)
        out_specs.append(pl.BlockSpec((r // nj, c), lambda j: (j, 0)))
        out_shape.append(jax.ShapeDtypeStruct((r, c), BF16))
    res = pl.pallas_call(
        functools.partial(_lru_rec_kernel, B, S, P, has_h0, len(casts)),
        grid=(nj,),
        in_specs=in_specs,
        out_specs=out_specs,
        out_shape=out_shape,
        scratch_shapes=[pltpu.VMEM((rows + 3 * B, LANES), F32)]
        + [pltpu.VMEM((rows, LANES), F32)] * 2 + [pltpu.VMEM((GATE_ROWS, LANES), F32)] * 4
        + [pltpu.VMEM((GATE_ROWS, 2 * LANES), F32)] * 2,
        compiler_params=pltpu.CompilerParams(
            dimension_semantics=("arbitrary",), vmem_limit_bytes=VMEM_LIMIT),
        name="lru_rec",
    )(*args)
    return res[0].reshape(B, P, w), res[1], res[2:]


def _sub_tiles():
    return [slice(s0, s0 + SUB) for s0 in range(0, TM, SUB)]


def _ffn(x1, m, g, wg_ref, wu_ref, wd_ref):
    h = [_modulated_norm(v, g, m[3:4], m[4:5]).astype(BF16) for v in x1]
    gu = [(jnp.dot(v, wg_ref[...], preferred_element_type=F32),
           jnp.dot(v, wu_ref[...], preferred_element_type=F32)) for v in h]
    dn = [_dot(jax.nn.silu(gate) * up, wd_ref[...]) for gate, up in gu]
    return [v + m[5:6] * d for v, d in zip(x1, dn)]


def _lru_out_kernel(tiles, xp_ref, xs_ref, hp_ref, hs_ref, y_ref, mod_ref, g_ref,
                    wo_ref, wg_ref, wu_ref, wd_ref, o_ref):
    m = mod_ref[...]
    hsum = tiles.read(hp_ref, hs_ref)
    x = tiles.read(xp_ref, xs_ref)
    subs = _sub_tiles()
    x1 = [x[r] + m[2:3] * _dot(hsum[r] * y_ref[r, :], wo_ref[...]) for r in subs]
    for r, v in zip(subs, _ffn(x1, m, g_ref[...], wg_ref, wu_ref, wd_ref)):
        o_ref[r, :] = v


def _lru_out(tiles, xp, xs, hp, hs, y, mods, g, w_out, w_gate, w_up, w_down, layer):
    d = xp.shape[1]
    w = y.shape[1]
    f = w_gate.shape[1]
    return pl.pallas_call(
        functools.partial(_lru_out_kernel, tiles),
        grid=(tiles.n,),
        in_specs=tiles.flat_pair(d) + tiles.padded_pair(w) + [
            tiles.combined(w), tiles.mod(layer, d), _const_spec((1, d)), _const_spec((w, d)),
            _const_spec((d, f)), _const_spec((d, f)), _const_spec((f, d))],
        out_specs=tiles.combined(d),
        out_shape=jax.ShapeDtypeStruct((tiles.n * TM, d), F32),
        compiler_params=_TOKEN_PARAMS,
        name="lru_out_ffn",
    )(xp, xs, hp, hs, y, mods, g, w_out, w_gate, w_up, w_down)


def _cm_kernel(tiles, x_ref, mod_ref, gm_ref, gf_ref, gfin_ref, wi_ref, bi_ref, lng_ref, lnb_ref,
               ws_ref, bs_ref, wo_ref, wg_ref, wu_ref, wd_ref, op_ref, os_ref):
    m = mod_ref[...]
    cw = lng_ref.shape[-1]
    n_groups = ws_ref.shape[0]
    gd = cw // n_groups
    xs = [x_ref[r, :] for r in _sub_tiles()]
    pre = [_dot(_modulated_norm(x, gm_ref[...], m[0:1], m[1:2]), wi_ref[...]) + bi_ref[...]
           for x in xs]
    x1 = []
    for x, z in zip(xs, pre):
        uv = jax.nn.gelu(z)
        u = uv[:, :cw]
        v = uv[:, cw:]
        mu = jnp.mean(v, axis=-1, keepdims=True)
        var = jnp.mean(jnp.square(v - mu), axis=-1, keepdims=True)
        v = ((v - mu) * lax.rsqrt(var + EPS) * lng_ref[...] + lnb_ref[...]).astype(BF16)
        mixed_rows = []
        for c in range(SUB // CHUNK):
            vc = v[c * CHUNK:(c + 1) * CHUNK]
            cols = [jnp.dot(ws_ref[g], vc[:, g * gd:(g + 1) * gd], preferred_element_type=F32)
                    for g in range(n_groups)]
            mixed_rows.append(jnp.concatenate(cols, axis=1) + bs_ref[...])
        mixed = jnp.concatenate(mixed_rows, axis=0)
        x1.append(x + m[2:3] * _dot(u * mixed, wo_ref[...]))
    x2 = _ffn(x1, m, gf_ref[...], wg_ref, wu_ref, wd_ref)
    tiles.write(op_ref, os_ref, jnp.concatenate([_rmsnorm(v, gfin_ref[...]) for v in x2], axis=0))


def _cm_layer(tiles, x, mods, g_mix, g_ffn, g_final, w_in, b_in, ln_g, ln_b, w_s, b_s, w_out,
              w_gate, w_up, w_down, layer):
    d = x.shape[1]
    cw = w_out.shape[0]
    f = w_gate.shape[1]
    return pl.pallas_call(
        functools.partial(_cm_kernel, tiles),
        grid=(tiles.n,),
        in_specs=[tiles.combined(d), tiles.mod(layer, d),
                  _const_spec((1, d)), _const_spec((1, d)), _const_spec((1, d)),
                  _const_spec((d, 2 * cw)), _const_spec((1, 2 * cw)),
                  _const_spec((1, cw)), _const_spec((1, cw)),
                  _const_spec(w_s.shape), _const_spec(b_s.shape), _const_spec((cw, d)),
                  _const_spec((d, f)), _const_spec((d, f)), _const_spec((f, d))],
        out_specs=tiles.flat_pair(d),
        out_shape=[jax.ShapeDtypeStruct((tiles.npt * TM, d), F32),
                   jax.ShapeDtypeStruct((tiles.nst * TM, d), F32)],
        compiler_params=_TOKEN_PARAMS,
        name="cm_layer",
    )(x, mods, g_mix, g_ffn, g_final, w_in, b_in, ln_g, ln_b, w_s, b_s, w_out,
      w_gate, w_up, w_down)


def _gate_weights(ga_w, gx_w, ga_b, gx_b):
    _, heads, hd, _ = ga_w.shape
    per = LANES // hd
    nj = heads // per
    w4 = jnp.stack([ga_w[0], gx_w[0], ga_w[1], gx_w[1]]).reshape(4, nj, per, hd, hd)
    eye = jnp.eye(per, dtype=w4.dtype)
    wg = jnp.einsum('kjhde,hg->jhdkge', w4, eye).reshape(nj, LANES, 4 * LANES)
    b4 = 0.5 * jnp.stack([ga_b[0], gx_b[0], ga_b[1], gx_b[1]]).reshape(4, nj, LANES)
    gb = jnp.transpose(b4, (1, 0, 2)).reshape(nj, 1, 4 * LANES)
    return wg.astype(BF16), gb


def kernel(x_prompt, x_sample, state_lru, c, c_ctx, mod_w, mod_b, norm_mix, norm_ffn, lru_w_in, lru_conv_w, lru_conv_b, lru_ga_w, lru_ga_b, lru_gx_w, lru_gx_b, lru_lambda, lru_w_out, cm_w_in, cm_b_in, cm_ln_g, cm_ln_b, cm_w_s, cm_b_s, cm_w_out, ffn_w_gate, ffn_w_up, ffn_w_down, final_norm):
    bp, sp, d = x_prompt.shape
    bs, ss, _ = x_sample.shape
    tiles = _Tiles(((bp, sp), (bs, ss)))
    xp = x_prompt.reshape(bp * sp, d)
    xs = x_sample.reshape(bs * ss, d)

    cond = jnp.concatenate([c_ctx[None], c, jnp.zeros((COND_ROWS - 1 - bs, d), F32)], 0)
    mods = _adaln(cond, mod_w, mod_b).reshape(-1, N_MOD, d)

    xbp, xbs, y = _lru_in(tiles, xp, xs, mods, norm_mix[0:1], lru_w_in[0].astype(BF16), 0, SUBLANES)
    wg, gb = _gate_weights(lru_ga_w[0], lru_gx_w[0], lru_ga_b[0], lru_gx_b[0])
    rec = (lru_conv_w[0], lru_conv_b[0:1], wg, gb, lru_lambda[0])
    hp, st, (wo0, wg0, wu0, wd0) = _lru_rec(
        xbp, *rec, None, [(lru_w_out, 0), (ffn_w_gate, 0), (ffn_w_up, 0), (ffn_w_down, 0)])
    hs, _, (wi1, wo1, wg1, wu1, wd1) = _lru_rec(
        xbs, *rec, jnp.transpose(state_lru[:, 0], (1, 0, 2)),
        [(cm_w_in, 0), (cm_w_out, 0), (ffn_w_gate, 1), (ffn_w_up, 1), (ffn_w_down, 1)])
    x = _lru_out(tiles, xp, xs, hp, hs, y, mods, norm_ffn[0:1], wo0, wg0, wu0, wd0, 0)

    n_groups = cm_w_s.shape[1]
    gd = cm_w_out.shape[1] // n_groups
    bs_tile = jnp.repeat(cm_b_s[0].T, gd, axis=1)
    yp, ys = _cm_layer(tiles, x, mods, norm_mix[1:2], norm_ffn[1:2], final_norm[None],
                       wi1, cm_b_in[0:1], cm_ln_g[0:1], cm_ln_b[0:1],
                       cm_w_s[0].astype(BF16), bs_tile, wo1, wg1, wu1, wd1, 1)

    new_state = jnp.transpose(st, (1, 0, 2))[:, None]
    return (yp.reshape(bp, sp, d), ys.reshape(bs, ss, d), new_state)
```

```python
import functools

import jax
import jax.numpy as jnp
from jax import lax
from jax.experimental import pallas as pl
from jax.experimental.pallas import tpu as pltpu

F32 = jnp.float32
BF16 = jnp.bfloat16

EPS = 1e-6
LRU_C = 8.0
LOG2_E = 1.4426950408889634
N_MOD = 6
COND_ROWS = 16
LANES = 128
SUBLANES = 8
CHUNK = 128
TM_LIGHT = 1024
TM_HEAVY = 512
SUB = 256
GATE_ROWS = 512
VMEM_LIMIT = 56 * 1024 * 1024


def _rmsnorm(x, g):
    y = x * lax.rsqrt(jnp.mean(x * x, axis=-1, keepdims=True) + EPS)
    return y * g


def _modulated_norm(x, g, shift, scale):
    return _rmsnorm(x, g) * (1.0 + scale) + shift


def _dot(a, b):
    return jnp.dot(a.astype(BF16), b, preferred_element_type=F32)


def _adaln_kernel(c_ref, w_ref, b_ref, o_ref):
    s = jax.nn.silu(c_ref[...])
    o_ref[...] = _dot(s, w_ref[...].astype(BF16)) + b_ref[...]


def _adaln(cond, mod_w, mod_b):
    depth, d, n = mod_w.shape
    tn = 1024
    return pl.pallas_call(
        _adaln_kernel,
        grid=(depth, n // tn),
        in_specs=[
            pl.BlockSpec((COND_ROWS, d), lambda l, j: (0, 0)),
            pl.BlockSpec((None, d, tn), lambda l, j: (l, 0, j)),
            pl.BlockSpec((None, 1, tn), lambda l, j: (l, 0, j)),
        ],
        out_specs=pl.BlockSpec((None, COND_ROWS, tn), lambda l, j: (l, 0, j)),
        out_shape=jax.ShapeDtypeStruct((depth, COND_ROWS, n), F32),
        compiler_params=pltpu.CompilerParams(
            dimension_semantics=("arbitrary", "arbitrary"), vmem_limit_bytes=VMEM_LIMIT),
        name="adaln",
    )(cond, mod_w, mod_b.reshape(depth, 1, n))


class _Tiles:
    def __init__(self, dims, tm):
        (self.bp, self.sp), (self.bs, self.ss) = dims
        assert tm % self.sp == 0 and self.ss % tm == 0 and (self.bp * self.sp) % tm == 0
        assert tm % SUB == 0
        self.tm = tm
        self.npt = self.bp * self.sp // tm
        self.nst = self.bs * self.ss // tm
        self.n = self.npt + self.nst

    def subs(self):
        return [slice(s0, s0 + SUB) for s0 in range(0, self.tm, SUB)]

    def is_prompt(self):
        return pl.program_id(0) < self.npt

    def combined(self, cols):
        return pl.BlockSpec((self.tm, cols), lambda i: (i, 0))

    def _pi(self, i):
        return jnp.minimum(i, self.npt - 1)

    def _si(self, i):
        return jnp.maximum(i - self.npt, 0)

    def flat_pair(self, cols):
        return [pl.BlockSpec((self.tm, cols), lambda i: (self._pi(i), 0)),
                pl.BlockSpec((self.tm, cols), lambda i: (self._si(i), 0))]

    def padded_pair(self, cols):
        ts = self.ss // self.tm
        return [pl.BlockSpec((self.tm // self.sp, self.sp, cols), lambda i: (self._pi(i), 0, 0)),
                pl.BlockSpec((None, self.tm, cols),
                             lambda i: (self._si(i) // ts, self._si(i) % ts, 0))]

    def padded_out_pair(self, cols, pad):
        ts = self.ss // self.tm
        return [pl.BlockSpec((self.tm // self.sp, self.sp + pad, cols), lambda i: (self._pi(i), 0, 0)),
                pl.BlockSpec((None, self.ss + pad, cols), lambda i: (self._si(i) // ts, 0, 0))]

    def write_padded(self, p_ref, s_ref, value):
        cols = value.shape[-1]
        pad = p_ref.shape[1] - self.sp

        @pl.when(self.is_prompt())
        def _():
            p_ref[:, :self.sp, :] = value.reshape(self.tm // self.sp, self.sp, cols)
            p_ref[:, self.sp:, :] = jnp.zeros((self.tm // self.sp, pad, cols), value.dtype)

        @pl.when(jnp.logical_not(self.is_prompt()))
        def _():
            part = self._si(pl.program_id(0)) % (self.ss // self.tm)
            s_ref[pl.ds(pl.multiple_of(part * self.tm, self.tm), self.tm), :] = value
            s_ref[self.ss:, :] = jnp.zeros((pad, cols), value.dtype)

    def mod(self, layer, d):
        def index_map(i):
            row = jnp.where(i < self.npt, 0, 1 + self._si(i) * self.tm // self.ss)
            return (layer * COND_ROWS + row, 0, 0)
        return pl.BlockSpec((None, N_MOD, d), index_map)

    def read(self, p_ref, s_ref):
        return jnp.where(self.is_prompt(), p_ref[...].reshape(s_ref.shape), s_ref[...])

    def write(self, p_ref, s_ref, value):
        @pl.when(self.is_prompt())
        def _():
            p_ref[...] = value.reshape(p_ref.shape)

        @pl.when(jnp.logical_not(self.is_prompt()))
        def _():
            s_ref[...] = value


def _const_spec(shape):
    nd = len(shape)
    return pl.BlockSpec(shape, lambda i: (0,) * nd, pipeline_mode=pl.Buffered(1))


_TOKEN_PARAMS = pltpu.CompilerParams(
    dimension_semantics=("arbitrary",), vmem_limit_bytes=VMEM_LIMIT)


def _lru_in_kernel(tiles, xp_ref, xs_ref, mod_ref, g_ref, w_ref, xbp_ref, xbs_ref):
    m = mod_ref[...]
    x = tiles.read(xp_ref, xs_ref)
    xb = [_dot(_modulated_norm(x[r], g_ref[...], m[0:1], m[1:2]), w_ref[...])
          for r in tiles.subs()]
    tiles.write_padded(xbp_ref, xbs_ref, jnp.concatenate(xb, axis=0))


def _lru_in(tiles, xp, xs, mods, g, w_x, layer, pad):
    d, w = w_x.shape
    return pl.pallas_call(
        functools.partial(_lru_in_kernel, tiles),
        grid=(tiles.n,),
        in_specs=tiles.flat_pair(d) + [tiles.mod(layer, d), _const_spec((1, d)),
                                       _const_spec((d, w))],
        out_specs=tiles.padded_out_pair(w, pad),
        out_shape=[jax.ShapeDtypeStruct((tiles.bp, tiles.sp + pad, w), F32),
                   jax.ShapeDtypeStruct((tiles.bs, tiles.ss + pad, w), F32)],
        compiler_params=_TOKEN_PARAMS,
        name="lru_in",
    )(xp, xs, mods, g, w_x)


def _lru_rec_kernel(B, S, P, has_h0, n_cast, *refs):
    n_in = 6 + int(has_h0)
    x_ref, cw_ref, cb_ref, wg_ref, gb_ref, lam_ref = refs[:6]
    h0_ref = refs[6] if has_h0 else None
    cast_src = refs[n_in:n_in + n_cast]
    out_ref, st_ref = refs[n_in + n_cast:n_in + n_cast + 2]
    cast_dst = refs[n_in + n_cast + 2:n_in + 2 * n_cast + 2]
    xt, xhb, hfb, a_e, bx_e, a_o, bx_o, g_e, g_o = refs[n_in + 2 * n_cast + 2:]
    rows = B * S
    R = GATE_ROWS
    n = rows // R
    steps = R // B
    assert n % 2 == 0 and n >= 4 and R % B == 0

    for src, dst in zip(cast_src, cast_dst):
        dst[...] = src[...].astype(BF16)

    for b in range(B):
        out_ref[pl.ds(b * P + S, P - S), :] = jnp.zeros((P - S, LANES), F32)

    xt[pl.ds(0, B), :] = jnp.zeros((B, LANES), F32)
    xt[pl.ds(rows + B, 2 * B), :] = jnp.zeros((2 * B, LANES), F32)

    def gather(t, _):
        xt[pl.ds(pl.multiple_of((t + 1) * B, B), B), :] = x_ref[pl.ds(t, B, stride=P), :]
        return 0
    lax.fori_loop(0, S, gather, 0, unroll=4)

    cw = 0.5 * cw_ref[...]
    cb = 0.5 * cb_ref[...]
    gb = gb_ref[...]
    neg = -lam_ref[...]
    sp = jnp.maximum(neg, 0.0) + jnp.log1p(jnp.exp(-jnp.abs(neg)))
    c_pos = (0.5 * LRU_C) * sp
    c_exp2 = -LOG2_E * c_pos

    def preact(z, c, g_dst):
        r0 = pl.multiple_of(c * R, R)
        if z == 0:
            xh = (cw[0:1] * xt[pl.ds(r0, R), :]
                  + cw[1:2] * xt[pl.ds(r0 + B, R), :]
                  + cw[2:3] * xt[pl.ds(r0 + 2 * B, R), :]
                  + cw[3:4] * xt[pl.ds(r0 + 3 * B, R), :]) + cb
            xhb[pl.ds(r0, R), :] = xh
        else:
            xh = xhb[pl.ds(r0, R), :]
        cols = slice(z * 2 * LANES, (z + 1) * 2 * LANES)
        g_dst[...] = _dot(xh, wg_ref[:, cols]) + gb[:, cols]

    def gates(z, c, g_src, a_dst, bx_dst):
        xh = xhb[pl.ds(pl.multiple_of(c * R, R), R), :]
        r2 = jnp.tanh(g_src[:, :LANES]) + 1.0
        i2 = jnp.tanh(g_src[:, LANES:]) + 1.0
        a = jnp.exp2(r2 * c_exp2[z:z + 1])
        om = jnp.tanh(r2 * c_pos[z:z + 1]) * (a * a + 1.0)
        mult = jnp.where(om > 0.0, om * lax.rsqrt(om), 0.0)
        a_dst[...] = a
        bx_dst[...] = mult * (i2 * xh)

    def scan_f(c, a_src, bx_src, h):
        for k in range(steps):
            h = a_src[k * B:(k + 1) * B, :] * h + bx_src[k * B:(k + 1) * B, :]
            hfb[pl.ds(pl.multiple_of(c * R + k * B, B), B), :] = h
        return h

    def scan_b(c, a_src, bx_src, h):
        for k in reversed(range(steps)):
            h = a_src[k * B:(k + 1) * B, :] * h + bx_src[k * B:(k + 1) * B, :]
            hf = hfb[pl.ds(pl.multiple_of(c * R + k * B, B), B), :]
            out_ref[pl.ds(c * steps + k, B, stride=P), :] = hf + h
        return h

    def run_pass(z, chunk, scan, h):
        even, odd = (g_e, a_e, bx_e), (g_o, a_o, bx_o)
        preact(z, chunk(0), g_e)
        gates(z, chunk(0), *even)
        preact(z, chunk(1), g_o)

        def body(k, h):
            i = 2 * k + 1
            preact(z, chunk(i + 1), g_e)
            gates(z, chunk(i), *odd)
            h = scan(chunk(i - 1), a_e, bx_e, h)
            preact(z, chunk(i + 2), g_o)
            gates(z, chunk(i + 1), *even)
            return scan(chunk(i), a_o, bx_o, h)
        h = lax.fori_loop(0, n // 2 - 1, body, h)
        gates(z, chunk(n - 1), *odd)
        h = scan(chunk(n - 2), a_e, bx_e, h)
        return scan(chunk(n - 1), a_o, bx_o, h)

    if has_h0:
        h0f, h0b = h0_ref[0], h0_ref[1]
    else:
        h0f = h0b = jnp.zeros((B, LANES), F32)
    st_ref[0] = run_pass(0, lambda i: i, scan_f, h0f)
    st_ref[1] = run_pass(1, lambda i: n - 1 - i, scan_b, h0b)


def _lru_rec(xb, conv_w, conv_b, wg, gb, lam, h0, casts):
    B, P, w = xb.shape
    S = P - SUBLANES
    rows = B * S
    nj = w // LANES
    has_h0 = h0 is not None
    slab = pl.BlockSpec((B * P, LANES), lambda j: (0, j))
    state = pl.BlockSpec((2, B, LANES), lambda j: (0, 0, j))
    in_specs = [
        slab,
        pl.BlockSpec((4, LANES), lambda j: (0, j)),
        pl.BlockSpec((1, LANES), lambda j: (0, j)),
        pl.BlockSpec((None, LANES, 4 * LANES), lambda j: (j, 0, 0)),
        pl.BlockSpec((None, 1, 4 * LANES), lambda j: (j, 0, 0)),
        pl.BlockSpec((2, LANES), lambda j: (0, j)),
    ]
    args = [xb.reshape(B * P, w), conv_w, conv_b, wg, gb, lam]
    if has_h0:
        in_specs.append(state)
        args.append(h0)
    out_specs = [slab, state]
    out_shape = [jax.ShapeDtypeStruct((B * P, w), F32), jax.ShapeDtypeStruct((2, B, w), F32)]
    for wt, layer in casts:
        _, r, c = wt.shape
        assert r % (nj * 2 * SUBLANES) == 0
        in_specs.append(pl.BlockSpec((None, r // nj, c), functools.partial(
            lambda j, layer: (layer, j, 0), layer=layer)))
        args.append(wt)
        out_specs.append(pl.BlockSpec((r // nj, c), lambda j: (j, 0)))
        out_shape.append(jax.ShapeDtypeStruct((r, c), BF16))
    res = pl.pallas_call(
        functools.partial(_lru_rec_kernel, B, S, P, has_h0, len(casts)),
        grid=(nj,),
        in_specs=in_specs,
        out_specs=out_specs,
        out_shape=out_shape,
        scratch_shapes=[pltpu.VMEM((rows + 3 * B, LANES), F32)]
        + [pltpu.VMEM((rows, LANES), F32)] * 2 + [pltpu.VMEM((GATE_ROWS, LANES), F32)] * 4
        + [pltpu.VMEM((GATE_ROWS, 2 * LANES), F32)] * 2,
        compiler_params=pltpu.CompilerParams(
            dimension_semantics=("arbitrary",), vmem_limit_bytes=VMEM_LIMIT),
        name="lru_rec",
    )(*args)
    return res[0].reshape(B, P, w), res[1], res[2:]


def _ffn(x1, m, g, wg_ref, wu_ref, wd_ref):
    h = [_modulated_norm(v, g, m[3:4], m[4:5]).astype(BF16) for v in x1]
    gu = [(jnp.dot(v, wg_ref[...], preferred_element_type=F32),
           jnp.dot(v, wu_ref[...], preferred_element_type=F32)) for v in h]
    dn = [_dot(jax.nn.silu(gate) * up, wd_ref[...]) for gate, up in gu]
    return [v + m[5:6] * d for v, d in zip(x1, dn)]


def _lru_out_kernel(tiles, xp_ref, xs_ref, hp_ref, hs_ref, mod_ref, gm_ref, gf_ref,
                    wy_ref, wo_ref, wg_ref, wu_ref, wd_ref, o_ref):
    m = mod_ref[...]
    hsum = tiles.read(hp_ref, hs_ref)
    x = tiles.read(xp_ref, xs_ref)
    subs = tiles.subs()
    y = [jax.nn.gelu(_dot(_modulated_norm(x[r], gm_ref[...], m[0:1], m[1:2]), wy_ref[...]))
         for r in subs]
    x1 = [x[r] + m[2:3] * _dot(hsum[r] * v, wo_ref[...]) for r, v in zip(subs, y)]
    for r, v in zip(subs, _ffn(x1, m, gf_ref[...], wg_ref, wu_ref, wd_ref)):
        o_ref[r, :] = v


def _lru_out(tiles, xp, xs, hp, hs, mods, g_mix, g_ffn, w_in, w_out, w_gate, w_up, w_down, layer):
    d = xp.shape[1]
    w = w_out.shape[0]
    f = w_gate.shape[1]
    w_y = pl.BlockSpec((d, w), lambda i: (0, 1), pipeline_mode=pl.Buffered(1))
    return pl.pallas_call(
        functools.partial(_lru_out_kernel, tiles),
        grid=(tiles.n,),
        in_specs=tiles.flat_pair(d) + tiles.padded_pair(w) + [
            tiles.mod(layer, d), _const_spec((1, d)), _const_spec((1, d)), w_y,
            _const_spec((w, d)), _const_spec((d, f)), _const_spec((d, f)), _const_spec((f, d))],
        out_specs=tiles.combined(d),
        out_shape=jax.ShapeDtypeStruct((tiles.n * tiles.tm, d), F32),
        compiler_params=_TOKEN_PARAMS,
        name="lru_out_ffn",
    )(xp, xs, hp, hs, mods, g_mix, g_ffn, w_in, w_out, w_gate, w_up, w_down)


def _cm_kernel(tiles, x_ref, mod_ref, gm_ref, gf_ref, gfin_ref, wi_ref, bi_ref, lng_ref, lnb_ref,
               ws_ref, bs_ref, wo_ref, wg_ref, wu_ref, wd_ref, op_ref, os_ref):
    m = mod_ref[...]
    cw = lng_ref.shape[-1]
    n_groups = ws_ref.shape[0]
    gd = cw // n_groups
    xs = [x_ref[r, :] for r in tiles.subs()]
    pre = [_dot(_modulated_norm(x, gm_ref[...], m[0:1], m[1:2]), wi_ref[...]) + bi_ref[...]
           for x in xs]
    x1 = []
    for x, z in zip(xs, pre):
        uv = jax.nn.gelu(z)
        u = uv[:, :cw]
        v = uv[:, cw:]
        mu = jnp.mean(v, axis=-1, keepdims=True)
        var = jnp.mean(jnp.square(v - mu), axis=-1, keepdims=True)
        v = ((v - mu) * lax.rsqrt(var + EPS) * lng_ref[...] + lnb_ref[...]).astype(BF16)
        mixed_rows = []
        for c in range(SUB // CHUNK):
            vc = v[c * CHUNK:(c + 1) * CHUNK]
            cols = [jnp.dot(ws_ref[g], vc[:, g * gd:(g + 1) * gd], preferred_element_type=F32)
                    for g in range(n_groups)]
            mixed_rows.append(jnp.concatenate(cols, axis=1) + bs_ref[...])
        mixed = jnp.concatenate(mixed_rows, axis=0)
        x1.append(x + m[2:3] * _dot(u * mixed, wo_ref[...]))
    x2 = _ffn(x1, m, gf_ref[...], wg_ref, wu_ref, wd_ref)
    tiles.write(op_ref, os_ref, jnp.concatenate([_rmsnorm(v, gfin_ref[...]) for v in x2], axis=0))


def _cm_layer(tiles, x, mods, g_mix, g_ffn, g_final, w_in, b_in, ln_g, ln_b, w_s, b_s, w_out,
              w_gate, w_up, w_down, layer):
    d = x.shape[1]
    cw = w_out.shape[0]
    f = w_gate.shape[1]
    return pl.pallas_call(
        functools.partial(_cm_kernel, tiles),
        grid=(tiles.n,),
        in_specs=[tiles.combined(d), tiles.mod(layer, d),
                  _const_spec((1, d)), _const_spec((1, d)), _const_spec((1, d)),
                  _const_spec((d, 2 * cw)), _const_spec((1, 2 * cw)),
                  _const_spec((1, cw)), _const_spec((1, cw)),
                  _const_spec(w_s.shape), _const_spec(b_s.shape), _const_spec((cw, d)),
                  _const_spec((d, f)), _const_spec((d, f)), _const_spec((f, d))],
        out_specs=tiles.flat_pair(d),
        out_shape=[jax.ShapeDtypeStruct((tiles.npt * tiles.tm, d), F32),
                   jax.ShapeDtypeStruct((tiles.nst * tiles.tm, d), F32)],
        compiler_params=_TOKEN_PARAMS,
        name="cm_layer",
    )(x, mods, g_mix, g_ffn, g_final, w_in, b_in, ln_g, ln_b, w_s, b_s, w_out,
      w_gate, w_up, w_down)


def _gate_weights(ga_w, gx_w, ga_b, gx_b):
    _, heads, hd, _ = ga_w.shape
    per = LANES // hd
    nj = heads // per
    w4 = jnp.stack([ga_w[0], gx_w[0], ga_w[1], gx_w[1]]).reshape(4, nj, per, hd, hd)
    eye = jnp.eye(per, dtype=w4.dtype)
    wg = jnp.einsum('kjhde,hg->jhdkge', w4, eye).reshape(nj, LANES, 4 * LANES)
    b4 = 0.5 * jnp.stack([ga_b[0], gx_b[0], ga_b[1], gx_b[1]]).reshape(4, nj, LANES)
    gb = jnp.transpose(b4, (1, 0, 2)).reshape(nj, 1, 4 * LANES)
    return wg.astype(BF16), gb


def kernel(x_prompt, x_sample, state_lru, c, c_ctx, mod_w, mod_b, norm_mix, norm_ffn, lru_w_in, lru_conv_w, lru_conv_b, lru_ga_w, lru_ga_b, lru_gx_w, lru_gx_b, lru_lambda, lru_w_out, cm_w_in, cm_b_in, cm_ln_g, cm_ln_b, cm_w_s, cm_b_s, cm_w_out, ffn_w_gate, ffn_w_up, ffn_w_down, final_norm):
    bp, sp, d = x_prompt.shape
    bs, ss, _ = x_sample.shape
    dims = ((bp, sp), (bs, ss))
    light, heavy = _Tiles(dims, TM_LIGHT), _Tiles(dims, TM_HEAVY)
    xp = x_prompt.reshape(bp * sp, d)
    xs = x_sample.reshape(bs * ss, d)

    cond = jnp.concatenate([c_ctx[None], c, jnp.zeros((COND_ROWS - 1 - bs, d), F32)], 0)
    mods = _adaln(cond, mod_w, mod_b).reshape(-1, N_MOD, d)

    w = lru_w_out.shape[1]
    xbp, xbs = _lru_in(light, xp, xs, mods, norm_mix[0:1], lru_w_in[0, :, :w].astype(BF16), 0,
                       SUBLANES)
    wg, gb = _gate_weights(lru_ga_w[0], lru_gx_w[0], lru_ga_b[0], lru_gx_b[0])
    rec = (lru_conv_w[0], lru_conv_b[0:1], wg, gb, lru_lambda[0])
    hp, st, (wi0, wo0, wg0, wu0, wd0) = _lru_rec(
        xbp, *rec, None,
        [(lru_w_in, 0), (lru_w_out, 0), (ffn_w_gate, 0), (ffn_w_up, 0), (ffn_w_down, 0)])
    hs, _, (wi1, wo1, wg1, wu1, wd1) = _lru_rec(
        xbs, *rec, jnp.transpose(state_lru[:, 0], (1, 0, 2)),
        [(cm_w_in, 0), (cm_w_out, 0), (ffn_w_gate, 1), (ffn_w_up, 1), (ffn_w_down, 1)])
    x = _lru_out(heavy, xp, xs, hp, hs, mods, norm_mix[0:1], norm_ffn[0:1], wi0, wo0, wg0, wu0,
                 wd0, 0)

    n_groups = cm_w_s.shape[1]
    gd = cm_w_out.shape[1] // n_groups
    bs_tile = jnp.repeat(cm_b_s[0].T, gd, axis=1)
    yp, ys = _cm_layer(heavy, x, mods, norm_mix[1:2], norm_ffn[1:2], final_norm[None],
                       wi1, cm_b_in[0:1], cm_ln_g[0:1], cm_ln_b[0:1],
                       cm_w_s[0].astype(BF16), bs_tile, wo1, wg1, wu1, wd1, 1)

    new_state = jnp.transpose(st, (1, 0, 2))[:, None]
    return (yp.reshape(bp, sp, d), ys.reshape(bs, ss, d), new_state)
```

```python
import functools

import jax
import jax.numpy as jnp
from jax import lax
from jax.experimental import pallas as pl
from jax.experimental.pallas import tpu as pltpu

F32 = jnp.float32
BF16 = jnp.bfloat16

EPS = 1e-6
LRU_C = 8.0
LOG2_E = 1.4426950408889634
N_MOD = 6
COND_ROWS = 16
LANES = 128
SUBLANES = 8
CHUNK = 128
TM_LIGHT = 1024
TM_HEAVY = 512
SUB = 256
GATE_ROWS = 512
VMEM_LIMIT = 56 * 1024 * 1024


def _rmsnorm(x, g):
    y = x * lax.rsqrt(jnp.mean(x * x, axis=-1, keepdims=True) + EPS)
    return y * g


def _modulated_norm(x, g, shift, scale):
    return _rmsnorm(x, g) * (1.0 + scale) + shift


def _dot(a, b):
    return jnp.dot(a.astype(BF16), b, preferred_element_type=F32)


def _adaln_kernel(c_ref, w_ref, b_ref, o_ref):
    s = jax.nn.silu(c_ref[...])
    bias = b_ref[pl.ds(pl.program_id(0), 1), :]
    o_ref[...] = _dot(s, w_ref[...].astype(BF16)) + bias


def _adaln(cond, mod_w, mod_b):
    depth, d, n = mod_w.shape
    tn = 1024
    return pl.pallas_call(
        _adaln_kernel,
        grid=(depth, n // tn),
        in_specs=[
            pl.BlockSpec((COND_ROWS, d), lambda l, j: (0, 0)),
            pl.BlockSpec((None, d, tn), lambda l, j: (l, 0, j)),
            pl.BlockSpec((depth, tn), lambda l, j: (0, j)),
        ],
        out_specs=pl.BlockSpec((None, COND_ROWS, tn), lambda l, j: (l, 0, j)),
        out_shape=jax.ShapeDtypeStruct((depth, COND_ROWS, n), F32),
        compiler_params=pltpu.CompilerParams(
            dimension_semantics=("arbitrary", "arbitrary"), vmem_limit_bytes=VMEM_LIMIT),
        name="adaln",
    )(cond, mod_w, mod_b)


class _Tiles:
    def __init__(self, dims, tm):
        (self.bp, self.sp), (self.bs, self.ss) = dims
        assert tm % self.sp == 0 and self.ss % tm == 0 and (self.bp * self.sp) % tm == 0
        assert tm % SUB == 0
        self.tm = tm
        self.npt = self.bp * self.sp // tm
        self.nst = self.bs * self.ss // tm
        self.n = self.npt + self.nst

    def subs(self):
        return [slice(s0, s0 + SUB) for s0 in range(0, self.tm, SUB)]

    def is_prompt(self):
        return pl.program_id(0) < self.npt

    def combined(self, cols):
        return pl.BlockSpec((self.tm, cols), lambda i: (i, 0))

    def _pi(self, i):
        return jnp.minimum(i, self.npt - 1)

    def _si(self, i):
        return jnp.maximum(i - self.npt, 0)

    def flat_pair(self, cols):
        return [pl.BlockSpec((self.tm, cols), lambda i: (self._pi(i), 0)),
                pl.BlockSpec((self.tm, cols), lambda i: (self._si(i), 0))]

    def padded_pair(self, cols):
        ts = self.ss // self.tm
        return [pl.BlockSpec((self.tm // self.sp, self.sp, cols), lambda i: (self._pi(i), 0, 0)),
                pl.BlockSpec((None, self.tm, cols),
                             lambda i: (self._si(i) // ts, self._si(i) % ts, 0))]

    def padded_out_pair(self, cols, pad):
        ts = self.ss // self.tm
        return [pl.BlockSpec((self.tm // self.sp, self.sp + pad, cols), lambda i: (self._pi(i), 0, 0)),
                pl.BlockSpec((None, self.ss + pad, cols), lambda i: (self._si(i) // ts, 0, 0))]

    def branch(self, prompt_fn, sample_fn):
        pl.when(self.is_prompt())(prompt_fn)
        pl.when(jnp.logical_not(self.is_prompt()))(sample_fn)

    def write_padded_prompt(self, p_ref, value):
        nb, rows, cols = p_ref.shape
        p_ref[:, :self.sp, :] = value.reshape(nb, self.sp, cols)
        p_ref[:, self.sp:, :] = jnp.zeros((nb, rows - self.sp, cols), value.dtype)

    def write_padded_sample(self, s_ref, value):
        rows, cols = s_ref.shape
        part = self._si(pl.program_id(0)) % (self.ss // self.tm)
        s_ref[pl.ds(pl.multiple_of(part * self.tm, self.tm), self.tm), :] = value
        s_ref[self.ss:, :] = jnp.zeros((rows - self.ss, cols), value.dtype)

    def mod(self, layer, mods):
        return pl.BlockSpec((None,) + mods.shape[1:], lambda i: (layer, 0, 0))

    def mods(self, mod_ref):
        i = pl.program_id(0)
        row = jnp.where(i < self.npt, 0, 1 + self._si(i) * self.tm // self.ss)
        m = mod_ref[pl.ds(row, 1), :]
        d = m.shape[1] // N_MOD
        return [m[:, k * d:(k + 1) * d] for k in range(N_MOD)]

    def read(self, p_ref, s_ref):
        return jnp.where(self.is_prompt(), p_ref[...].reshape(s_ref.shape), s_ref[...])


def _const_spec(shape):
    nd = len(shape)
    return pl.BlockSpec(shape, lambda i: (0,) * nd, pipeline_mode=pl.Buffered(1))


_TOKEN_PARAMS = pltpu.CompilerParams(
    dimension_semantics=("arbitrary",), vmem_limit_bytes=VMEM_LIMIT)


def _lru_in_kernel(tiles, xp_ref, xs_ref, mod_ref, g_ref, w_ref, xbp_ref, xbs_ref):
    def project(x_ref):
        m = tiles.mods(mod_ref)
        return jnp.concatenate(
            [_dot(_modulated_norm(x_ref[r, :], g_ref[...], m[0], m[1]), w_ref[...])
             for r in tiles.subs()], axis=0)

    tiles.branch(lambda: tiles.write_padded_prompt(xbp_ref, project(xp_ref)),
                 lambda: tiles.write_padded_sample(xbs_ref, project(xs_ref)))


def _lru_in(tiles, xp, xs, mods, g, w_x, layer, pad):
    d, w = w_x.shape
    return pl.pallas_call(
        functools.partial(_lru_in_kernel, tiles),
        grid=(tiles.n,),
        in_specs=tiles.flat_pair(d) + [tiles.mod(layer, mods), _const_spec((1, d)),
                                       _const_spec((d, w))],
        out_specs=tiles.padded_out_pair(w, pad),
        out_shape=[jax.ShapeDtypeStruct((tiles.bp, tiles.sp + pad, w), F32),
                   jax.ShapeDtypeStruct((tiles.bs, tiles.ss + pad, w), F32)],
        compiler_params=_TOKEN_PARAMS,
        name="lru_in",
    )(xp, xs, mods, g, w_x)


def _lru_rec_kernel(B, S, P, has_h0, n_cast, *refs):
    n_in = 6 + int(has_h0)
    x_ref, cw_ref, cb_ref, wg_ref, gb_ref, lam_ref = refs[:6]
    h0_ref = refs[6] if has_h0 else None
    cast_src = refs[n_in:n_in + n_cast]
    out_ref, st_ref = refs[n_in + n_cast:n_in + n_cast + 2]
    cast_dst = refs[n_in + n_cast + 2:n_in + 2 * n_cast + 2]
    xt, xhb, hfb, a_e, bx_e, a_o, bx_o, g_e, g_o = refs[n_in + 2 * n_cast + 2:]
    rows = B * S
    R = GATE_ROWS
    n = rows // R
    steps = R // B
    assert n % 2 == 0 and n >= 4 and R % B == 0

    for src, dst in zip(cast_src, cast_dst):
        dst[...] = src[...].astype(BF16)

    for b in range(B):
        out_ref[pl.ds(b * P + S, P - S), :] = jnp.zeros((P - S, LANES), F32)

    xt[pl.ds(0, B), :] = jnp.zeros((B, LANES), F32)
    xt[pl.ds(rows + B, 2 * B), :] = jnp.zeros((2 * B, LANES), F32)

    def gather(t, _):
        xt[pl.ds(pl.multiple_of((t + 1) * B, B), B), :] = x_ref[pl.ds(t, B, stride=P), :]
        return 0
    lax.fori_loop(0, S, gather, 0, unroll=4)

    cw = 0.5 * cw_ref[...]
    cb = 0.5 * cb_ref[...]
    gb = gb_ref[...]
    neg = -lam_ref[...]
    sp = jnp.maximum(neg, 0.0) + jnp.log1p(jnp.exp(-jnp.abs(neg)))
    c_pos = (0.5 * LRU_C) * sp
    c_exp2 = -LOG2_E * c_pos

    def preact(z, c, g_dst):
        r0 = pl.multiple_of(c * R, R)
        if z == 0:
            xh = (cw[0:1] * xt[pl.ds(r0, R), :]
                  + cw[1:2] * xt[pl.ds(r0 + B, R), :]
                  + cw[2:3] * xt[pl.ds(r0 + 2 * B, R), :]
                  + cw[3:4] * xt[pl.ds(r0 + 3 * B, R), :]) + cb
            xhb[pl.ds(r0, R), :] = xh
        else:
            xh = xhb[pl.ds(r0, R), :]
        cols = slice(z * 2 * LANES, (z + 1) * 2 * LANES)
        g_dst[...] = _dot(xh, wg_ref[:, cols]) + gb[:, cols]

    def gates(z, c, g_src, a_dst, bx_dst):
        xh = xhb[pl.ds(pl.multiple_of(c * R, R), R), :]
        r2 = jnp.tanh(g_src[:, :LANES]) + 1.0
        i2 = jnp.tanh(g_src[:, LANES:]) + 1.0
        a = jnp.exp2(r2 * c_exp2[z:z + 1])
        om = jnp.tanh(r2 * c_pos[z:z + 1]) * (a * a + 1.0)
        mult = jnp.where(om > 0.0, om * lax.rsqrt(om), 0.0)
        a_dst[...] = a
        bx_dst[...] = mult * (i2 * xh)

    def scan_f(c, a_src, bx_src, h):
        for k in range(steps):
            h = a_src[k * B:(k + 1) * B, :] * h + bx_src[k * B:(k + 1) * B, :]
            hfb[pl.ds(pl.multiple_of(c * R + k * B, B), B), :] = h
        return h

    def scan_b(c, a_src, bx_src, h):
        for k in reversed(range(steps)):
            h = a_src[k * B:(k + 1) * B, :] * h + bx_src[k * B:(k + 1) * B, :]
            hf = hfb[pl.ds(pl.multiple_of(c * R + k * B, B), B), :]
            out_ref[pl.ds(c * steps + k, B, stride=P), :] = hf + h
        return h

    def run_pass(z, chunk, scan, h):
        even, odd = (g_e, a_e, bx_e), (g_o, a_o, bx_o)
        preact(z, chunk(0), g_e)
        gates(z, chunk(0), *even)
        preact(z, chunk(1), g_o)

        def body(k, h):
            i = 2 * k + 1
            preact(z, chunk(i + 1), g_e)
            gates(z, chunk(i), *odd)
            h = scan(chunk(i - 1), a_e, bx_e, h)
            preact(z, chunk(i + 2), g_o)
            gates(z, chunk(i + 1), *even)
            return scan(chunk(i), a_o, bx_o, h)
        h = lax.fori_loop(0, n // 2 - 1, body, h)
        gates(z, chunk(n - 1), *odd)
        h = scan(chunk(n - 2), a_e, bx_e, h)
        return scan(chunk(n - 1), a_o, bx_o, h)

    if has_h0:
        h0f, h0b = h0_ref[0], h0_ref[1]
    else:
        h0f = h0b = jnp.zeros((B, LANES), F32)
    st_ref[0] = run_pass(0, lambda i: i, scan_f, h0f)
    st_ref[1] = run_pass(1, lambda i: n - 1 - i, scan_b, h0b)


def _lru_rec(xb, conv_w, conv_b, wg, gb, lam, h0, casts):
    B, P, w = xb.shape
    S = P - SUBLANES
    rows = B * S
    nj = w // LANES
    has_h0 = h0 is not None
    slab = pl.BlockSpec((B * P, LANES), lambda j: (0, j))
    state = pl.BlockSpec((2, B, LANES), lambda j: (0, 0, j))
    in_specs = [
        slab,
        pl.BlockSpec((4, LANES), lambda j: (0, j)),
        pl.BlockSpec((1, LANES), lambda j: (0, j)),
        pl.BlockSpec((None, LANES, 4 * LANES), lambda j: (j, 0, 0)),
        pl.BlockSpec((None, 1, 4 * LANES), lambda j: (j, 0, 0)),
        pl.BlockSpec((2, LANES), lambda j: (0, j)),
    ]
    args = [xb.reshape(B * P, w), conv_w, conv_b, wg, gb, lam]
    if has_h0:
        in_specs.append(state)
        args.append(h0)
    out_specs = [slab, state]
    out_shape = [jax.ShapeDtypeStruct((B * P, w), F32), jax.ShapeDtypeStruct((2, B, w), F32)]
    for wt, layer in casts:
        _, r, c = wt.shape
        assert r % (nj * 2 * SUBLANES) == 0
        in_specs.append(pl.BlockSpec((None, r // nj, c), functools.partial(
            lambda j, layer: (layer, j, 0), layer=layer)))
        args.append(wt)
        out_specs.append(pl.BlockSpec((r // nj, c), lambda j: (j, 0)))
        out_shape.append(jax.ShapeDtypeStruct((r, c), BF16))
    res = pl.pallas_call(
        functools.partial(_lru_rec_kernel, B, S, P, has_h0, len(casts)),
        grid=(nj,),
        in_specs=in_specs,
        out_specs=out_specs,
        out_shape=out_shape,
        scratch_shapes=[pltpu.VMEM((rows + 3 * B, LANES), F32)]
        + [pltpu.VMEM((rows, LANES), F32)] * 2 + [pltpu.VMEM((GATE_ROWS, LANES), F32)] * 4
        + [pltpu.VMEM((GATE_ROWS, 2 * LANES), F32)] * 2,
        compiler_params=pltpu.CompilerParams(
            dimension_semantics=("arbitrary",), vmem_limit_bytes=VMEM_LIMIT),
        name="lru_rec",
    )(*args)
    return res[0].reshape(B, P, w), res[1], res[2:]


def _ffn(x1, m, g, wg_ref, wu_ref, wd_ref):
    h = [_modulated_norm(v, g, m[3], m[4]).astype(BF16) for v in x1]
    gu = [(jnp.dot(v, wg_ref[...], preferred_element_type=F32),
           jnp.dot(v, wu_ref[...], preferred_element_type=F32)) for v in h]
    dn = [_dot(jax.nn.silu(gate) * up, wd_ref[...]) for gate, up in gu]
    return [v + m[5] * d for v, d in zip(x1, dn)]


def _lru_out_kernel(tiles, xp_ref, xs_ref, hp_ref, hs_ref, mod_ref, gm_ref, gf_ref,
                    wy_ref, wo_ref, wg_ref, wu_ref, wd_ref, o_ref):
    subs = tiles.subs()

    def body(x_ref, hsum):
        m = tiles.mods(mod_ref)
        x = [x_ref[r, :] for r in subs]
        y = [jax.nn.gelu(_dot(_modulated_norm(v, gm_ref[...], m[0], m[1]), wy_ref[...]))
             for v in x]
        x1 = [v + m[2] * _dot(h * g, wo_ref[...]) for v, h, g in zip(x, hsum, y)]
        for r, v in zip(subs, _ffn(x1, m, gf_ref[...], wg_ref, wu_ref, wd_ref)):
            o_ref[r, :] = v

    def prompt_hsum():
        flat = hp_ref[...].reshape(tiles.tm, hp_ref.shape[-1])
        return [flat[r] for r in subs]

    tiles.branch(lambda: body(xp_ref, prompt_hsum()),
                 lambda: body(xs_ref, [hs_ref[r, :] for r in subs]))


def _lru_out(tiles, xp, xs, hp, hs, mods, g_mix, g_ffn, w_in, w_out, w_gate, w_up, w_down, layer):
    d = xp.shape[1]
    w = w_out.shape[0]
    f = w_gate.shape[1]
    w_y = pl.BlockSpec((d, w), lambda i: (0, 1), pipeline_mode=pl.Buffered(1))
    return pl.pallas_call(
        functools.partial(_lru_out_kernel, tiles),
        grid=(tiles.n,),
        in_specs=tiles.flat_pair(d) + tiles.padded_pair(w) + [
            tiles.mod(layer, mods), _const_spec((1, d)), _const_spec((1, d)), w_y,
            _const_spec((w, d)), _const_spec((d, f)), _const_spec((d, f)), _const_spec((f, d))],
        out_specs=tiles.combined(d),
        out_shape=jax.ShapeDtypeStruct((tiles.n * tiles.tm, d), F32),
        compiler_params=_TOKEN_PARAMS,
        name="lru_out_ffn",
    )(xp, xs, hp, hs, mods, g_mix, g_ffn, w_in, w_out, w_gate, w_up, w_down)


def _cm_kernel(tiles, x_ref, mod_ref, gm_ref, gf_ref, gfin_ref, wi_ref, bi_ref, lng_ref, lnb_ref,
               ws_ref, bs_ref, wo_ref, wg_ref, wu_ref, wd_ref, op_ref, os_ref):
    cw = lng_ref.shape[-1]
    n_groups = ws_ref.shape[0]
    gd = cw // n_groups

    def body(o_ref):
        m = tiles.mods(mod_ref)
        xs = [x_ref[r, :] for r in tiles.subs()]
        pre = [_dot(_modulated_norm(x, gm_ref[...], m[0], m[1]), wi_ref[...]) + bi_ref[...]
               for x in xs]
        x1 = []
        for x, z in zip(xs, pre):
            uv = jax.nn.gelu(z)
            u = uv[:, :cw]
            v = uv[:, cw:]
            mu = jnp.mean(v, axis=-1, keepdims=True)
            var = jnp.mean(jnp.square(v - mu), axis=-1, keepdims=True)
            v = ((v - mu) * lax.rsqrt(var + EPS) * lng_ref[...] + lnb_ref[...]).astype(BF16)
            mixed_rows = []
            for c in range(SUB // CHUNK):
                vc = v[c * CHUNK:(c + 1) * CHUNK]
                cols = [jnp.dot(ws_ref[g], vc[:, g * gd:(g + 1) * gd],
                                preferred_element_type=F32) for g in range(n_groups)]
                mixed_rows.append(jnp.concatenate(cols, axis=1) + bs_ref[...])
            mixed = jnp.concatenate(mixed_rows, axis=0)
            x1.append(x + m[2] * _dot(u * mixed, wo_ref[...]))
        x2 = _ffn(x1, m, gf_ref[...], wg_ref, wu_ref, wd_ref)
        for r, v in zip(tiles.subs(), x2):
            o_ref[r, :] = _rmsnorm(v, gfin_ref[...])

    tiles.branch(lambda: body(op_ref), lambda: body(os_ref))


def _cm_layer(tiles, x, mods, g_mix, g_ffn, g_final, w_in, b_in, ln_g, ln_b, w_s, b_s, w_out,
              w_gate, w_up, w_down, layer):
    d = x.shape[1]
    cw = w_out.shape[0]
    f = w_gate.shape[1]
    return pl.pallas_call(
        functools.partial(_cm_kernel, tiles),
        grid=(tiles.n,),
        in_specs=[tiles.combined(d), tiles.mod(layer, mods),
                  _const_spec((1, d)), _const_spec((1, d)), _const_spec((1, d)),
                  _const_spec((d, 2 * cw)), _const_spec((1, 2 * cw)),
                  _const_spec((1, cw)), _const_spec((1, cw)),
                  _const_spec(w_s.shape), _const_spec(b_s.shape), _const_spec((cw, d)),
                  _const_spec((d, f)), _const_spec((d, f)), _const_spec((f, d))],
        out_specs=tiles.flat_pair(d),
        out_shape=[jax.ShapeDtypeStruct((tiles.npt * tiles.tm, d), F32),
                   jax.ShapeDtypeStruct((tiles.nst * tiles.tm, d), F32)],
        compiler_params=_TOKEN_PARAMS,
        name="cm_layer",
    )(x, mods, g_mix, g_ffn, g_final, w_in, b_in, ln_g, ln_b, w_s, b_s, w_out,
      w_gate, w_up, w_down)


def _gate_weights(ga_w, gx_w, ga_b, gx_b):
    _, heads, hd, _ = ga_w.shape
    per = LANES // hd
    nj = heads // per
    w4 = jnp.stack([ga_w[0], gx_w[0], ga_w[1], gx_w[1]]).reshape(4, nj, per, hd, hd)
    eye = jnp.eye(per, dtype=w4.dtype)
    wg = jnp.einsum('kjhde,hg->jhdkge', w4, eye).reshape(nj, LANES, 4 * LANES)
    b4 = 0.5 * jnp.stack([ga_b[0], gx_b[0], ga_b[1], gx_b[1]]).reshape(4, nj, LANES)
    gb = jnp.transpose(b4, (1, 0, 2)).reshape(nj, 1, 4 * LANES)
    return wg.astype(BF16), gb


def kernel(x_prompt, x_sample, state_lru, c, c_ctx, mod_w, mod_b, norm_mix, norm_ffn, lru_w_in, lru_conv_w, lru_conv_b, lru_ga_w, lru_ga_b, lru_gx_w, lru_gx_b, lru_lambda, lru_w_out, cm_w_in, cm_b_in, cm_ln_g, cm_ln_b, cm_w_s, cm_b_s, cm_w_out, ffn_w_gate, ffn_w_up, ffn_w_down, final_norm):
    bp, sp, d = x_prompt.shape
    bs, ss, _ = x_sample.shape
    dims = ((bp, sp), (bs, ss))
    light, heavy = _Tiles(dims, TM_LIGHT), _Tiles(dims, TM_HEAVY)
    xp = x_prompt.reshape(bp * sp, d)
    xs = x_sample.reshape(bs * ss, d)

    cond = jnp.concatenate([c_ctx[None], c, jnp.zeros((COND_ROWS - 1 - bs, d), F32)], 0)
    mods = _adaln(cond, mod_w, mod_b)

    w = lru_w_out.shape[1]
    xbp, xbs = _lru_in(light, xp, xs, mods, norm_mix[0:1], lru_w_in[0, :, :w].astype(BF16), 0,
                       SUBLANES)
    wg, gb = _gate_weights(lru_ga_w[0], lru_gx_w[0], lru_ga_b[0], lru_gx_b[0])
    rec = (lru_conv_w[0], lru_conv_b[0:1], wg, gb, lru_lambda[0])
    hp, st, (wi0, wo0, wg0, wu0, wd0) = _lru_rec(
        xbp, *rec, None,
        [(lru_w_in, 0), (lru_w_out, 0), (ffn_w_gate, 0), (ffn_w_up, 0), (ffn_w_down, 0)])
    hs, _, (wi1, wo1, wg1, wu1, wd1) = _lru_rec(
        xbs, *rec, jnp.transpose(state_lru[:, 0], (1, 0, 2)),
        [(cm_w_in, 0), (cm_w_out, 0), (ffn_w_gate, 1), (ffn_w_up, 1), (ffn_w_down, 1)])
    x = _lru_out(heavy, xp, xs, hp, hs, mods, norm_mix[0:1], norm_ffn[0:1], wi0, wo0, wg0, wu0,
                 wd0, 0)

    n_groups = cm_w_s.shape[1]
    gd = cm_w_out.shape[1] // n_groups
    bs_tile = jnp.repeat(cm_b_s[0].T, gd, axis=1)
    yp, ys = _cm_layer(heavy, x, mods, norm_mix[1:2], norm_ffn[1:2], final_norm[None],
                       wi1, cm_b_in[0:1], cm_ln_g[0:1], cm_ln_b[0:1],
                       cm_w_s[0].astype(BF16), bs_tile, wo1, wg1, wu1, wd1, 1)

    new_state = jnp.transpose(st, (1, 0, 2))[:, None]
    return (yp.reshape(bp, sp, d), ys.reshape(bs, ss, d), new_state)
```

```python
import functools

import jax
import jax.numpy as jnp
from jax import lax
from jax.experimental import pallas as pl
from jax.experimental.pallas import tpu as pltpu

F32 = jnp.float32
BF16 = jnp.bfloat16

EPS = 1e-6
LRU_C = 8.0
LOG2_E = 1.4426950408889634
N_MOD = 6
COND_ROWS = 16
LANES = 128
SUBLANES = 8
CHUNK = 128
TM_LIGHT = 1024
TM_HEAVY = 512
SUB = 256
GATE_ROWS = 512
VMEM_LIMIT = 56 * 1024 * 1024


def _rmsnorm(x, g):
    y = x * lax.rsqrt(jnp.mean(x * x, axis=-1, keepdims=True) + EPS)
    return y * g


def _modulated_norm(x, g, shift, scale):
    return _rmsnorm(x, g) * (1.0 + scale) + shift


def _dot(a, b):
    return jnp.dot(a.astype(BF16), b, preferred_element_type=F32)


def _adaln_kernel(c_ref, w_ref, b_ref, o_ref):
    s = jax.nn.silu(c_ref[...])
    bias = b_ref[pl.ds(pl.program_id(0), 1), :]
    o_ref[...] = _dot(s, w_ref[...].astype(BF16)) + bias


def _adaln(cond, mod_w, mod_b):
    depth, d, n = mod_w.shape
    tn = 1024
    return pl.pallas_call(
        _adaln_kernel,
        grid=(depth, n // tn),
        in_specs=[
            pl.BlockSpec((COND_ROWS, d), lambda l, j: (0, 0)),
            pl.BlockSpec((None, d, tn), lambda l, j: (l, 0, j)),
            pl.BlockSpec((depth, tn), lambda l, j: (0, j)),
        ],
        out_specs=pl.BlockSpec((None, COND_ROWS, tn), lambda l, j: (l, 0, j)),
        out_shape=jax.ShapeDtypeStruct((depth, COND_ROWS, n), F32),
        compiler_params=pltpu.CompilerParams(
            dimension_semantics=("arbitrary", "arbitrary"), vmem_limit_bytes=VMEM_LIMIT),
        name="adaln",
    )(cond, mod_w, mod_b)


class _Tiles:
    def __init__(self, dims, tm):
        (self.bp, self.sp), (self.bs, self.ss) = dims
        assert tm % self.sp == 0 and self.ss % tm == 0 and (self.bp * self.sp) % tm == 0
        assert tm % SUB == 0
        self.tm = tm
        self.npt = self.bp * self.sp // tm
        self.nst = self.bs * self.ss // tm
        self.n = self.npt + self.nst

    def subs(self):
        return [slice(s0, s0 + SUB) for s0 in range(0, self.tm, SUB)]

    def is_prompt(self):
        return pl.program_id(0) < self.npt

    def combined(self, cols):
        return pl.BlockSpec((self.tm, cols), lambda i: (i, 0))

    def _pi(self, i):
        return jnp.minimum(i, self.npt - 1)

    def _si(self, i):
        return jnp.maximum(i - self.npt, 0)

    def flat_pair(self, cols):
        return [pl.BlockSpec((self.tm, cols), lambda i: (self._pi(i), 0)),
                pl.BlockSpec((self.tm, cols), lambda i: (self._si(i), 0))]

    def padded_pair(self, cols):
        ts = self.ss // self.tm
        return [pl.BlockSpec((self.tm // self.sp, self.sp, cols), lambda i: (self._pi(i), 0, 0)),
                pl.BlockSpec((None, self.tm, cols),
                             lambda i: (self._si(i) // ts, self._si(i) % ts, 0))]

    def padded_out_pair(self, cols, pad):
        ts = self.ss // self.tm
        return [pl.BlockSpec((self.tm // self.sp, self.sp + pad, cols), lambda i: (self._pi(i), 0, 0)),
                pl.BlockSpec((None, self.ss + pad, cols), lambda i: (self._si(i) // ts, 0, 0))]

    def branch(self, prompt_fn, sample_fn):
        pl.when(self.is_prompt())(prompt_fn)
        pl.when(jnp.logical_not(self.is_prompt()))(sample_fn)

    def write_padded_prompt(self, p_ref, value):
        nb, rows, cols = p_ref.shape
        p_ref[:, :self.sp, :] = value.reshape(nb, self.sp, cols)
        p_ref[:, self.sp:, :] = jnp.zeros((nb, rows - self.sp, cols), value.dtype)

    def write_padded_sample(self, s_ref, value):
        rows, cols = s_ref.shape
        part = self._si(pl.program_id(0)) % (self.ss // self.tm)
        s_ref[pl.ds(pl.multiple_of(part * self.tm, self.tm), self.tm), :] = value
        s_ref[self.ss:, :] = jnp.zeros((rows - self.ss, cols), value.dtype)

    def mod(self, layer, mods):
        return pl.BlockSpec((None,) + mods.shape[1:], lambda i: (layer, 0, 0))

    def mods(self, mod_ref):
        i = pl.program_id(0)
        row = jnp.where(i < self.npt, 0, 1 + self._si(i) * self.tm // self.ss)
        m = mod_ref[pl.ds(row, 1), :]
        d = m.shape[1] // N_MOD
        return [m[:, k * d:(k + 1) * d] for k in range(N_MOD)]

    def read(self, p_ref, s_ref):
        return jnp.where(self.is_prompt(), p_ref[...].reshape(s_ref.shape), s_ref[...])


def _const_spec(shape):
    nd = len(shape)
    return pl.BlockSpec(shape, lambda i: (0,) * nd, pipeline_mode=pl.Buffered(1))


_TOKEN_PARAMS = pltpu.CompilerParams(
    dimension_semantics=("arbitrary",), vmem_limit_bytes=VMEM_LIMIT)


def _lru_in_kernel(tiles, xp_ref, xs_ref, mod_ref, g_ref, w_ref, xbp_ref, xbs_ref):
    def project(x_ref):
        m = tiles.mods(mod_ref)
        return jnp.concatenate(
            [_dot(_modulated_norm(x_ref[r, :], g_ref[...], m[0], m[1]), w_ref[...])
             for r in tiles.subs()], axis=0)

    tiles.branch(lambda: tiles.write_padded_prompt(xbp_ref, project(xp_ref)),
                 lambda: tiles.write_padded_sample(xbs_ref, project(xs_ref)))


def _lru_in(tiles, xp, xs, mods, g, w_x, layer, pad):
    d, w = w_x.shape
    return pl.pallas_call(
        functools.partial(_lru_in_kernel, tiles),
        grid=(tiles.n,),
        in_specs=tiles.flat_pair(d) + [tiles.mod(layer, mods), _const_spec((1, d)),
                                       _const_spec((d, w))],
        out_specs=tiles.padded_out_pair(w, pad),
        out_shape=[jax.ShapeDtypeStruct((tiles.bp, tiles.sp + pad, w), F32),
                   jax.ShapeDtypeStruct((tiles.bs, tiles.ss + pad, w), F32)],
        compiler_params=_TOKEN_PARAMS,
        name="lru_in",
    )(xp, xs, mods, g, w_x)


def _lru_rec_kernel(B, S, P, has_h0, n_cast, *refs):
    n_in = 6 + int(has_h0)
    x_ref, cw_ref, cb_ref, wg_ref, gb_ref, lam_ref = refs[:6]
    h0_ref = refs[6] if has_h0 else None
    cast_src = refs[n_in:n_in + n_cast]
    out_ref, st_ref = refs[n_in + n_cast:n_in + n_cast + 2]
    cast_dst = refs[n_in + n_cast + 2:n_in + 2 * n_cast + 2]
    xt, xhb, hfb, a_e, bx_e, a_o, bx_o, g_e, g_o = refs[n_in + 2 * n_cast + 2:]
    rows = B * S
    R = GATE_ROWS
    n = rows // R
    steps = R // B
    assert n % 2 == 0 and n >= 4 and R % B == 0

    for src, dst in zip(cast_src, cast_dst):
        dst[...] = src[...].astype(BF16)

    for b in range(B):
        out_ref[pl.ds(b * P + S, P - S), :] = jnp.zeros((P - S, LANES), F32)

    xt[pl.ds(0, B), :] = jnp.zeros((B, LANES), F32)
    xt[pl.ds(rows + B, 2 * B), :] = jnp.zeros((2 * B, LANES), F32)

    def gather(t, _):
        xt[pl.ds(pl.multiple_of((t + 1) * B, B), B), :] = x_ref[pl.ds(t, B, stride=P), :]
        return 0
    lax.fori_loop(0, S, gather, 0, unroll=4)

    cw = 0.5 * cw_ref[...]
    cb = 0.5 * cb_ref[...]
    wgz = [jnp.concatenate([wg_ref[2 * z], wg_ref[2 * z + 1]], axis=1) for z in range(2)]
    gbz = [jnp.concatenate([gb_ref[2 * z], gb_ref[2 * z + 1]], axis=1) for z in range(2)]
    neg = -lam_ref[...]
    sp = jnp.maximum(neg, 0.0) + jnp.log1p(jnp.exp(-jnp.abs(neg)))
    c_pos = (0.5 * LRU_C) * sp
    c_exp2 = -LOG2_E * c_pos

    def preact(z, c, g_dst):
        r0 = pl.multiple_of(c * R, R)
        if z == 0:
            xh = (cw[0:1] * xt[pl.ds(r0, R), :]
                  + cw[1:2] * xt[pl.ds(r0 + B, R), :]
                  + cw[2:3] * xt[pl.ds(r0 + 2 * B, R), :]
                  + cw[3:4] * xt[pl.ds(r0 + 3 * B, R), :]) + cb
            xhb[pl.ds(r0, R), :] = xh
        else:
            xh = xhb[pl.ds(r0, R), :]
        g_dst[...] = _dot(xh, wgz[z]) + gbz[z]

    def gates(z, c, g_src, a_dst, bx_dst):
        xh = xhb[pl.ds(pl.multiple_of(c * R, R), R), :]
        r2 = jnp.tanh(g_src[:, :LANES]) + 1.0
        i2 = jnp.tanh(g_src[:, LANES:]) + 1.0
        a = jnp.exp2(r2 * c_exp2[z:z + 1])
        om = jnp.tanh(r2 * c_pos[z:z + 1]) * (a * a + 1.0)
        mult = jnp.where(om > 0.0, om * lax.rsqrt(om), 0.0)
        a_dst[...] = a
        bx_dst[...] = mult * (i2 * xh)

    def scan_f(c, a_src, bx_src, h):
        for k in range(steps):
            h = a_src[k * B:(k + 1) * B, :] * h + bx_src[k * B:(k + 1) * B, :]
            hfb[pl.ds(pl.multiple_of(c * R + k * B, B), B), :] = h
        return h

    def scan_b(c, a_src, bx_src, h):
        for k in reversed(range(steps)):
            h = a_src[k * B:(k + 1) * B, :] * h + bx_src[k * B:(k + 1) * B, :]
            hf = hfb[pl.ds(pl.multiple_of(c * R + k * B, B), B), :]
            out_ref[pl.ds(c * steps + k, B, stride=P), :] = hf + h
        return h

    def run_pass(z, chunk, scan, h):
        even, odd = (g_e, a_e, bx_e), (g_o, a_o, bx_o)
        preact(z, chunk(0), g_e)
        gates(z, chunk(0), *even)
        preact(z, chunk(1), g_o)

        def body(k, h):
            i = 2 * k + 1
            preact(z, chunk(i + 1), g_e)
            gates(z, chunk(i), *odd)
            h = scan(chunk(i - 1), a_e, bx_e, h)
            preact(z, chunk(i + 2), g_o)
            gates(z, chunk(i + 1), *even)
            return scan(chunk(i), a_o, bx_o, h)
        h = lax.fori_loop(0, n // 2 - 1, body, h)
        gates(z, chunk(n - 1), *odd)
        h = scan(chunk(n - 2), a_e, bx_e, h)
        return scan(chunk(n - 1), a_o, bx_o, h)

    if has_h0:
        h0f, h0b = h0_ref[0], h0_ref[1]
    else:
        h0f = h0b = jnp.zeros((B, LANES), F32)
    st_ref[0] = run_pass(0, lambda i: i, scan_f, h0f)
    st_ref[1] = run_pass(1, lambda i: n - 1 - i, scan_b, h0b)


def _lru_rec(xb, conv_w, conv_b, wg, gb, lam, h0, casts):
    B, P, w = xb.shape
    S = P - SUBLANES
    rows = B * S
    nj = w // LANES
    has_h0 = h0 is not None
    slab = pl.BlockSpec((B * P, LANES), lambda j: (0, j))
    state = pl.BlockSpec((2, B, LANES), lambda j: (0, 0, j))
    in_specs = [
        slab,
        pl.BlockSpec((4, LANES), lambda j: (0, j)),
        pl.BlockSpec((1, LANES), lambda j: (0, j)),
        pl.BlockSpec((4, None, LANES, LANES), lambda j: (0, j, 0, 0)),
        pl.BlockSpec((4, None, 1, LANES), lambda j: (0, j, 0, 0)),
        pl.BlockSpec((2, LANES), lambda j: (0, j)),
    ]
    args = [xb.reshape(B * P, w), conv_w, conv_b, wg, gb, lam]
    if has_h0:
        in_specs.append(state)
        args.append(h0)
    out_specs = [slab, state]
    out_shape = [jax.ShapeDtypeStruct((B * P, w), F32), jax.ShapeDtypeStruct((2, B, w), F32)]
    for wt, layer in casts:
        _, r, c = wt.shape
        assert r % (nj * 2 * SUBLANES) == 0
        in_specs.append(pl.BlockSpec((None, r // nj, c), functools.partial(
            lambda j, layer: (layer, j, 0), layer=layer)))
        args.append(wt)
        out_specs.append(pl.BlockSpec((r // nj, c), lambda j: (j, 0)))
        out_shape.append(jax.ShapeDtypeStruct((r, c), BF16))
    res = pl.pallas_call(
        functools.partial(_lru_rec_kernel, B, S, P, has_h0, len(casts)),
        grid=(nj,),
        in_specs=in_specs,
        out_specs=out_specs,
        out_shape=out_shape,
        scratch_shapes=[pltpu.VMEM((rows + 3 * B, LANES), F32)]
        + [pltpu.VMEM((rows, LANES), F32)] * 2 + [pltpu.VMEM((GATE_ROWS, LANES), F32)] * 4
        + [pltpu.VMEM((GATE_ROWS, 2 * LANES), F32)] * 2,
        compiler_params=pltpu.CompilerParams(
            dimension_semantics=("arbitrary",), vmem_limit_bytes=VMEM_LIMIT),
        name="lru_rec",
    )(*args)
    return res[0].reshape(B, P, w), res[1], res[2:]


def _ffn(x1, m, g, wg_ref, wu_ref, wd_ref):
    h = [_modulated_norm(v, g, m[3], m[4]).astype(BF16) for v in x1]
    gu = [(jnp.dot(v, wg_ref[...], preferred_element_type=F32),
           jnp.dot(v, wu_ref[...], preferred_element_type=F32)) for v in h]
    dn = [_dot(jax.nn.silu(gate) * up, wd_ref[...]) for gate, up in gu]
    return [v + m[5] * d for v, d in zip(x1, dn)]


def _lru_out_kernel(tiles, xp_ref, xs_ref, hp_ref, hs_ref, mod_ref, gm_ref, gf_ref,
                    wy_ref, wo_ref, wg_ref, wu_ref, wd_ref, o_ref):
    subs = tiles.subs()

    def body(x_ref, hsum):
        m = tiles.mods(mod_ref)
        x = [x_ref[r, :] for r in subs]
        y = [jax.nn.gelu(_dot(_modulated_norm(v, gm_ref[...], m[0], m[1]), wy_ref[...]))
             for v in x]
        x1 = [v + m[2] * _dot(h * g, wo_ref[...]) for v, h, g in zip(x, hsum, y)]
        for r, v in zip(subs, _ffn(x1, m, gf_ref[...], wg_ref, wu_ref, wd_ref)):
            o_ref[r, :] = v

    def prompt_hsum():
        flat = hp_ref[...].reshape(tiles.tm, hp_ref.shape[-1])
        return [flat[r] for r in subs]

    tiles.branch(lambda: body(xp_ref, prompt_hsum()),
                 lambda: body(xs_ref, [hs_ref[r, :] for r in subs]))


def _lru_out(tiles, xp, xs, hp, hs, mods, g_mix, g_ffn, w_in, w_out, w_gate, w_up, w_down, layer):
    d = xp.shape[1]
    w = w_out.shape[0]
    f = w_gate.shape[1]
    w_y = pl.BlockSpec((d, w), lambda i: (0, 1), pipeline_mode=pl.Buffered(1))
    return pl.pallas_call(
        functools.partial(_lru_out_kernel, tiles),
        grid=(tiles.n,),
        in_specs=tiles.flat_pair(d) + tiles.padded_pair(w) + [
            tiles.mod(layer, mods), _const_spec((1, d)), _const_spec((1, d)), w_y,
            _const_spec((w, d)), _const_spec((d, f)), _const_spec((d, f)), _const_spec((f, d))],
        out_specs=tiles.combined(d),
        out_shape=jax.ShapeDtypeStruct((tiles.n * tiles.tm, d), F32),
        compiler_params=_TOKEN_PARAMS,
        name="lru_out_ffn",
    )(xp, xs, hp, hs, mods, g_mix, g_ffn, w_in, w_out, w_gate, w_up, w_down)


def _cm_kernel(tiles, x_ref, mod_ref, gm_ref, gf_ref, gfin_ref, wi_ref, bi_ref, lng_ref, lnb_ref,
               ws_ref, bs_ref, wo_ref, wg_ref, wu_ref, wd_ref, op_ref, os_ref):
    cw = lng_ref.shape[-1]
    n_groups = ws_ref.shape[0]
    gd = cw // n_groups

    def body():
        m = tiles.mods(mod_ref)
        xs = [x_ref[r, :] for r in tiles.subs()]
        pre = [_dot(_modulated_norm(x, gm_ref[...], m[0], m[1]), wi_ref[...]) + bi_ref[...]
               for x in xs]
        x1 = []
        for x, z in zip(xs, pre):
            uv = jax.nn.gelu(z)
            u = uv[:, :cw]
            v = uv[:, cw:]
            mu = jnp.mean(v, axis=-1, keepdims=True)
            var = jnp.mean(jnp.square(v - mu), axis=-1, keepdims=True)
            v = ((v - mu) * lax.rsqrt(var + EPS) * lng_ref[...] + lnb_ref[...]).astype(BF16)
            mixed_rows = []
            for c in range(SUB // CHUNK):
                vc = v[c * CHUNK:(c + 1) * CHUNK]
                cols = [jnp.dot(ws_ref[g], vc[:, g * gd:(g + 1) * gd],
                                preferred_element_type=F32) for g in range(n_groups)]
                mixed_rows.append(jnp.concatenate(cols, axis=1) + bs_ref[...])
            mixed = jnp.concatenate(mixed_rows, axis=0)
            x1.append(x + m[2] * _dot(u * mixed, wo_ref[...]))
        x2 = _ffn(x1, m, gf_ref[...], wg_ref, wu_ref, wd_ref)
        return jnp.concatenate([_rmsnorm(v, gfin_ref[...]) for v in x2], axis=0)

    out = body()

    def put(o_ref):
        o_ref[...] = out
    tiles.branch(lambda: put(op_ref), lambda: put(os_ref))


def _cm_layer(tiles, x, mods, g_mix, g_ffn, g_final, w_in, b_in, ln_g, ln_b, w_s, b_s, w_out,
              w_gate, w_up, w_down, layer):
    d = x.shape[1]
    cw = w_out.shape[0]
    f = w_gate.shape[1]
    return pl.pallas_call(
        functools.partial(_cm_kernel, tiles),
        grid=(tiles.n,),
        in_specs=[tiles.combined(d), tiles.mod(layer, mods),
                  _const_spec((1, d)), _const_spec((1, d)), _const_spec((1, d)),
                  _const_spec((d, 2 * cw)), _const_spec((1, 2 * cw)),
                  _const_spec((1, cw)), _const_spec((1, cw)),
                  _const_spec(w_s.shape), _const_spec(b_s.shape), _const_spec((cw, d)),
                  _const_spec((d, f)), _const_spec((d, f)), _const_spec((f, d))],
        out_specs=tiles.flat_pair(d),
        out_shape=[jax.ShapeDtypeStruct((tiles.npt * tiles.tm, d), F32),
                   jax.ShapeDtypeStruct((tiles.nst * tiles.tm, d), F32)],
        compiler_params=_TOKEN_PARAMS,
        name="cm_layer",
    )(x, mods, g_mix, g_ffn, g_final, w_in, b_in, ln_g, ln_b, w_s, b_s, w_out,
      w_gate, w_up, w_down)


def _gate_weights(ga_w, gx_w, ga_b, gx_b):
    _, heads, hd, _ = ga_w.shape
    per = LANES // hd
    nj = heads // per
    w4 = jnp.stack([ga_w[0], gx_w[0], ga_w[1], gx_w[1]]).reshape(4, nj, per, hd, hd)
    rows = []
    for h in range(per):
        blocks = [w4[:, :, h] if g == h else jnp.zeros_like(w4[:, :, h]) for g in range(per)]
        rows.append(jnp.concatenate(blocks, axis=-1))
    wg = jnp.concatenate(rows, axis=-2).astype(BF16)
    gb = 0.5 * jnp.stack([ga_b[0], gx_b[0], ga_b[1], gx_b[1]]).reshape(4, nj, 1, LANES)
    return wg, gb


def kernel(x_prompt, x_sample, state_lru, c, c_ctx, mod_w, mod_b, norm_mix, norm_ffn, lru_w_in, lru_conv_w, lru_conv_b, lru_ga_w, lru_ga_b, lru_gx_w, lru_gx_b, lru_lambda, lru_w_out, cm_w_in, cm_b_in, cm_ln_g, cm_ln_b, cm_w_s, cm_b_s, cm_w_out, ffn_w_gate, ffn_w_up, ffn_w_down, final_norm):
    bp, sp, d = x_prompt.shape
    bs, ss, _ = x_sample.shape
    dims = ((bp, sp), (bs, ss))
    light, heavy = _Tiles(dims, TM_LIGHT), _Tiles(dims, TM_HEAVY)
    xp = x_prompt.reshape(bp * sp, d)
    xs = x_sample.reshape(bs * ss, d)

    cond = jnp.concatenate([c_ctx[None], c, jnp.zeros((COND_ROWS - 1 - bs, d), F32)], 0)
    mods = _adaln(cond, mod_w, mod_b)

    w = lru_w_out.shape[1]
    xbp, xbs = _lru_in(light, xp, xs, mods, norm_mix[0:1], lru_w_in[0, :, :w].astype(BF16), 0,
                       SUBLANES)
    wg, gb = _gate_weights(lru_ga_w[0], lru_gx_w[0], lru_ga_b[0], lru_gx_b[0])
    rec = (lru_conv_w[0], lru_conv_b[0:1], wg, gb, lru_lambda[0])
    hp, st, (wi0, wo0, wg0, wu0, wd0) = _lru_rec(
        xbp, *rec, None,
        [(lru_w_in, 0), (lru_w_out, 0), (ffn_w_gate, 0), (ffn_w_up, 0), (ffn_w_down, 0)])
    hs, _, (wi1, wo1, wg1, wu1, wd1) = _lru_rec(
        xbs, *rec, jnp.transpose(state_lru[:, 0], (1, 0, 2)),
        [(cm_w_in, 0), (cm_w_out, 0), (ffn_w_gate, 1), (ffn_w_up, 1), (ffn_w_down, 1)])
    x = _lru_out(heavy, xp, xs, hp, hs, mods, norm_mix[0:1], norm_ffn[0:1], wi0, wo0, wg0, wu0,
                 wd0, 0)

    n_groups = cm_w_s.shape[1]
    gd = cm_w_out.shape[1] // n_groups
    bs_tile = jnp.repeat(cm_b_s[0].T, gd, axis=1)
    yp, ys = _cm_layer(heavy, x, mods, norm_mix[1:2], norm_ffn[1:2], final_norm[None],
                       wi1, cm_b_in[0:1], cm_ln_g[0:1], cm_ln_b[0:1],
                       cm_w_s[0].astype(BF16), bs_tile, wo1, wg1, wu1, wd1, 1)

    new_state = jnp.transpose(st, (1, 0, 2))[:, None]
    return (yp.reshape(bp, sp, d), ys.reshape(bs, ss, d), new_state)
```

```python
import functools

import jax
import jax.numpy as jnp
from jax import lax
from jax.experimental import pallas as pl
from jax.experimental.pallas import tpu as pltpu

F32 = jnp.float32
BF16 = jnp.bfloat16

EPS = 1e-6
LRU_C = 8.0
LOG2_E = 1.4426950408889634
N_MOD = 6
COND_ROWS = 16
LANES = 128
SUBLANES = 8
CHUNK = 128
TM_LIGHT = 1024
TM_HEAVY = 512
SUB = 256
GATE_ROWS = 512
VMEM_LIMIT = 56 * 1024 * 1024


def _rmsnorm(x, g):
    y = x * lax.rsqrt(jnp.mean(x * x, axis=-1, keepdims=True) + EPS)
    return y * g


def _modulated_norm(x, g, shift, scale):
    return _rmsnorm(x, g) * (1.0 + scale) + shift


def _dot(a, b):
    return jnp.dot(a.astype(BF16), b, preferred_element_type=F32)


def _adaln_kernel(c_ref, w_ref, b_ref, o_ref):
    s = jax.nn.silu(c_ref[...])
    bias = b_ref[pl.ds(pl.program_id(0), 1), :]
    o_ref[...] = _dot(s, w_ref[...].astype(BF16)) + bias


def _adaln(cond, mod_w, mod_b):
    depth, d, n = mod_w.shape
    tn = 1024
    return pl.pallas_call(
        _adaln_kernel,
        grid=(depth, n // tn),
        in_specs=[
            pl.BlockSpec((COND_ROWS, d), lambda l, j: (0, 0)),
            pl.BlockSpec((None, d, tn), lambda l, j: (l, 0, j)),
            pl.BlockSpec((depth, tn), lambda l, j: (0, j)),
        ],
        out_specs=pl.BlockSpec((None, COND_ROWS, tn), lambda l, j: (l, 0, j)),
        out_shape=jax.ShapeDtypeStruct((depth, COND_ROWS, n), F32),
        compiler_params=pltpu.CompilerParams(
            dimension_semantics=("arbitrary", "arbitrary"), vmem_limit_bytes=VMEM_LIMIT),
        name="adaln",
    )(cond, mod_w, mod_b)


class _Tiles:
    def __init__(self, dims, tm):
        (self.bp, self.sp), (self.bs, self.ss) = dims
        assert tm % self.sp == 0 and self.ss % tm == 0 and (self.bp * self.sp) % tm == 0
        assert tm % SUB == 0
        self.tm = tm
        self.npt = self.bp * self.sp // tm
        self.nst = self.bs * self.ss // tm
        self.n = self.npt + self.nst

    def subs(self):
        return [slice(s0, s0 + SUB) for s0 in range(0, self.tm, SUB)]

    def is_prompt(self):
        return pl.program_id(0) < self.npt

    def combined(self, cols):
        return pl.BlockSpec((self.tm, cols), lambda i: (i, 0))

    def _pi(self, i):
        return jnp.minimum(i, self.npt - 1)

    def _si(self, i):
        return jnp.maximum(i - self.npt, 0)

    def flat_pair(self, cols):
        return [pl.BlockSpec((self.tm, cols), lambda i: (self._pi(i), 0)),
                pl.BlockSpec((self.tm, cols), lambda i: (self._si(i), 0))]

    def padded_pair(self, cols):
        ts = self.ss // self.tm
        return [pl.BlockSpec((self.tm // self.sp, self.sp, cols), lambda i: (self._pi(i), 0, 0)),
                pl.BlockSpec((None, self.tm, cols),
                             lambda i: (self._si(i) // ts, self._si(i) % ts, 0))]

    def padded_out_pair(self, cols, pad):
        ts = self.ss // self.tm
        return [pl.BlockSpec((self.tm // self.sp, self.sp + pad, cols), lambda i: (self._pi(i), 0, 0)),
                pl.BlockSpec((None, self.ss + pad, cols), lambda i: (self._si(i) // ts, 0, 0))]

    def branch(self, prompt_fn, sample_fn):
        pl.when(self.is_prompt())(prompt_fn)
        pl.when(jnp.logical_not(self.is_prompt()))(sample_fn)

    def write_padded_prompt(self, p_ref, value):
        nb, rows, cols = p_ref.shape
        p_ref[:, :self.sp, :] = value.reshape(nb, self.sp, cols)
        p_ref[:, self.sp:, :] = jnp.zeros((nb, rows - self.sp, cols), value.dtype)

    def write_padded_sample(self, s_ref, value):
        rows, cols = s_ref.shape
        part = self._si(pl.program_id(0)) % (self.ss // self.tm)
        s_ref[pl.ds(pl.multiple_of(part * self.tm, self.tm), self.tm), :] = value
        s_ref[self.ss:, :] = jnp.zeros((rows - self.ss, cols), value.dtype)

    def mod(self, layer, mods):
        return pl.BlockSpec((None,) + mods.shape[1:], lambda i: (layer, 0, 0))

    def mods(self, mod_ref):
        i = pl.program_id(0)
        row = jnp.where(i < self.npt, 0, 1 + self._si(i) * self.tm // self.ss)
        m = mod_ref[pl.ds(row, 1), :]
        d = m.shape[1] // N_MOD
        return [m[:, k * d:(k + 1) * d] for k in range(N_MOD)]

    def read(self, p_ref, s_ref):
        return jnp.where(self.is_prompt(), p_ref[...].reshape(s_ref.shape), s_ref[...])


def _const_spec(shape):
    nd = len(shape)
    return pl.BlockSpec(shape, lambda i: (0,) * nd, pipeline_mode=pl.Buffered(1))


_TOKEN_PARAMS = pltpu.CompilerParams(
    dimension_semantics=("arbitrary",), vmem_limit_bytes=VMEM_LIMIT)


def _lru_in_kernel(tiles, xp_ref, xs_ref, mod_ref, g_ref, w_ref, xbp_ref, xbs_ref):
    def project(x_ref):
        m = tiles.mods(mod_ref)
        return jnp.concatenate(
            [_dot(_modulated_norm(x_ref[r, :], g_ref[...], m[0], m[1]), w_ref[...])
             for r in tiles.subs()], axis=0)

    tiles.branch(lambda: tiles.write_padded_prompt(xbp_ref, project(xp_ref)),
                 lambda: tiles.write_padded_sample(xbs_ref, project(xs_ref)))


def _lru_in(tiles, xp, xs, mods, g, w_x, layer, pad):
    d, w = w_x.shape
    return pl.pallas_call(
        functools.partial(_lru_in_kernel, tiles),
        grid=(tiles.n,),
        in_specs=tiles.flat_pair(d) + [tiles.mod(layer, mods), _const_spec((1, d)),
                                       _const_spec((d, w))],
        out_specs=tiles.padded_out_pair(w, pad),
        out_shape=[jax.ShapeDtypeStruct((tiles.bp, tiles.sp + pad, w), F32),
                   jax.ShapeDtypeStruct((tiles.bs, tiles.ss + pad, w), F32)],
        compiler_params=_TOKEN_PARAMS,
        name="lru_in",
    )(xp, xs, mods, g, w_x)


def _lru_rec_kernel(B, S, P, has_h0, n_cast, *refs):
    n_in = 6 + int(has_h0)
    x_ref, cw_ref, cb_ref, wg_ref, gb_ref, lam_ref = refs[:6]
    h0_ref = refs[6] if has_h0 else None
    cast_src = refs[n_in:n_in + n_cast]
    out_ref, st_ref = refs[n_in + n_cast:n_in + n_cast + 2]
    cast_dst = refs[n_in + n_cast + 2:n_in + 2 * n_cast + 2]
    xt, xhb, hfb, a_e, bx_e, a_o, bx_o, g_e, g_o = refs[n_in + 2 * n_cast + 2:]
    rows = B * S
    R = GATE_ROWS
    n = rows // R
    steps = R // B
    assert n % 2 == 0 and n >= 4 and R % B == 0

    for src, dst in zip(cast_src, cast_dst):
        dst[...] = src[...].astype(BF16)

    for b in range(B):
        out_ref[pl.ds(b * P + S, P - S), :] = jnp.zeros((P - S, LANES), F32)

    xt[pl.ds(0, B), :] = jnp.zeros((B, LANES), F32)
    xt[pl.ds(rows + B, 2 * B), :] = jnp.zeros((2 * B, LANES), F32)

    def gather(t, _):
        xt[pl.ds(pl.multiple_of((t + 1) * B, B), B), :] = x_ref[pl.ds(t, B, stride=P), :]
        return 0
    lax.fori_loop(0, S, gather, 0, unroll=4)

    cw = 0.5 * cw_ref[...]
    cb = 0.5 * cb_ref[...]
    wgz = [jnp.concatenate([wg_ref[2 * z], wg_ref[2 * z + 1]], axis=1) for z in range(2)]
    gbz = [jnp.concatenate([gb_ref[2 * z], gb_ref[2 * z + 1]], axis=1) for z in range(2)]
    neg = -lam_ref[...]
    sp = jnp.maximum(neg, 0.0) + jnp.log1p(jnp.exp(-jnp.abs(neg)))
    c_pos = (0.5 * LRU_C) * sp
    c_exp2 = -LOG2_E * c_pos

    def preact(z, c, g_dst):
        r0 = pl.multiple_of(c * R, R)
        if z == 0:
            xh = (cw[0:1] * xt[pl.ds(r0, R), :]
                  + cw[1:2] * xt[pl.ds(r0 + B, R), :]
                  + cw[2:3] * xt[pl.ds(r0 + 2 * B, R), :]
                  + cw[3:4] * xt[pl.ds(r0 + 3 * B, R), :]) + cb
            xhb[pl.ds(r0, R), :] = xh
        else:
            xh = xhb[pl.ds(r0, R), :]
        g_dst[...] = _dot(xh, wgz[z]) + gbz[z]

    def gates(z, c, g_src, a_dst, bx_dst):
        xh = xhb[pl.ds(pl.multiple_of(c * R, R), R), :]
        r2 = jnp.tanh(g_src[:, :LANES]) + 1.0
        i2 = jnp.tanh(g_src[:, LANES:]) + 1.0
        a = jnp.exp2(r2 * c_exp2[z:z + 1])
        om = jnp.tanh(r2 * c_pos[z:z + 1]) * (a * a + 1.0)
        mult = jnp.where(om > 0.0, om * lax.rsqrt(om), 0.0)
        a_dst[...] = a
        bx_dst[...] = mult * (i2 * xh)

    def scan_f(c, a_src, bx_src, h):
        for k in range(steps):
            h = a_src[k * B:(k + 1) * B, :] * h + bx_src[k * B:(k + 1) * B, :]
            hfb[pl.ds(pl.multiple_of(c * R + k * B, B), B), :] = h
        return h

    def scan_b(c, a_src, bx_src, h):
        for k in reversed(range(steps)):
            h = a_src[k * B:(k + 1) * B, :] * h + bx_src[k * B:(k + 1) * B, :]
            hf = hfb[pl.ds(pl.multiple_of(c * R + k * B, B), B), :]
            out_ref[pl.ds(c * steps + k, B, stride=P), :] = hf + h
        return h

    def run_pass(z, chunk, scan, h):
        even, odd = (g_e, a_e, bx_e), (g_o, a_o, bx_o)
        preact(z, chunk(0), g_e)
        gates(z, chunk(0), *even)
        preact(z, chunk(1), g_o)

        def body(k, h):
            i = 2 * k + 1
            preact(z, chunk(i + 1), g_e)
            gates(z, chunk(i), *odd)
            h = scan(chunk(i - 1), a_e, bx_e, h)
            preact(z, chunk(i + 2), g_o)
            gates(z, chunk(i + 1), *even)
            return scan(chunk(i), a_o, bx_o, h)
        h = lax.fori_loop(0, n // 2 - 1, body, h)
        gates(z, chunk(n - 1), *odd)
        h = scan(chunk(n - 2), a_e, bx_e, h)
        return scan(chunk(n - 1), a_o, bx_o, h)

    if has_h0:
        h0f, h0b = h0_ref[0], h0_ref[1]
    else:
        h0f = h0b = jnp.zeros((B, LANES), F32)
    st_ref[0] = run_pass(0, lambda i: i, scan_f, h0f)
    st_ref[1] = run_pass(1, lambda i: n - 1 - i, scan_b, h0b)


def _lru_rec(xb, conv_w, conv_b, wg, gb, lam, h0, casts):
    B, P, w = xb.shape
    S = P - SUBLANES
    rows = B * S
    nj = w // LANES
    has_h0 = h0 is not None
    slab = pl.BlockSpec((B * P, LANES), lambda j: (0, j))
    state = pl.BlockSpec((2, B, LANES), lambda j: (0, 0, j))
    in_specs = [
        slab,
        pl.BlockSpec((4, LANES), lambda j: (0, j)),
        pl.BlockSpec((1, LANES), lambda j: (0, j)),
        pl.BlockSpec((4, None, LANES, LANES), lambda j: (0, j, 0, 0)),
        pl.BlockSpec((4, None, 1, LANES), lambda j: (0, j, 0, 0)),
        pl.BlockSpec((2, LANES), lambda j: (0, j)),
    ]
    args = [xb.reshape(B * P, w), conv_w, conv_b, wg, gb, lam]
    if has_h0:
        in_specs.append(state)
        args.append(h0)
    out_specs = [slab, state]
    out_shape = [jax.ShapeDtypeStruct((B * P, w), F32), jax.ShapeDtypeStruct((2, B, w), F32)]
    for wt, layer in casts:
        _, r, c = wt.shape
        assert r % (nj * 2 * SUBLANES) == 0
        in_specs.append(pl.BlockSpec((None, r // nj, c), functools.partial(
            lambda j, layer: (layer, j, 0), layer=layer)))
        args.append(wt)
        out_specs.append(pl.BlockSpec((r // nj, c), lambda j: (j, 0)))
        out_shape.append(jax.ShapeDtypeStruct((r, c), BF16))
    res = pl.pallas_call(
        functools.partial(_lru_rec_kernel, B, S, P, has_h0, len(casts)),
        grid=(nj,),
        in_specs=in_specs,
        out_specs=out_specs,
        out_shape=out_shape,
        scratch_shapes=[pltpu.VMEM((rows + 3 * B, LANES), F32)]
        + [pltpu.VMEM((rows, LANES), F32)] * 2 + [pltpu.VMEM((GATE_ROWS, LANES), F32)] * 4
        + [pltpu.VMEM((GATE_ROWS, 2 * LANES), F32)] * 2,
        compiler_params=pltpu.CompilerParams(
            dimension_semantics=("arbitrary",), vmem_limit_bytes=VMEM_LIMIT),
        name="lru_rec",
    )(*args)
    return res[0].reshape(B, P, w), res[1], res[2:]


def _ffn(x1, m, g, wg_ref, wu_ref, wd_ref):
    h = [_modulated_norm(v, g, m[3], m[4]).astype(BF16) for v in x1]
    gu = [(jnp.dot(v, wg_ref[...], preferred_element_type=F32),
           jnp.dot(v, wu_ref[...], preferred_element_type=F32)) for v in h]
    dn = [_dot(jax.nn.silu(gate) * up, wd_ref[...]) for gate, up in gu]
    return [v + m[5] * d for v, d in zip(x1, dn)]


def _lru_out_kernel(tiles, xp_ref, xs_ref, hp_ref, hs_ref, mod_ref, gm_ref, gf_ref,
                    wy_ref, wo_ref, wg_ref, wu_ref, wd_ref, o_ref):
    subs = tiles.subs()

    def body(x_ref, hsum):
        m = tiles.mods(mod_ref)
        x = [x_ref[r, :] for r in subs]
        y = [jax.nn.gelu(_dot(_modulated_norm(v, gm_ref[...], m[0], m[1]), wy_ref[...]))
             for v in x]
        x1 = [v + m[2] * _dot(h * g, wo_ref[...]) for v, h, g in zip(x, hsum, y)]
        for r, v in zip(subs, _ffn(x1, m, gf_ref[...], wg_ref, wu_ref, wd_ref)):
            o_ref[r, :] = v

    def prompt_hsum():
        flat = hp_ref[...].reshape(tiles.tm, hp_ref.shape[-1])
        return [flat[r] for r in subs]

    tiles.branch(lambda: body(xp_ref, prompt_hsum()),
                 lambda: body(xs_ref, [hs_ref[r, :] for r in subs]))


def _lru_out(tiles, xp, xs, hp, hs, mods, g_mix, g_ffn, w_in, w_out, w_gate, w_up, w_down, layer):
    d = xp.shape[1]
    w = w_out.shape[0]
    f = w_gate.shape[1]
    w_y = pl.BlockSpec((d, w), lambda i: (0, 1), pipeline_mode=pl.Buffered(1))
    return pl.pallas_call(
        functools.partial(_lru_out_kernel, tiles),
        grid=(tiles.n,),
        in_specs=tiles.flat_pair(d) + tiles.padded_pair(w) + [
            tiles.mod(layer, mods), _const_spec((1, d)), _const_spec((1, d)), w_y,
            _const_spec((w, d)), _const_spec((d, f)), _const_spec((d, f)), _const_spec((f, d))],
        out_specs=tiles.combined(d),
        out_shape=jax.ShapeDtypeStruct((tiles.n * tiles.tm, d), F32),
        compiler_params=_TOKEN_PARAMS,
        name="lru_out_ffn",
    )(xp, xs, hp, hs, mods, g_mix, g_ffn, w_in, w_out, w_gate, w_up, w_down)


def _cm_kernel(tiles, x_ref, mod_ref, gm_ref, gf_ref, gfin_ref, wi_ref, bi_ref, lng_ref, lnb_ref,
               ws_ref, bs_ref, wo_ref, wg_ref, wu_ref, wd_ref, op_ref, os_ref):
    cw = lng_ref.shape[-1]
    n_groups = ws_ref.shape[0]
    gd = cw // n_groups

    def sub_tile(k, carry):
        rows = pl.ds(pl.multiple_of(k * SUB, SUB), SUB)
        m = tiles.mods(mod_ref)
        x = x_ref[rows, :]
        z = _dot(_modulated_norm(x, gm_ref[...], m[0], m[1]), wi_ref[...]) + bi_ref[...]
        uv = jax.nn.gelu(z)
        u = uv[:, :cw]
        v = uv[:, cw:]
        mu = jnp.mean(v, axis=-1, keepdims=True)
        var = jnp.mean(jnp.square(v - mu), axis=-1, keepdims=True)
        v = ((v - mu) * lax.rsqrt(var + EPS) * lng_ref[...] + lnb_ref[...]).astype(BF16)
        mixed_rows = []
        for c in range(SUB // CHUNK):
            vc = v[c * CHUNK:(c + 1) * CHUNK]
            cols = [jnp.dot(ws_ref[g], vc[:, g * gd:(g + 1) * gd],
                            preferred_element_type=F32) for g in range(n_groups)]
            mixed_rows.append(jnp.concatenate(cols, axis=1) + bs_ref[...])
        mixed = jnp.concatenate(mixed_rows, axis=0)
        x1 = x + m[2] * _dot(u * mixed, wo_ref[...])
        (x2,) = _ffn([x1], m, gf_ref[...], wg_ref, wu_ref, wd_ref)
        out = _rmsnorm(x2, gfin_ref[...])

        def put(o_ref):
            o_ref[rows, :] = out
        tiles.branch(lambda: put(op_ref), lambda: put(os_ref))
        return carry

    lax.fori_loop(0, tiles.tm // SUB, sub_tile, 0)


def _cm_layer(tiles, x, mods, g_mix, g_ffn, g_final, w_in, b_in, ln_g, ln_b, w_s, b_s, w_out,
              w_gate, w_up, w_down, layer):
    d = x.shape[1]
    cw = w_out.shape[0]
    f = w_gate.shape[1]
    return pl.pallas_call(
        functools.partial(_cm_kernel, tiles),
        grid=(tiles.n,),
        in_specs=[tiles.combined(d), tiles.mod(layer, mods),
                  _const_spec((1, d)), _const_spec((1, d)), _const_spec((1, d)),
                  _const_spec((d, 2 * cw)), _const_spec((1, 2 * cw)),
                  _const_spec((1, cw)), _const_spec((1, cw)),
                  _const_spec(w_s.shape), _const_spec(b_s.shape), _const_spec((cw, d)),
                  _const_spec((d, f)), _const_spec((d, f)), _const_spec((f, d))],
        out_specs=tiles.flat_pair(d),
        out_shape=[jax.ShapeDtypeStruct((tiles.npt * tiles.tm, d), F32),
                   jax.ShapeDtypeStruct((tiles.nst * tiles.tm, d), F32)],
        compiler_params=_TOKEN_PARAMS,
        name="cm_layer",
    )(x, mods, g_mix, g_ffn, g_final, w_in, b_in, ln_g, ln_b, w_s, b_s, w_out,
      w_gate, w_up, w_down)


def _gate_weights(ga_w, gx_w, ga_b, gx_b):
    _, heads, hd, _ = ga_w.shape
    per = LANES // hd
    nj = heads // per
    w4 = jnp.stack([ga_w[0], gx_w[0], ga_w[1], gx_w[1]]).reshape(4, nj, per, hd, hd)
    rows = []
    for h in range(per):
        blocks = [w4[:, :, h] if g == h else jnp.zeros_like(w4[:, :, h]) for g in range(per)]
        rows.append(jnp.concatenate(blocks, axis=-1))
    wg = jnp.concatenate(rows, axis=-2).astype(BF16)
    gb = 0.5 * jnp.stack([ga_b[0], gx_b[0], ga_b[1], gx_b[1]]).reshape(4, nj, 1, LANES)
    return wg, gb


def kernel(x_prompt, x_sample, state_lru, c, c_ctx, mod_w, mod_b, norm_mix, norm_ffn, lru_w_in, lru_conv_w, lru_conv_b, lru_ga_w, lru_ga_b, lru_gx_w, lru_gx_b, lru_lambda, lru_w_out, cm_w_in, cm_b_in, cm_ln_g, cm_ln_b, cm_w_s, cm_b_s, cm_w_out, ffn_w_gate, ffn_w_up, ffn_w_down, final_norm):
    bp, sp, d = x_prompt.shape
    bs, ss, _ = x_sample.shape
    dims = ((bp, sp), (bs, ss))
    light, heavy = _Tiles(dims, TM_LIGHT), _Tiles(dims, TM_HEAVY)
    xp = x_prompt.reshape(bp * sp, d)
    xs = x_sample.reshape(bs * ss, d)

    cond = jnp.concatenate([c_ctx[None], c, jnp.zeros((COND_ROWS - 1 - bs, d), F32)], 0)
    mods = _adaln(cond, mod_w, mod_b)

    w = lru_w_out.shape[1]
    xbp, xbs = _lru_in(light, xp, xs, mods, norm_mix[0:1], lru_w_in[0, :, :w].astype(BF16), 0,
                       SUBLANES)
    wg, gb = _gate_weights(lru_ga_w[0], lru_gx_w[0], lru_ga_b[0], lru_gx_b[0])
    rec = (lru_conv_w[0], lru_conv_b[0:1], wg, gb, lru_lambda[0])
    hp, st, (wi0, wo0, wg0, wu0, wd0) = _lru_rec(
        xbp, *rec, None,
        [(lru_w_in, 0), (lru_w_out, 0), (ffn_w_gate, 0), (ffn_w_up, 0), (ffn_w_down, 0)])
    hs, _, (wi1, wo1, wg1, wu1, wd1) = _lru_rec(
        xbs, *rec, jnp.transpose(state_lru[:, 0], (1, 0, 2)),
        [(cm_w_in, 0), (cm_w_out, 0), (ffn_w_gate, 1), (ffn_w_up, 1), (ffn_w_down, 1)])
    x = _lru_out(heavy, xp, xs, hp, hs, mods, norm_mix[0:1], norm_ffn[0:1], wi0, wo0, wg0, wu0,
                 wd0, 0)

    n_groups = cm_w_s.shape[1]
    gd = cm_w_out.shape[1] // n_groups
    bs_tile = jnp.repeat(cm_b_s[0].T, gd, axis=1)
    yp, ys = _cm_layer(heavy, x, mods, norm_mix[1:2], norm_ffn[1:2], final_norm[None],
                       wi1, cm_b_in[0:1], cm_ln_g[0:1], cm_ln_b[0:1],
                       cm_w_s[0].astype(BF16), bs_tile, wo1, wg1, wu1, wd1, 1)

    new_state = jnp.transpose(st, (1, 0, 2))[:, None]
    return (yp.reshape(bp, sp, d), ys.reshape(bs, ss, d), new_state)
```

```python
import functools

import jax
import jax.numpy as jnp
from jax import lax
from jax.experimental import pallas as pl
from jax.experimental.pallas import tpu as pltpu

F32 = jnp.float32
BF16 = jnp.bfloat16

EPS = 1e-6
LRU_C = 8.0
LOG2_E = 1.4426950408889634
N_MOD = 6
COND_ROWS = 16
LANES = 128
SUBLANES = 8
CHUNK = 128
TM_LIGHT = 1024
TM_HEAVY = 512
SUB = 256
GATE_ROWS = 512
VMEM_LIMIT = 56 * 1024 * 1024


def _rmsnorm(x, g):
    y = x * lax.rsqrt(jnp.mean(x * x, axis=-1, keepdims=True) + EPS)
    return y * g


def _modulated_norm(x, g, shift, scale):
    return _rmsnorm(x, g) * (1.0 + scale) + shift


def _dot(a, b):
    return jnp.dot(a.astype(BF16), b, preferred_element_type=F32)


def _adaln_kernel(c_ref, w_ref, b_ref, o_ref):
    s = jax.nn.silu(c_ref[...])
    bias = b_ref[pl.ds(pl.program_id(0), 1), :]
    o_ref[...] = _dot(s, w_ref[...].astype(BF16)) + bias


def _adaln(cond, mod_w, mod_b):
    depth, d, n = mod_w.shape
    tn = 1024
    return pl.pallas_call(
        _adaln_kernel,
        grid=(depth, n // tn),
        in_specs=[
            pl.BlockSpec((COND_ROWS, d), lambda l, j: (0, 0)),
            pl.BlockSpec((None, d, tn), lambda l, j: (l, 0, j)),
            pl.BlockSpec((depth, tn), lambda l, j: (0, j)),
        ],
        out_specs=pl.BlockSpec((None, COND_ROWS, tn), lambda l, j: (l, 0, j)),
        out_shape=jax.ShapeDtypeStruct((depth, COND_ROWS, n), F32),
        compiler_params=pltpu.CompilerParams(
            dimension_semantics=("arbitrary", "arbitrary"), vmem_limit_bytes=VMEM_LIMIT),
        name="adaln",
    )(cond, mod_w, mod_b)


class _Tiles:
    def __init__(self, dims, tm):
        (self.bp, self.sp), (self.bs, self.ss) = dims
        assert tm % self.sp == 0 and self.ss % tm == 0 and (self.bp * self.sp) % tm == 0
        assert tm % SUB == 0
        self.tm = tm
        self.npt = self.bp * self.sp // tm
        self.nst = self.bs * self.ss // tm
        self.n = self.npt + self.nst

    def subs(self):
        return [slice(s0, s0 + SUB) for s0 in range(0, self.tm, SUB)]

    def is_prompt(self):
        return pl.program_id(0) < self.npt

    def combined(self, cols):
        return pl.BlockSpec((self.tm, cols), lambda i: (i, 0))

    def _pi(self, i):
        return jnp.minimum(i, self.npt - 1)

    def _si(self, i):
        return jnp.maximum(i - self.npt, 0)

    def flat_pair(self, cols):
        return [pl.BlockSpec((self.tm, cols), lambda i: (self._pi(i), 0)),
                pl.BlockSpec((self.tm, cols), lambda i: (self._si(i), 0))]

    def slab_pair(self, nj):
        ts = self.ss // self.tm
        return [pl.BlockSpec((nj, self.tm // self.sp, self.sp, LANES),
                             lambda i: (0, self._pi(i), 0, 0)),
                pl.BlockSpec((nj, None, self.tm, LANES),
                             lambda i: (0, self._si(i) // ts, self._si(i) % ts, 0))]

    def slab_out_pair(self, nj, pad):
        ts = self.ss // self.tm
        return [pl.BlockSpec((nj, self.tm // self.sp, self.sp + pad, LANES),
                             lambda i: (0, self._pi(i), 0, 0)),
                pl.BlockSpec((nj, None, self.ss + pad, LANES),
                             lambda i: (0, self._si(i) // ts, 0, 0))]

    def branch(self, prompt_fn, sample_fn):
        pl.when(self.is_prompt())(prompt_fn)
        pl.when(jnp.logical_not(self.is_prompt()))(sample_fn)

    def write_slabs_prompt(self, p_ref, value):
        nj, nb, rows, _ = p_ref.shape
        for j in range(nj):
            p_ref[j, :, :self.sp, :] = value[:, j * LANES:(j + 1) * LANES].reshape(nb, self.sp, LANES)
        p_ref[:, :, self.sp:, :] = jnp.zeros((nj, nb, rows - self.sp, LANES), value.dtype)

    def write_slabs_sample(self, s_ref, value):
        nj, rows, _ = s_ref.shape
        part = self._si(pl.program_id(0)) % (self.ss // self.tm)
        dst = pl.ds(pl.multiple_of(part * self.tm, self.tm), self.tm)
        for j in range(nj):
            s_ref[j, dst, :] = value[:, j * LANES:(j + 1) * LANES]
        s_ref[:, self.ss:, :] = jnp.zeros((nj, rows - self.ss, LANES), value.dtype)

    def mod(self, layer, mods):
        return pl.BlockSpec((None,) + mods.shape[1:], lambda i: (layer, 0, 0))

    def mods(self, mod_ref):
        i = pl.program_id(0)
        row = jnp.where(i < self.npt, 0, 1 + self._si(i) * self.tm // self.ss)
        m = mod_ref[pl.ds(row, 1), :]
        d = m.shape[1] // N_MOD
        return [m[:, k * d:(k + 1) * d] for k in range(N_MOD)]

    def read(self, p_ref, s_ref):
        return jnp.where(self.is_prompt(), p_ref[...].reshape(s_ref.shape), s_ref[...])


def _const_spec(shape):
    nd = len(shape)
    return pl.BlockSpec(shape, lambda i: (0,) * nd, pipeline_mode=pl.Buffered(1))


_TOKEN_PARAMS = pltpu.CompilerParams(
    dimension_semantics=("arbitrary",), vmem_limit_bytes=VMEM_LIMIT)


def _lru_in_kernel(tiles, xp_ref, xs_ref, mod_ref, g_ref, w_ref, xbp_ref, xbs_ref):
    def project(x_ref):
        m = tiles.mods(mod_ref)
        return jnp.concatenate(
            [_dot(_modulated_norm(x_ref[r, :], g_ref[...], m[0], m[1]), w_ref[...])
             for r in tiles.subs()], axis=0)

    tiles.branch(lambda: tiles.write_slabs_prompt(xbp_ref, project(xp_ref)),
                 lambda: tiles.write_slabs_sample(xbs_ref, project(xs_ref)))


def _lru_in(tiles, xp, xs, mods, g, w_x, layer, pad):
    d, w = w_x.shape
    return pl.pallas_call(
        functools.partial(_lru_in_kernel, tiles),
        grid=(tiles.n,),
        in_specs=tiles.flat_pair(d) + [tiles.mod(layer, mods), _const_spec((1, d)),
                                       _const_spec((d, w))],
        out_specs=tiles.slab_out_pair(w // LANES, pad),
        out_shape=[jax.ShapeDtypeStruct((w // LANES, tiles.bp, tiles.sp + pad, LANES), F32),
                   jax.ShapeDtypeStruct((w // LANES, tiles.bs, tiles.ss + pad, LANES), F32)],
        compiler_params=_TOKEN_PARAMS,
        name="lru_in",
    )(xp, xs, mods, g, w_x)


def _lru_rec_kernel(B, S, P, has_h0, n_cast, *refs):
    n_in = 6 + int(has_h0)
    x_ref, cw_ref, cb_ref, wg_ref, gb_ref, lam_ref = refs[:6]
    h0_ref = refs[6] if has_h0 else None
    cast_src = refs[n_in:n_in + n_cast]
    out_ref, st_ref = refs[n_in + n_cast:n_in + n_cast + 2]
    cast_dst = refs[n_in + n_cast + 2:n_in + 2 * n_cast + 2]
    xt, xhb, hfb, a_e, bx_e, a_o, bx_o, g_e, g_o = refs[n_in + 2 * n_cast + 2:]
    rows = B * S
    R = GATE_ROWS
    n = rows // R
    steps = R // B
    assert n % 2 == 0 and n >= 4 and R % B == 0

    for src, dst in zip(cast_src, cast_dst):
        dst[...] = src[...].astype(BF16)

    for b in range(B):
        out_ref[pl.ds(b * P + S, P - S), :] = jnp.zeros((P - S, LANES), F32)

    xt[pl.ds(0, B), :] = jnp.zeros((B, LANES), F32)
    xt[pl.ds(rows + B, 2 * B), :] = jnp.zeros((2 * B, LANES), F32)

    def gather(t, _):
        xt[pl.ds(pl.multiple_of((t + 1) * B, B), B), :] = x_ref[pl.ds(t, B, stride=P), :]
        return 0
    lax.fori_loop(0, S, gather, 0, unroll=4)

    cw = 0.5 * cw_ref[...]
    cb = 0.5 * cb_ref[...]
    wgz = [jnp.concatenate([wg_ref[2 * z], wg_ref[2 * z + 1]], axis=1) for z in range(2)]
    gbz = [jnp.concatenate([gb_ref[2 * z], gb_ref[2 * z + 1]], axis=1) for z in range(2)]
    neg = -lam_ref[...]
    sp = jnp.maximum(neg, 0.0) + jnp.log1p(jnp.exp(-jnp.abs(neg)))
    c_pos = (0.5 * LRU_C) * sp
    c_exp2 = -LOG2_E * c_pos

    def preact(z, c, g_dst):
        r0 = pl.multiple_of(c * R, R)
        if z == 0:
            xh = (cw[0:1] * xt[pl.ds(r0, R), :]
                  + cw[1:2] * xt[pl.ds(r0 + B, R), :]
                  + cw[2:3] * xt[pl.ds(r0 + 2 * B, R), :]
                  + cw[3:4] * xt[pl.ds(r0 + 3 * B, R), :]) + cb
            xhb[pl.ds(r0, R), :] = xh
        else:
            xh = xhb[pl.ds(r0, R), :]
        g_dst[...] = _dot(xh, wgz[z]) + gbz[z]

    def gates(z, c, g_src, a_dst, bx_dst):
        xh = xhb[pl.ds(pl.multiple_of(c * R, R), R), :]
        r2 = jnp.tanh(g_src[:, :LANES]) + 1.0
        i2 = jnp.tanh(g_src[:, LANES:]) + 1.0
        a = jnp.exp2(r2 * c_exp2[z:z + 1])
        om = jnp.tanh(r2 * c_pos[z:z + 1]) * (a * a + 1.0)
        mult = jnp.where(om > 0.0, om * lax.rsqrt(om), 0.0)
        a_dst[...] = a
        bx_dst[...] = mult * (i2 * xh)

    def scan_f(c, a_src, bx_src, h):
        for k in range(steps):
            h = a_src[k * B:(k + 1) * B, :] * h + bx_src[k * B:(k + 1) * B, :]
            hfb[pl.ds(pl.multiple_of(c * R + k * B, B), B), :] = h
        return h

    def scan_b(c, a_src, bx_src, h):
        for k in reversed(range(steps)):
            h = a_src[k * B:(k + 1) * B, :] * h + bx_src[k * B:(k + 1) * B, :]
            hf = hfb[pl.ds(pl.multiple_of(c * R + k * B, B), B), :]
            out_ref[pl.ds(c * steps + k, B, stride=P), :] = hf + h
        return h

    def run_pass(z, chunk, scan, h):
        even, odd = (g_e, a_e, bx_e), (g_o, a_o, bx_o)
        preact(z, chunk(0), g_e)
        gates(z, chunk(0), *even)
        preact(z, chunk(1), g_o)

        def body(k, h):
            i = 2 * k + 1
            preact(z, chunk(i + 1), g_e)
            gates(z, chunk(i), *odd)
            h = scan(chunk(i - 1), a_e, bx_e, h)
            preact(z, chunk(i + 2), g_o)
            gates(z, chunk(i + 1), *even)
            return scan(chunk(i), a_o, bx_o, h)
        h = lax.fori_loop(0, n // 2 - 1, body, h)
        gates(z, chunk(n - 1), *odd)
        h = scan(chunk(n - 2), a_e, bx_e, h)
        return scan(chunk(n - 1), a_o, bx_o, h)

    if has_h0:
        h0f, h0b = h0_ref[0], h0_ref[1]
    else:
        h0f = h0b = jnp.zeros((B, LANES), F32)
    st_ref[0] = run_pass(0, lambda i: i, scan_f, h0f)
    st_ref[1] = run_pass(1, lambda i: n - 1 - i, scan_b, h0b)


def _lru_rec(xb, conv_w, conv_b, wg, gb, lam, h0, casts):
    nj, B, P, _ = xb.shape
    S = P - SUBLANES
    rows = B * S
    w = nj * LANES
    has_h0 = h0 is not None
    slab = pl.BlockSpec((None, B * P, LANES), lambda j: (j, 0, 0))
    state = pl.BlockSpec((2, B, LANES), lambda j: (0, 0, j))
    in_specs = [
        slab,
        pl.BlockSpec((4, LANES), lambda j: (0, j)),
        pl.BlockSpec((1, LANES), lambda j: (0, j)),
        pl.BlockSpec((4, None, LANES, LANES), lambda j: (0, j, 0, 0)),
        pl.BlockSpec((4, None, 1, LANES), lambda j: (0, j, 0, 0)),
        pl.BlockSpec((2, LANES), lambda j: (0, j)),
    ]
    args = [xb.reshape(nj, B * P, LANES), conv_w, conv_b, wg, gb, lam]
    if has_h0:
        in_specs.append(state)
        args.append(h0)
    out_specs = [slab, state]
    out_shape = [jax.ShapeDtypeStruct((nj, B * P, LANES), F32),
                 jax.ShapeDtypeStruct((2, B, w), F32)]
    for wt, layer in casts:
        _, r, c = wt.shape
        assert r % (nj * 2 * SUBLANES) == 0
        in_specs.append(pl.BlockSpec((None, r // nj, c), functools.partial(
            lambda j, layer: (layer, j, 0), layer=layer)))
        args.append(wt)
        out_specs.append(pl.BlockSpec((r // nj, c), lambda j: (j, 0)))
        out_shape.append(jax.ShapeDtypeStruct((r, c), BF16))
    res = pl.pallas_call(
        functools.partial(_lru_rec_kernel, B, S, P, has_h0, len(casts)),
        grid=(nj,),
        in_specs=in_specs,
        out_specs=out_specs,
        out_shape=out_shape,
        scratch_shapes=[pltpu.VMEM((rows + 3 * B, LANES), F32)]
        + [pltpu.VMEM((rows, LANES), F32)] * 2 + [pltpu.VMEM((GATE_ROWS, LANES), F32)] * 4
        + [pltpu.VMEM((GATE_ROWS, 2 * LANES), F32)] * 2,
        compiler_params=pltpu.CompilerParams(
            dimension_semantics=("arbitrary",), vmem_limit_bytes=VMEM_LIMIT),
        name="lru_rec",
    )(*args)
    return res[0].reshape(nj, B, P, LANES), res[1], res[2:]


def _ffn(x1, m, g, wg_ref, wu_ref, wd_ref):
    h = [_modulated_norm(v, g, m[3], m[4]).astype(BF16) for v in x1]
    gu = [(jnp.dot(v, wg_ref[...], preferred_element_type=F32),
           jnp.dot(v, wu_ref[...], preferred_element_type=F32)) for v in h]
    dn = [_dot(jax.nn.silu(gate) * up, wd_ref[...]) for gate, up in gu]
    return [v + m[5] * d for v, d in zip(x1, dn)]


def _lru_out_kernel(tiles, xp_ref, xs_ref, hp_ref, hs_ref, mod_ref, gm_ref, gf_ref,
                    wy_ref, wo_ref, wg_ref, wu_ref, wd_ref, o_ref):
    subs = tiles.subs()

    def body(x_ref, hsum):
        m = tiles.mods(mod_ref)
        x = [x_ref[r, :] for r in subs]
        y = [jax.nn.gelu(_dot(_modulated_norm(v, gm_ref[...], m[0], m[1]), wy_ref[...]))
             for v in x]
        x1 = [v + m[2] * _dot(h * g, wo_ref[...]) for v, h, g in zip(x, hsum, y)]
        for r, v in zip(subs, _ffn(x1, m, gf_ref[...], wg_ref, wu_ref, wd_ref)):
            o_ref[r, :] = v

    nj = hp_ref.shape[0]

    def prompt_hsum(r):
        return jnp.concatenate(
            [hp_ref[j].reshape(tiles.tm, LANES)[r] for j in range(nj)], axis=1)

    def sample_hsum(r):
        return jnp.concatenate([hs_ref[j, r, :] for j in range(nj)], axis=1)

    tiles.branch(lambda: body(xp_ref, [prompt_hsum(r) for r in subs]),
                 lambda: body(xs_ref, [sample_hsum(r) for r in subs]))


def _lru_out(tiles, xp, xs, hp, hs, mods, g_mix, g_ffn, w_in, w_out, w_gate, w_up, w_down, layer):
    d = xp.shape[1]
    w = w_out.shape[0]
    f = w_gate.shape[1]
    w_y = pl.BlockSpec((d, w), lambda i: (0, 1), pipeline_mode=pl.Buffered(1))
    return pl.pallas_call(
        functools.partial(_lru_out_kernel, tiles),
        grid=(tiles.n,),
        in_specs=tiles.flat_pair(d) + tiles.slab_pair(w // LANES) + [
            tiles.mod(layer, mods), _const_spec((1, d)), _const_spec((1, d)), w_y,
            _const_spec((w, d)), _const_spec((d, f)), _const_spec((d, f)), _const_spec((f, d))],
        out_specs=tiles.combined(d),
        out_shape=jax.ShapeDtypeStruct((tiles.n * tiles.tm, d), F32),
        compiler_params=_TOKEN_PARAMS,
        name="lru_out_ffn",
    )(xp, xs, hp, hs, mods, g_mix, g_ffn, w_in, w_out, w_gate, w_up, w_down)


def _cm_kernel(tiles, x_ref, mod_ref, gm_ref, gf_ref, gfin_ref, wi_ref, bi_ref, lng_ref, lnb_ref,
               ws_ref, bs_ref, wo_ref, wg_ref, wu_ref, wd_ref, op_ref, os_ref):
    cw = lng_ref.shape[-1]
    n_groups = ws_ref.shape[0]
    gd = cw // n_groups

    def body():
        m = tiles.mods(mod_ref)
        xs = [x_ref[r, :] for r in tiles.subs()]
        pre = [_dot(_modulated_norm(x, gm_ref[...], m[0], m[1]), wi_ref[...]) + bi_ref[...]
               for x in xs]
        x1 = []
        for x, z in zip(xs, pre):
            uv = jax.nn.gelu(z)
            u = uv[:, :cw]
            v = uv[:, cw:]
            mu = jnp.mean(v, axis=-1, keepdims=True)
            var = jnp.mean(jnp.square(v - mu), axis=-1, keepdims=True)
            v = ((v - mu) * lax.rsqrt(var + EPS) * lng_ref[...] + lnb_ref[...]).astype(BF16)
            mixed_rows = []
            for c in range(SUB // CHUNK):
                vc = v[c * CHUNK:(c + 1) * CHUNK]
                cols = [jnp.dot(ws_ref[g], vc[:, g * gd:(g + 1) * gd],
                                preferred_element_type=F32) for g in range(n_groups)]
                mixed_rows.append(jnp.concatenate(cols, axis=1) + bs_ref[...])
            mixed = jnp.concatenate(mixed_rows, axis=0)
            x1.append(x + m[2] * _dot(u * mixed, wo_ref[...]))
        x2 = _ffn(x1, m, gf_ref[...], wg_ref, wu_ref, wd_ref)
        return jnp.concatenate([_rmsnorm(v, gfin_ref[...]) for v in x2], axis=0)

    out = body()

    def put(o_ref):
        o_ref[...] = out
    tiles.branch(lambda: put(op_ref), lambda: put(os_ref))


def _cm_layer(tiles, x, mods, g_mix, g_ffn, g_final, w_in, b_in, ln_g, ln_b, w_s, b_s, w_out,
              w_gate, w_up, w_down, layer):
    d = x.shape[1]
    cw = w_out.shape[0]
    f = w_gate.shape[1]
    return pl.pallas_call(
        functools.partial(_cm_kernel, tiles),
        grid=(tiles.n,),
        in_specs=[tiles.combined(d), tiles.mod(layer, mods),
                  _const_spec((1, d)), _const_spec((1, d)), _const_spec((1, d)),
                  _const_spec((d, 2 * cw)), _const_spec((1, 2 * cw)),
                  _const_spec((1, cw)), _const_spec((1, cw)),
                  _const_spec(w_s.shape), _const_spec(b_s.shape), _const_spec((cw, d)),
                  _const_spec((d, f)), _const_spec((d, f)), _const_spec((f, d))],
        out_specs=tiles.flat_pair(d),
        out_shape=[jax.ShapeDtypeStruct((tiles.npt * tiles.tm, d), F32),
                   jax.ShapeDtypeStruct((tiles.nst * tiles.tm, d), F32)],
        compiler_params=_TOKEN_PARAMS,
        name="cm_layer",
    )(x, mods, g_mix, g_ffn, g_final, w_in, b_in, ln_g, ln_b, w_s, b_s, w_out,
      w_gate, w_up, w_down)


def _gate_weights(ga_w, gx_w, ga_b, gx_b):
    _, heads, hd, _ = ga_w.shape
    per = LANES // hd
    nj = heads // per
    w4 = jnp.stack([ga_w[0], gx_w[0], ga_w[1], gx_w[1]]).reshape(4, nj, per, hd, hd)
    rows = []
    for h in range(per):
        blocks = [w4[:, :, h] if g == h else jnp.zeros_like(w4[:, :, h]) for g in range(per)]
        rows.append(jnp.concatenate(blocks, axis=-1))
    wg = jnp.concatenate(rows, axis=-2).astype(BF16)
    gb = 0.5 * jnp.stack([ga_b[0], gx_b[0], ga_b[1], gx_b[1]]).reshape(4, nj, 1, LANES)
    return wg, gb


def kernel(x_prompt, x_sample, state_lru, c, c_ctx, mod_w, mod_b, norm_mix, norm_ffn, lru_w_in, lru_conv_w, lru_conv_b, lru_ga_w, lru_ga_b, lru_gx_w, lru_gx_b, lru_lambda, lru_w_out, cm_w_in, cm_b_in, cm_ln_g, cm_ln_b, cm_w_s, cm_b_s, cm_w_out, ffn_w_gate, ffn_w_up, ffn_w_down, final_norm):
    bp, sp, d = x_prompt.shape
    bs, ss, _ = x_sample.shape
    dims = ((bp, sp), (bs, ss))
    light, heavy = _Tiles(dims, TM_LIGHT), _Tiles(dims, TM_HEAVY)
    xp = x_prompt.reshape(bp * sp, d)
    xs = x_sample.reshape(bs * ss, d)

    cond = jnp.concatenate([c_ctx[None], c, jnp.zeros((COND_ROWS - 1 - bs, d), F32)], 0)
    mods = _adaln(cond, mod_w, mod_b)

    w = lru_w_out.shape[1]
    xbp, xbs = _lru_in(light, xp, xs, mods, norm_mix[0:1], lru_w_in[0, :, :w].astype(BF16), 0,
                       SUBLANES)
    wg, gb = _gate_weights(lru_ga_w[0], lru_gx_w[0], lru_ga_b[0], lru_gx_b[0])
    rec = (lru_conv_w[0], lru_conv_b[0:1], wg, gb, lru_lambda[0])
    hp, st, (wi0, wo0, wg0, wu0, wd0) = _lru_rec(
        xbp, *rec, None,
        [(lru_w_in, 0), (lru_w_out, 0), (ffn_w_gate, 0), (ffn_w_up, 0), (ffn_w_down, 0)])
    hs, _, (wi1, wo1, wg1, wu1, wd1) = _lru_rec(
        xbs, *rec, jnp.transpose(state_lru[:, 0], (1, 0, 2)),
        [(cm_w_in, 0), (cm_w_out, 0), (ffn_w_gate, 1), (ffn_w_up, 1), (ffn_w_down, 1)])
    x = _lru_out(heavy, xp, xs, hp, hs, mods, norm_mix[0:1], norm_ffn[0:1], wi0, wo0, wg0, wu0,
                 wd0, 0)

    n_groups = cm_w_s.shape[1]
    gd = cm_w_out.shape[1] // n_groups
    bs_tile = jnp.repeat(cm_b_s[0].T, gd, axis=1)
    yp, ys = _cm_layer(heavy, x, mods, norm_mix[1:2], norm_ffn[1:2], final_norm[None],
                       wi1, cm_b_in[0:1], cm_ln_g[0:1], cm_ln_b[0:1],
                       cm_w_s[0].astype(BF16), bs_tile, wo1, wg1, wu1, wd1, 1)

    new_state = jnp.transpose(st, (1, 0, 2))[:, None]
    return (yp.reshape(bp, sp, d), ys.reshape(bs, ss, d), new_state)
```

```python
import functools

import jax
import jax.numpy as jnp
from jax import lax
from jax.experimental import pallas as pl
from jax.experimental.pallas import tpu as pltpu

F32 = jnp.float32
BF16 = jnp.bfloat16

EPS = 1e-6
LRU_C = 8.0
LOG2_E = 1.4426950408889634
N_MOD = 6
COND_ROWS = 16
LANES = 128
SUBLANES = 8
CHUNK = 128
TM_LIGHT = 1024
TM_HEAVY = 512
SUB = 256
GATE_ROWS = 512
VMEM_LIMIT = 56 * 1024 * 1024


def _rmsnorm(x, g):
    y = x * lax.rsqrt(jnp.mean(x * x, axis=-1, keepdims=True) + EPS)
    return y * g


def _modulated_norm(x, g, shift, scale):
    return _rmsnorm(x, g) * (1.0 + scale) + shift


def _dot(a, b):
    return jnp.dot(a.astype(BF16), b, preferred_element_type=F32)


def _adaln_kernel(c_ref, w_ref, b_ref, o_ref):
    s = jax.nn.silu(c_ref[...])
    bias = b_ref[pl.ds(pl.program_id(0), 1), :]
    o_ref[...] = _dot(s, w_ref[...].astype(BF16)) + bias


def _adaln(cond, mod_w, mod_b):
    depth, d, n = mod_w.shape
    tn = 1024
    return pl.pallas_call(
        _adaln_kernel,
        grid=(depth, n // tn),
        in_specs=[
            pl.BlockSpec((COND_ROWS, d), lambda l, j: (0, 0)),
            pl.BlockSpec((None, d, tn), lambda l, j: (l, 0, j)),
            pl.BlockSpec((depth, tn), lambda l, j: (0, j)),
        ],
        out_specs=pl.BlockSpec((None, COND_ROWS, tn), lambda l, j: (l, 0, j)),
        out_shape=jax.ShapeDtypeStruct((depth, COND_ROWS, n), F32),
        compiler_params=pltpu.CompilerParams(
            dimension_semantics=("arbitrary", "arbitrary"), vmem_limit_bytes=VMEM_LIMIT),
        name="adaln",
    )(cond, mod_w, mod_b)


class _Tiles:
    def __init__(self, dims, tm):
        (self.bp, self.sp), (self.bs, self.ss) = dims
        assert tm % self.sp == 0 and self.ss % tm == 0 and (self.bp * self.sp) % tm == 0
        assert tm % SUB == 0
        self.tm = tm
        self.npt = self.bp * self.sp // tm
        self.nst = self.bs * self.ss // tm
        self.n = self.npt + self.nst

    def subs(self):
        return [slice(s0, s0 + SUB) for s0 in range(0, self.tm, SUB)]

    def is_prompt(self):
        return pl.program_id(0) < self.npt

    def combined(self, cols):
        return pl.BlockSpec((self.tm, cols), lambda i: (i, 0))

    def _pi(self, i):
        return jnp.minimum(i, self.npt - 1)

    def _si(self, i):
        return jnp.maximum(i - self.npt, 0)

    def flat_pair(self, cols):
        return [pl.BlockSpec((self.tm, cols), lambda i: (self._pi(i), 0)),
                pl.BlockSpec((self.tm, cols), lambda i: (self._si(i), 0))]

    def slab_pair(self, nj):
        ts = self.ss // self.tm
        return [pl.BlockSpec((nj, self.tm // self.sp, self.sp, LANES),
                             lambda i: (0, self._pi(i), 0, 0)),
                pl.BlockSpec((nj, None, self.tm, LANES),
                             lambda i: (0, self._si(i) // ts, self._si(i) % ts, 0))]

    def slab_out_pair(self, nj, pad):
        ts = self.ss // self.tm
        return [pl.BlockSpec((nj, self.tm // self.sp, self.sp + pad, LANES),
                             lambda i: (0, self._pi(i), 0, 0)),
                pl.BlockSpec((nj, None, self.ss + pad, LANES),
                             lambda i: (0, self._si(i) // ts, 0, 0))]

    def branch(self, prompt_fn, sample_fn):
        pl.when(self.is_prompt())(prompt_fn)
        pl.when(jnp.logical_not(self.is_prompt()))(sample_fn)

    def write_slabs_prompt(self, p_ref, value):
        nj, nb, rows, _ = p_ref.shape
        for j in range(nj):
            p_ref[j, :, :self.sp, :] = value[:, j * LANES:(j + 1) * LANES].reshape(nb, self.sp, LANES)
        p_ref[:, :, self.sp:, :] = jnp.zeros((nj, nb, rows - self.sp, LANES), value.dtype)

    def write_slabs_sample(self, s_ref, value):
        nj, rows, _ = s_ref.shape
        part = self._si(pl.program_id(0)) % (self.ss // self.tm)
        dst = pl.ds(pl.multiple_of(part * self.tm, self.tm), self.tm)
        for j in range(nj):
            s_ref[j, dst, :] = value[:, j * LANES:(j + 1) * LANES]
        s_ref[:, self.ss:, :] = jnp.zeros((nj, rows - self.ss, LANES), value.dtype)

    def mod(self, layer, mods):
        return pl.BlockSpec((None,) + mods.shape[1:], lambda i: (layer, 0, 0))

    def mods(self, mod_ref):
        i = pl.program_id(0)
        row = jnp.where(i < self.npt, 0, 1 + self._si(i) * self.tm // self.ss)
        m = mod_ref[pl.ds(row, 1), :]
        d = m.shape[1] // N_MOD
        return [m[:, k * d:(k + 1) * d] for k in range(N_MOD)]

    def read(self, p_ref, s_ref):
        return jnp.where(self.is_prompt(), p_ref[...].reshape(s_ref.shape), s_ref[...])


def _const_spec(shape):
    nd = len(shape)
    return pl.BlockSpec(shape, lambda i: (0,) * nd, pipeline_mode=pl.Buffered(1))


_TOKEN_PARAMS = pltpu.CompilerParams(
    dimension_semantics=("arbitrary",), vmem_limit_bytes=VMEM_LIMIT)


def _lru_in_kernel(tiles, xp_ref, xs_ref, mod_ref, g_ref, w_ref, xbp_ref, xbs_ref):
    def project(x_ref):
        m = tiles.mods(mod_ref)
        return jnp.concatenate(
            [_dot(_modulated_norm(x_ref[r, :], g_ref[...], m[0], m[1]), w_ref[...])
             for r in tiles.subs()], axis=0)

    tiles.branch(lambda: tiles.write_slabs_prompt(xbp_ref, project(xp_ref)),
                 lambda: tiles.write_slabs_sample(xbs_ref, project(xs_ref)))


def _lru_in(tiles, xp, xs, mods, g, w_x, layer, pad):
    d, w = w_x.shape
    return pl.pallas_call(
        functools.partial(_lru_in_kernel, tiles),
        grid=(tiles.n,),
        in_specs=tiles.flat_pair(d) + [tiles.mod(layer, mods), _const_spec((1, d)),
                                       _const_spec((d, w))],
        out_specs=tiles.slab_out_pair(w // LANES, pad),
        out_shape=[jax.ShapeDtypeStruct((w // LANES, tiles.bp, tiles.sp + pad, LANES), F32),
                   jax.ShapeDtypeStruct((w // LANES, tiles.bs, tiles.ss + pad, LANES), F32)],
        compiler_params=_TOKEN_PARAMS,
        name="lru_in",
    )(xp, xs, mods, g, w_x)


def _lru_rec_kernel(B, S, P, has_h0, n_cast, *refs):
    n_in = 6 + int(has_h0)
    x_ref, cw_ref, cb_ref, wg_ref, gb_ref, lam_ref = refs[:6]
    h0_ref = refs[6] if has_h0 else None
    cast_src = refs[n_in:n_in + n_cast]
    out_ref, st_ref = refs[n_in + n_cast:n_in + n_cast + 2]
    cast_dst = refs[n_in + n_cast + 2:n_in + 2 * n_cast + 2]
    xt, xhb, hfb, a_e, bx_e, a_o, bx_o = refs[n_in + 2 * n_cast + 2:]
    rows = B * S
    R = GATE_ROWS
    n = rows // R
    steps = R // B
    assert n % 2 == 0 and n >= 4 and R % B == 0

    for src, dst in zip(cast_src, cast_dst):
        dst[...] = src[...].astype(BF16)

    for b in range(B):
        out_ref[pl.ds(b * P + S, P - S), :] = jnp.zeros((P - S, LANES), F32)

    xt[pl.ds(0, B), :] = jnp.zeros((B, LANES), F32)
    xt[pl.ds(rows + B, 2 * B), :] = jnp.zeros((2 * B, LANES), F32)

    def gather(t, _):
        xt[pl.ds(pl.multiple_of((t + 1) * B, B), B), :] = x_ref[pl.ds(t, B, stride=P), :]
        return 0
    lax.fori_loop(0, S, gather, 0, unroll=4)

    cw = 0.5 * cw_ref[...]
    cb = 0.5 * cb_ref[...]
    wgz = [jnp.concatenate([wg_ref[2 * z], wg_ref[2 * z + 1]], axis=1) for z in range(2)]
    gbz = [jnp.concatenate([gb_ref[2 * z], gb_ref[2 * z + 1]], axis=1) for z in range(2)]
    neg = -lam_ref[...]
    sp = jnp.maximum(neg, 0.0) + jnp.log1p(jnp.exp(-jnp.abs(neg)))
    c_pos = (0.5 * LRU_C) * sp
    c_exp2 = -LOG2_E * c_pos

    def gates(z, c, a_dst, bx_dst):
        r0 = pl.multiple_of(c * R, R)
        if z == 0:
            xh = (cw[0:1] * xt[pl.ds(r0, R), :]
                  + cw[1:2] * xt[pl.ds(r0 + B, R), :]
                  + cw[2:3] * xt[pl.ds(r0 + 2 * B, R), :]
                  + cw[3:4] * xt[pl.ds(r0 + 3 * B, R), :]) + cb
            xhb[pl.ds(r0, R), :] = xh
        else:
            xh = xhb[pl.ds(r0, R), :]
        g = _dot(xh, wgz[z]) + gbz[z]
        r2 = jnp.tanh(g[:, :LANES]) + 1.0
        i2 = jnp.tanh(g[:, LANES:]) + 1.0
        a = jnp.exp2(r2 * c_exp2[z:z + 1])
        om = jnp.tanh(r2 * c_pos[z:z + 1]) * (a * a + 1.0)
        mult = jnp.where(om > 0.0, om * lax.rsqrt(om), 0.0)
        a_dst[...] = a
        bx_dst[...] = mult * (i2 * xh)

    def scan_f(c, a_src, bx_src, h):
        for k in range(steps):
            h = a_src[k * B:(k + 1) * B, :] * h + bx_src[k * B:(k + 1) * B, :]
            hfb[pl.ds(pl.multiple_of(c * R + k * B, B), B), :] = h
        return h

    def scan_b(c, a_src, bx_src, h):
        for k in reversed(range(steps)):
            h = a_src[k * B:(k + 1) * B, :] * h + bx_src[k * B:(k + 1) * B, :]
            hf = hfb[pl.ds(pl.multiple_of(c * R + k * B, B), B), :]
            out_ref[pl.ds(c * steps + k, B, stride=P), :] = hf + h
        return h

    def run_pass(z, chunk, scan, h):
        gates(z, chunk(0), a_e, bx_e)

        def body(k, h):
            i = 2 * k + 1
            gates(z, chunk(i), a_o, bx_o)
            h = scan(chunk(i - 1), a_e, bx_e, h)
            gates(z, chunk(i + 1), a_e, bx_e)
            return scan(chunk(i), a_o, bx_o, h)
        h = lax.fori_loop(0, n // 2 - 1, body, h)
        gates(z, chunk(n - 1), a_o, bx_o)
        h = scan(chunk(n - 2), a_e, bx_e, h)
        return scan(chunk(n - 1), a_o, bx_o, h)

    if has_h0:
        h0f, h0b = h0_ref[0], h0_ref[1]
    else:
        h0f = h0b = jnp.zeros((B, LANES), F32)
    st_ref[0] = run_pass(0, lambda i: i, scan_f, h0f)
    st_ref[1] = run_pass(1, lambda i: n - 1 - i, scan_b, h0b)


def _lru_rec(xb, conv_w, conv_b, wg, gb, lam, h0, casts):
    nj, B, P, _ = xb.shape
    S = P - SUBLANES
    rows = B * S
    w = nj * LANES
    has_h0 = h0 is not None
    slab = pl.BlockSpec((None, B * P, LANES), lambda j: (j, 0, 0))
    state = pl.BlockSpec((2, B, LANES), lambda j: (0, 0, j))
    in_specs = [
        slab,
        pl.BlockSpec((4, LANES), lambda j: (0, j)),
        pl.BlockSpec((1, LANES), lambda j: (0, j)),
        pl.BlockSpec((4, None, LANES, LANES), lambda j: (0, j, 0, 0)),
        pl.BlockSpec((4, None, 1, LANES), lambda j: (0, j, 0, 0)),
        pl.BlockSpec((2, LANES), lambda j: (0, j)),
    ]
    args = [xb.reshape(nj, B * P, LANES), conv_w, conv_b, wg, gb, lam]
    if has_h0:
        in_specs.append(state)
        args.append(h0)
    out_specs = [slab, state]
    out_shape = [jax.ShapeDtypeStruct((nj, B * P, LANES), F32),
                 jax.ShapeDtypeStruct((2, B, w), F32)]
    for wt, layer in casts:
        _, r, c = wt.shape
        assert r % (nj * 2 * SUBLANES) == 0
        in_specs.append(pl.BlockSpec((None, r // nj, c), functools.partial(
            lambda j, layer: (layer, j, 0), layer=layer)))
        args.append(wt)
        out_specs.append(pl.BlockSpec((r // nj, c), lambda j: (j, 0)))
        out_shape.append(jax.ShapeDtypeStruct((r, c), BF16))
    res = pl.pallas_call(
        functools.partial(_lru_rec_kernel, B, S, P, has_h0, len(casts)),
        grid=(nj,),
        in_specs=in_specs,
        out_specs=out_specs,
        out_shape=out_shape,
        scratch_shapes=[pltpu.VMEM((rows + 3 * B, LANES), F32)]
        + [pltpu.VMEM((rows, LANES), F32)] * 2 + [pltpu.VMEM((GATE_ROWS, LANES), F32)] * 4,
        compiler_params=pltpu.CompilerParams(
            dimension_semantics=("arbitrary",), vmem_limit_bytes=VMEM_LIMIT),
        name="lru_rec",
    )(*args)
    return res[0].reshape(nj, B, P, LANES), res[1], res[2:]


def _ffn(x1, m, g, wg_ref, wu_ref, wd_ref):
    h = [_modulated_norm(v, g, m[3], m[4]).astype(BF16) for v in x1]
    gu = [(jnp.dot(v, wg_ref[...], preferred_element_type=F32),
           jnp.dot(v, wu_ref[...], preferred_element_type=F32)) for v in h]
    dn = [_dot(jax.nn.silu(gate) * up, wd_ref[...]) for gate, up in gu]
    return [v + m[5] * d for v, d in zip(x1, dn)]


def _lru_out_kernel(tiles, xp_ref, xs_ref, hp_ref, hs_ref, mod_ref, gm_ref, gf_ref,
                    wy_ref, wo_ref, wg_ref, wu_ref, wd_ref, o_ref):
    subs = tiles.subs()

    def body(x_ref, hsum):
        m = tiles.mods(mod_ref)
        x = [x_ref[r, :] for r in subs]
        y = [jax.nn.gelu(_dot(_modulated_norm(v, gm_ref[...], m[0], m[1]), wy_ref[...]))
             for v in x]
        x1 = [v + m[2] * _dot(h * g, wo_ref[...]) for v, h, g in zip(x, hsum, y)]
        for r, v in zip(subs, _ffn(x1, m, gf_ref[...], wg_ref, wu_ref, wd_ref)):
            o_ref[r, :] = v

    nj = hp_ref.shape[0]

    def prompt_hsum(r):
        return jnp.concatenate(
            [hp_ref[j].reshape(tiles.tm, LANES)[r] for j in range(nj)], axis=1)

    def sample_hsum(r):
        return jnp.concatenate([hs_ref[j, r, :] for j in range(nj)], axis=1)

    tiles.branch(lambda: body(xp_ref, [prompt_hsum(r) for r in subs]),
                 lambda: body(xs_ref, [sample_hsum(r) for r in subs]))


def _lru_out(tiles, xp, xs, hp, hs, mods, g_mix, g_ffn, w_in, w_out, w_gate, w_up, w_down, layer):
    d = xp.shape[1]
    w = w_out.shape[0]
    f = w_gate.shape[1]
    w_y = pl.BlockSpec((d, w), lambda i: (0, 1), pipeline_mode=pl.Buffered(1))
    return pl.pallas_call(
        functools.partial(_lru_out_kernel, tiles),
        grid=(tiles.n,),
        in_specs=tiles.flat_pair(d) + tiles.slab_pair(w // LANES) + [
            tiles.mod(layer, mods), _const_spec((1, d)), _const_spec((1, d)), w_y,
            _const_spec((w, d)), _const_spec((d, f)), _const_spec((d, f)), _const_spec((f, d))],
        out_specs=tiles.combined(d),
        out_shape=jax.ShapeDtypeStruct((tiles.n * tiles.tm, d), F32),
        compiler_params=_TOKEN_PARAMS,
        name="lru_out_ffn",
    )(xp, xs, hp, hs, mods, g_mix, g_ffn, w_in, w_out, w_gate, w_up, w_down)


def _cm_kernel(tiles, x_ref, mod_ref, gm_ref, gf_ref, gfin_ref, wi_ref, bi_ref, lng_ref, lnb_ref,
               ws_ref, bs_ref, wo_ref, wg_ref, wu_ref, wd_ref, op_ref, os_ref):
    cw = lng_ref.shape[-1]
    n_groups = ws_ref.shape[0]
    gd = cw // n_groups

    def body():
        m = tiles.mods(mod_ref)
        xs = [x_ref[r, :] for r in tiles.subs()]
        pre = [_dot(_modulated_norm(x, gm_ref[...], m[0], m[1]), wi_ref[...]) + bi_ref[...]
               for x in xs]
        x1 = []
        for x, z in zip(xs, pre):
            uv = jax.nn.gelu(z)
            u = uv[:, :cw]
            v = uv[:, cw:]
            mu = jnp.mean(v, axis=-1, keepdims=True)
            var = jnp.mean(jnp.square(v - mu), axis=-1, keepdims=True)
            v = ((v - mu) * lax.rsqrt(var + EPS) * lng_ref[...] + lnb_ref[...]).astype(BF16)
            mixed_rows = []
            for c in range(SUB // CHUNK):
                vc = v[c * CHUNK:(c + 1) * CHUNK]
                cols = [jnp.dot(ws_ref[g], vc[:, g * gd:(g + 1) * gd],
                                preferred_element_type=F32) for g in range(n_groups)]
                mixed_rows.append(jnp.concatenate(cols, axis=1) + bs_ref[...])
            mixed = jnp.concatenate(mixed_rows, axis=0)
            x1.append(x + m[2] * _dot(u * mixed, wo_ref[...]))
        x2 = _ffn(x1, m, gf_ref[...], wg_ref, wu_ref, wd_ref)
        return jnp.concatenate([_rmsnorm(v, gfin_ref[...]) for v in x2], axis=0)

    out = body()

    def put(o_ref):
        o_ref[...] = out
    tiles.branch(lambda: put(op_ref), lambda: put(os_ref))


def _cm_layer(tiles, x, mods, g_mix, g_ffn, g_final, w_in, b_in, ln_g, ln_b, w_s, b_s, w_out,
              w_gate, w_up, w_down, layer):
    d = x.shape[1]
    cw = w_out.shape[0]
    f = w_gate.shape[1]
    return pl.pallas_call(
        functools.partial(_cm_kernel, tiles),
        grid=(tiles.n,),
        in_specs=[tiles.combined(d), tiles.mod(layer, mods),
                  _const_spec((1, d)), _const_spec((1, d)), _const_spec((1, d)),
                  _const_spec((d, 2 * cw)), _const_spec((1, 2 * cw)),
                  _const_spec((1, cw)), _const_spec((1, cw)),
                  _const_spec(w_s.shape), _const_spec(b_s.shape), _const_spec((cw, d)),
                  _const_spec((d, f)), _const_spec((d, f)), _const_spec((f, d))],
        out_specs=tiles.flat_pair(d),
        out_shape=[jax.ShapeDtypeStruct((tiles.npt * tiles.tm, d), F32),
                   jax.ShapeDtypeStruct((tiles.nst * tiles.tm, d), F32)],
        compiler_params=_TOKEN_PARAMS,
        name="cm_layer",
    )(x, mods, g_mix, g_ffn, g_final, w_in, b_in, ln_g, ln_b, w_s, b_s, w_out,
      w_gate, w_up, w_down)


def _gate_weights(ga_w, gx_w, ga_b, gx_b):
    _, heads, hd, _ = ga_w.shape
    per = LANES // hd
    nj = heads // per
    w4 = jnp.stack([ga_w[0], gx_w[0], ga_w[1], gx_w[1]]).reshape(4, nj, per, hd, hd)
    rows = []
    for h in range(per):
        blocks = [w4[:, :, h] if g == h else jnp.zeros_like(w4[:, :, h]) for g in range(per)]
        rows.append(jnp.concatenate(blocks, axis=-1))
    wg = jnp.concatenate(rows, axis=-2).astype(BF16)
    gb = 0.5 * jnp.stack([ga_b[0], gx_b[0], ga_b[1], gx_b[1]]).reshape(4, nj, 1, LANES)
    return wg, gb


def kernel(x_prompt, x_sample, state_lru, c, c_ctx, mod_w, mod_b, norm_mix, norm_ffn, lru_w_in, lru_conv_w, lru_conv_b, lru_ga_w, lru_ga_b, lru_gx_w, lru_gx_b, lru_lambda, lru_w_out, cm_w_in, cm_b_in, cm_ln_g, cm_ln_b, cm_w_s, cm_b_s, cm_w_out, ffn_w_gate, ffn_w_up, ffn_w_down, final_norm):
    bp, sp, d = x_prompt.shape
    bs, ss, _ = x_sample.shape
    dims = ((bp, sp), (bs, ss))
    light, heavy = _Tiles(dims, TM_LIGHT), _Tiles(dims, TM_HEAVY)
    xp = x_prompt.reshape(bp * sp, d)
    xs = x_sample.reshape(bs * ss, d)

    cond = jnp.concatenate([c_ctx[None], c, jnp.zeros((COND_ROWS - 1 - bs, d), F32)], 0)
    mods = _adaln(cond, mod_w, mod_b)

    w = lru_w_out.shape[1]
    xbp, xbs = _lru_in(light, xp, xs, mods, norm_mix[0:1], lru_w_in[0, :, :w].astype(BF16), 0,
                       SUBLANES)
    wg, gb = _gate_weights(lru_ga_w[0], lru_gx_w[0], lru_ga_b[0], lru_gx_b[0])
    rec = (lru_conv_w[0], lru_conv_b[0:1], wg, gb, lru_lambda[0])
    hp, st, (wi0, wo0, wg0, wu0, wd0) = _lru_rec(
        xbp, *rec, None,
        [(lru_w_in, 0), (lru_w_out, 0), (ffn_w_gate, 0), (ffn_w_up, 0), (ffn_w_down, 0)])
    hs, _, (wi1, wo1, wg1, wu1, wd1) = _lru_rec(
        xbs, *rec, jnp.transpose(state_lru[:, 0], (1, 0, 2)),
        [(cm_w_in, 0), (cm_w_out, 0), (ffn_w_gate, 1), (ffn_w_up, 1), (ffn_w_down, 1)])
    x = _lru_out(heavy, xp, xs, hp, hs, mods, norm_mix[0:1], norm_ffn[0:1], wi0, wo0, wg0, wu0,
                 wd0, 0)

    n_groups = cm_w_s.shape[1]
    gd = cm_w_out.shape[1] // n_groups
    bs_tile = jnp.repeat(cm_b_s[0].T, gd, axis=1)
    yp, ys = _cm_layer(heavy, x, mods, norm_mix[1:2], norm_ffn[1:2], final_norm[None],
                       wi1, cm_b_in[0:1], cm_ln_g[0:1], cm_ln_b[0:1],
                       cm_w_s[0].astype(BF16), bs_tile, wo1, wg1, wu1, wd1, 1)

    new_state = jnp.transpose(st, (1, 0, 2))[:, None]
    return (yp.reshape(bp, sp, d), ys.reshape(bs, ss, d), new_state)
```

```python
import functools

import jax
import jax.numpy as jnp
from jax import lax
from jax.experimental import pallas as pl
from jax.experimental.pallas import tpu as pltpu

F32 = jnp.float32
BF16 = jnp.bfloat16

EPS = 1e-6
LRU_C = 8.0
LOG2_E = 1.4426950408889634
N_MOD = 6
COND_ROWS = 16
LANES = 128
SUBLANES = 8
CHUNK = 128
TM_LIGHT = 1024
TM_HEAVY = 512
SUB = 256
GATE_ROWS = 512
GATE_STEPS = 32
VMEM_LIMIT = 56 * 1024 * 1024


def _rmsnorm(x, g):
    y = x * lax.rsqrt(jnp.mean(x * x, axis=-1, keepdims=True) + EPS)
    return y * g


def _modulated_norm(x, g, shift, scale):
    return _rmsnorm(x, g) * (1.0 + scale) + shift


def _dot(a, b):
    return jnp.dot(a.astype(BF16), b, preferred_element_type=F32)


def _adaln_kernel(c_ref, w_ref, b_ref, o_ref):
    s = jax.nn.silu(c_ref[...])
    bias = b_ref[pl.ds(pl.program_id(0), 1), :]
    o_ref[...] = _dot(s, w_ref[...].astype(BF16)) + bias


def _adaln(cond, mod_w, mod_b):
    depth, d, n = mod_w.shape
    tn = 1024
    return pl.pallas_call(
        _adaln_kernel,
        grid=(depth, n // tn),
        in_specs=[
            pl.BlockSpec((COND_ROWS, d), lambda l, j: (0, 0)),
            pl.BlockSpec((None, d, tn), lambda l, j: (l, 0, j)),
            pl.BlockSpec((depth, tn), lambda l, j: (0, j)),
        ],
        out_specs=pl.BlockSpec((None, COND_ROWS, tn), lambda l, j: (l, 0, j)),
        out_shape=jax.ShapeDtypeStruct((depth, COND_ROWS, n), F32),
        compiler_params=pltpu.CompilerParams(
            dimension_semantics=("arbitrary", "arbitrary"), vmem_limit_bytes=VMEM_LIMIT),
        name="adaln",
    )(cond, mod_w, mod_b)


class _Tiles:
    def __init__(self, dims, tm):
        (self.bp, self.sp), (self.bs, self.ss) = dims
        assert tm % self.sp == 0 and self.ss % tm == 0 and (self.bp * self.sp) % tm == 0
        assert tm % SUB == 0
        self.tm = tm
        self.npt = self.bp * self.sp // tm
        self.nst = self.bs * self.ss // tm
        self.n = self.npt + self.nst

    def subs(self):
        return [slice(s0, s0 + SUB) for s0 in range(0, self.tm, SUB)]

    def is_prompt(self):
        return pl.program_id(0) < self.npt

    def combined(self, cols):
        return pl.BlockSpec((self.tm, cols), lambda i: (i, 0))

    def _pi(self, i):
        return jnp.minimum(i, self.npt - 1)

    def _si(self, i):
        return jnp.maximum(i - self.npt, 0)

    def flat_pair(self, cols):
        return [pl.BlockSpec((self.tm, cols), lambda i: (self._pi(i), 0)),
                pl.BlockSpec((self.tm, cols), lambda i: (self._si(i), 0))]

    def slab_pair(self, nj):
        ts = self.ss // self.tm
        return [pl.BlockSpec((nj, self.tm // self.sp, self.sp, LANES),
                             lambda i: (0, self._pi(i), 0, 0)),
                pl.BlockSpec((nj, None, self.tm, LANES),
                             lambda i: (0, self._si(i) // ts, self._si(i) % ts, 0))]

    def slab_out_pair(self, nj, pad):
        ts = self.ss // self.tm
        return [pl.BlockSpec((nj, self.tm // self.sp, self.sp + pad, LANES),
                             lambda i: (0, self._pi(i), 0, 0)),
                pl.BlockSpec((nj, None, self.ss + pad, LANES),
                             lambda i: (0, self._si(i) // ts, 0, 0))]

    def branch(self, prompt_fn, sample_fn):
        pl.when(self.is_prompt())(prompt_fn)
        pl.when(jnp.logical_not(self.is_prompt()))(sample_fn)

    def write_slabs_prompt(self, p_ref, value):
        nj, nb, rows, _ = p_ref.shape
        for j in range(nj):
            p_ref[j, :, :self.sp, :] = value[:, j * LANES:(j + 1) * LANES].reshape(nb, self.sp, LANES)
        p_ref[:, :, self.sp:, :] = jnp.zeros((nj, nb, rows - self.sp, LANES), value.dtype)

    def write_slabs_sample(self, s_ref, value):
        nj, rows, _ = s_ref.shape
        part = self._si(pl.program_id(0)) % (self.ss // self.tm)
        dst = pl.ds(pl.multiple_of(part * self.tm, self.tm), self.tm)
        for j in range(nj):
            s_ref[j, dst, :] = value[:, j * LANES:(j + 1) * LANES]
        s_ref[:, self.ss:, :] = jnp.zeros((nj, rows - self.ss, LANES), value.dtype)

    def mod(self, layer, mods):
        return pl.BlockSpec((None,) + mods.shape[1:], lambda i: (layer, 0, 0))

    def mods(self, mod_ref):
        i = pl.program_id(0)
        row = jnp.where(i < self.npt, 0, 1 + self._si(i) * self.tm // self.ss)
        m = mod_ref[pl.ds(row, 1), :]
        d = m.shape[1] // N_MOD
        return [m[:, k * d:(k + 1) * d] for k in range(N_MOD)]

    def read(self, p_ref, s_ref):
        return jnp.where(self.is_prompt(), p_ref[...].reshape(s_ref.shape), s_ref[...])


def _const_spec(shape):
    nd = len(shape)
    return pl.BlockSpec(shape, lambda i: (0,) * nd, pipeline_mode=pl.Buffered(1))


_TOKEN_PARAMS = pltpu.CompilerParams(
    dimension_semantics=("arbitrary",), vmem_limit_bytes=VMEM_LIMIT)


def _lru_in_kernel(tiles, xp_ref, xs_ref, mod_ref, g_ref, w_ref, xbp_ref, xbs_ref):
    def project(x_ref):
        m = tiles.mods(mod_ref)
        return jnp.concatenate(
            [_dot(_modulated_norm(x_ref[r, :], g_ref[...], m[0], m[1]), w_ref[...])
             for r in tiles.subs()], axis=0)

    tiles.branch(lambda: tiles.write_slabs_prompt(xbp_ref, project(xp_ref)),
                 lambda: tiles.write_slabs_sample(xbs_ref, project(xs_ref)))


def _lru_in(tiles, xp, xs, mods, g, w_x, layer, pad):
    d, w = w_x.shape
    return pl.pallas_call(
        functools.partial(_lru_in_kernel, tiles),
        grid=(tiles.n,),
        in_specs=tiles.flat_pair(d) + [tiles.mod(layer, mods), _const_spec((1, d)),
                                       _const_spec((d, w))],
        out_specs=tiles.slab_out_pair(w // LANES, pad),
        out_shape=[jax.ShapeDtypeStruct((w // LANES, tiles.bp, tiles.sp + pad, LANES), F32),
                   jax.ShapeDtypeStruct((w // LANES, tiles.bs, tiles.ss + pad, LANES), F32)],
        compiler_params=_TOKEN_PARAMS,
        name="lru_in",
    )(xp, xs, mods, g, w_x)


def _gate_rows(B):
    return max(GATE_ROWS, GATE_STEPS * B)


def _lru_rec_kernel(B, S, P, has_h0, n_cast, *refs):
    n_in = 6 + int(has_h0)
    x_ref, cw_ref, cb_ref, wg_ref, gb_ref, lam_ref = refs[:6]
    h0_ref = refs[6] if has_h0 else None
    cast_src = refs[n_in:n_in + n_cast]
    out_ref, st_ref = refs[n_in + n_cast:n_in + n_cast + 2]
    cast_dst = refs[n_in + n_cast + 2:n_in + 2 * n_cast + 2]
    xt, xhb, hfb, a_e, bx_e, a_o, bx_o, g_e, g_o = refs[n_in + 2 * n_cast + 2:]
    rows = B * S
    R = _gate_rows(B)
    n = rows // R
    steps = R // B
    assert n % 2 == 0 and n >= 4 and R % B == 0

    for src, dst in zip(cast_src, cast_dst):
        dst[...] = src[...].astype(BF16)

    for b in range(B):
        out_ref[pl.ds(b * P + S, P - S), :] = jnp.zeros((P - S, LANES), F32)

    xt[pl.ds(0, B), :] = jnp.zeros((B, LANES), F32)
    xt[pl.ds(rows + B, 2 * B), :] = jnp.zeros((2 * B, LANES), F32)

    def gather(t, _):
        xt[pl.ds(pl.multiple_of((t + 1) * B, B), B), :] = x_ref[pl.ds(t, B, stride=P), :]
        return 0
    lax.fori_loop(0, S, gather, 0, unroll=4)

    cw = 0.5 * cw_ref[...]
    cb = 0.5 * cb_ref[...]
    wgz = [jnp.concatenate([wg_ref[2 * z], wg_ref[2 * z + 1]], axis=1) for z in range(2)]
    gbz = [jnp.concatenate([gb_ref[2 * z], gb_ref[2 * z + 1]], axis=1) for z in range(2)]
    neg = -lam_ref[...]
    sp = jnp.maximum(neg, 0.0) + jnp.log1p(jnp.exp(-jnp.abs(neg)))
    c_pos = (0.5 * LRU_C) * sp
    c_exp2 = -LOG2_E * c_pos

    def preact(z, c, g_dst):
        r0 = pl.multiple_of(c * R, R)
        if z == 0:
            xh = (cw[0:1] * xt[pl.ds(r0, R), :]
                  + cw[1:2] * xt[pl.ds(r0 + B, R), :]
                  + cw[2:3] * xt[pl.ds(r0 + 2 * B, R), :]
                  + cw[3:4] * xt[pl.ds(r0 + 3 * B, R), :]) + cb
            xhb[pl.ds(r0, R), :] = xh
        else:
            xh = xhb[pl.ds(r0, R), :]
        g_dst[...] = _dot(xh, wgz[z]) + gbz[z]

    def gates(z, c, g_src, a_dst, bx_dst):
        xh = xhb[pl.ds(pl.multiple_of(c * R, R), R), :]
        r2 = jnp.tanh(g_src[:, :LANES]) + 1.0
        i2 = jnp.tanh(g_src[:, LANES:]) + 1.0
        a = jnp.exp2(r2 * c_exp2[z:z + 1])
        om = jnp.tanh(r2 * c_pos[z:z + 1]) * (a * a + 1.0)
        mult = jnp.where(om > 0.0, om * lax.rsqrt(om), 0.0)
        a_dst[...] = a
        bx_dst[...] = mult * (i2 * xh)

    def scan_f(c, a_src, bx_src, h):
        for k in range(steps):
            h = a_src[k * B:(k + 1) * B, :] * h + bx_src[k * B:(k + 1) * B, :]
            hfb[pl.ds(pl.multiple_of(c * R + k * B, B), B), :] = h
        return h

    def scan_b(c, a_src, bx_src, h):
        for k in reversed(range(steps)):
            h = a_src[k * B:(k + 1) * B, :] * h + bx_src[k * B:(k + 1) * B, :]
            hf = hfb[pl.ds(pl.multiple_of(c * R + k * B, B), B), :]
            out_ref[pl.ds(c * steps + k, B, stride=P), :] = hf + h
        return h

    def run_pass(z, chunk, scan, h):
        even, odd = (g_e, a_e, bx_e), (g_o, a_o, bx_o)
        preact(z, chunk(0), g_e)
        gates(z, chunk(0), *even)
        preact(z, chunk(1), g_o)

        def body(k, h):
            i = 2 * k + 1
            preact(z, chunk(i + 1), g_e)
            gates(z, chunk(i), *odd)
            h = scan(chunk(i - 1), a_e, bx_e, h)
            preact(z, chunk(i + 2), g_o)
            gates(z, chunk(i + 1), *even)
            return scan(chunk(i), a_o, bx_o, h)
        h = lax.fori_loop(0, n // 2 - 1, body, h)
        gates(z, chunk(n - 1), *odd)
        h = scan(chunk(n - 2), a_e, bx_e, h)
        return scan(chunk(n - 1), a_o, bx_o, h)

    if has_h0:
        h0f, h0b = h0_ref[0], h0_ref[1]
    else:
        h0f = h0b = jnp.zeros((B, LANES), F32)
    st_ref[0] = run_pass(0, lambda i: i, scan_f, h0f)
    st_ref[1] = run_pass(1, lambda i: n - 1 - i, scan_b, h0b)


def _lru_rec(xb, conv_w, conv_b, wg, gb, lam, h0, casts):
    nj, B, P, _ = xb.shape
    S = P - SUBLANES
    rows = B * S
    w = nj * LANES
    has_h0 = h0 is not None
    slab = pl.BlockSpec((None, B * P, LANES), lambda j: (j, 0, 0))
    state = pl.BlockSpec((2, B, LANES), lambda j: (0, 0, j))
    in_specs = [
        slab,
        pl.BlockSpec((4, LANES), lambda j: (0, j)),
        pl.BlockSpec((1, LANES), lambda j: (0, j)),
        pl.BlockSpec((4, None, LANES, LANES), lambda j: (0, j, 0, 0)),
        pl.BlockSpec((4, None, 1, LANES), lambda j: (0, j, 0, 0)),
        pl.BlockSpec((2, LANES), lambda j: (0, j)),
    ]
    args = [xb.reshape(nj, B * P, LANES), conv_w, conv_b, wg, gb, lam]
    if has_h0:
        in_specs.append(state)
        args.append(h0)
    out_specs = [slab, state]
    out_shape = [jax.ShapeDtypeStruct((nj, B * P, LANES), F32),
                 jax.ShapeDtypeStruct((2, B, w), F32)]
    for wt, layer in casts:
        _, r, c = wt.shape
        assert r % (nj * 2 * SUBLANES) == 0
        in_specs.append(pl.BlockSpec((None, r // nj, c), functools.partial(
            lambda j, layer: (layer, j, 0), layer=layer)))
        args.append(wt)
        out_specs.append(pl.BlockSpec((r // nj, c), lambda j: (j, 0)))
        out_shape.append(jax.ShapeDtypeStruct((r, c), BF16))
    res = pl.pallas_call(
        functools.partial(_lru_rec_kernel, B, S, P, has_h0, len(casts)),
        grid=(nj,),
        in_specs=in_specs,
        out_specs=out_specs,
        out_shape=out_shape,
        scratch_shapes=[pltpu.VMEM((rows + 3 * B, LANES), F32)]
        + [pltpu.VMEM((rows, LANES), F32)] * 2 + [pltpu.VMEM((_gate_rows(B), LANES), F32)] * 4
        + [pltpu.VMEM((_gate_rows(B), 2 * LANES), F32)] * 2,
        compiler_params=pltpu.CompilerParams(
            dimension_semantics=("arbitrary",), vmem_limit_bytes=VMEM_LIMIT),
        name="lru_rec",
    )(*args)
    return res[0].reshape(nj, B, P, LANES), res[1], res[2:]


def _ffn(x1, m, g, wg_ref, wu_ref, wd_ref):
    h = [_modulated_norm(v, g, m[3], m[4]).astype(BF16) for v in x1]
    gu = [(jnp.dot(v, wg_ref[...], preferred_element_type=F32),
           jnp.dot(v, wu_ref[...], preferred_element_type=F32)) for v in h]
    dn = [_dot(jax.nn.silu(gate) * up, wd_ref[...]) for gate, up in gu]
    return [v + m[5] * d for v, d in zip(x1, dn)]


def _lru_out_kernel(tiles, xp_ref, xs_ref, hp_ref, hs_ref, mod_ref, gm_ref, gf_ref,
                    wy_ref, wo_ref, wg_ref, wu_ref, wd_ref, o_ref):
    subs = tiles.subs()

    def body(x_ref, hsum):
        m = tiles.mods(mod_ref)
        x = [x_ref[r, :] for r in subs]
        y = [jax.nn.gelu(_dot(_modulated_norm(v, gm_ref[...], m[0], m[1]), wy_ref[...]))
             for v in x]
        x1 = [v + m[2] * _dot(h * g, wo_ref[...]) for v, h, g in zip(x, hsum, y)]
        for r, v in zip(subs, _ffn(x1, m, gf_ref[...], wg_ref, wu_ref, wd_ref)):
            o_ref[r, :] = v

    nj = hp_ref.shape[0]

    def prompt_hsum(r):
        return jnp.concatenate(
            [hp_ref[j].reshape(tiles.tm, LANES)[r] for j in range(nj)], axis=1)

    def sample_hsum(r):
        return jnp.concatenate([hs_ref[j, r, :] for j in range(nj)], axis=1)

    tiles.branch(lambda: body(xp_ref, [prompt_hsum(r) for r in subs]),
                 lambda: body(xs_ref, [sample_hsum(r) for r in subs]))


def _lru_out(tiles, xp, xs, hp, hs, mods, g_mix, g_ffn, w_in, w_out, w_gate, w_up, w_down, layer):
    d = xp.shape[1]
    w = w_out.shape[0]
    f = w_gate.shape[1]
    w_y = pl.BlockSpec((d, w), lambda i: (0, 1), pipeline_mode=pl.Buffered(1))
    return pl.pallas_call(
        functools.partial(_lru_out_kernel, tiles),
        grid=(tiles.n,),
        in_specs=tiles.flat_pair(d) + tiles.slab_pair(w // LANES) + [
            tiles.mod(layer, mods), _const_spec((1, d)), _const_spec((1, d)), w_y,
            _const_spec((w, d)), _const_spec((d, f)), _const_spec((d, f)), _const_spec((f, d))],
        out_specs=tiles.combined(d),
        out_shape=jax.ShapeDtypeStruct((tiles.n * tiles.tm, d), F32),
        compiler_params=_TOKEN_PARAMS,
        name="lru_out_ffn",
    )(xp, xs, hp, hs, mods, g_mix, g_ffn, w_in, w_out, w_gate, w_up, w_down)


def _cm_kernel(tiles, x_ref, mod_ref, gm_ref, gf_ref, gfin_ref, wi_ref, bi_ref, lng_ref, lnb_ref,
               ws_ref, bs_ref, wo_ref, wg_ref, wu_ref, wd_ref, op_ref, os_ref):
    cw = lng_ref.shape[-1]
    n_groups = ws_ref.shape[0]
    gd = cw // n_groups

    def body():
        m = tiles.mods(mod_ref)
        xs = [x_ref[r, :] for r in tiles.subs()]
        pre = [_dot(_modulated_norm(x, gm_ref[...], m[0], m[1]), wi_ref[...]) + bi_ref[...]
               for x in xs]
        x1 = []
        for x, z in zip(xs, pre):
            uv = jax.nn.gelu(z)
            u = uv[:, :cw]
            v = uv[:, cw:]
            mu = jnp.mean(v, axis=-1, keepdims=True)
            var = jnp.mean(jnp.square(v - mu), axis=-1, keepdims=True)
            v = ((v - mu) * lax.rsqrt(var + EPS) * lng_ref[...] + lnb_ref[...]).astype(BF16)
            mixed_rows = []
            for c in range(SUB // CHUNK):
                vc = v[c * CHUNK:(c + 1) * CHUNK]
                cols = [jnp.dot(ws_ref[g], vc[:, g * gd:(g + 1) * gd],
                                preferred_element_type=F32) for g in range(n_groups)]
                mixed_rows.append(jnp.concatenate(cols, axis=1) + bs_ref[...])
            mixed = jnp.concatenate(mixed_rows, axis=0)
            x1.append(x + m[2] * _dot(u * mixed, wo_ref[...]))
        x2 = _ffn(x1, m, gf_ref[...], wg_ref, wu_ref, wd_ref)
        return jnp.concatenate([_rmsnorm(v, gfin_ref[...]) for v in x2], axis=0)

    out = body()

    def put(o_ref):
        o_ref[...] = out
    tiles.branch(lambda: put(op_ref), lambda: put(os_ref))


def _cm_layer(tiles, x, mods, g_mix, g_ffn, g_final, w_in, b_in, ln_g, ln_b, w_s, b_s, w_out,
              w_gate, w_up, w_down, layer):
    d = x.shape[1]
    cw = w_out.shape[0]
    f = w_gate.shape[1]
    return pl.pallas_call(
        functools.partial(_cm_kernel, tiles),
        grid=(tiles.n,),
        in_specs=[tiles.combined(d), tiles.mod(layer, mods),
                  _const_spec((1, d)), _const_spec((1, d)), _const_spec((1, d)),
                  _const_spec((d, 2 * cw)), _const_spec((1, 2 * cw)),
                  _const_spec((1, cw)), _const_spec((1, cw)),
                  _const_spec(w_s.shape), _const_spec(b_s.shape), _const_spec((cw, d)),
                  _const_spec((d, f)), _const_spec((d, f)), _const_spec((f, d))],
        out_specs=tiles.flat_pair(d),
        out_shape=[jax.ShapeDtypeStruct((tiles.npt * tiles.tm, d), F32),
                   jax.ShapeDtypeStruct((tiles.nst * tiles.tm, d), F32)],
        compiler_params=_TOKEN_PARAMS,
        name="cm_layer",
    )(x, mods, g_mix, g_ffn, g_final, w_in, b_in, ln_g, ln_b, w_s, b_s, w_out,
      w_gate, w_up, w_down)


def _gate_weights(ga_w, gx_w, ga_b, gx_b):
    _, heads, hd, _ = ga_w.shape
    per = LANES // hd
    nj = heads // per
    w4 = jnp.stack([ga_w[0], gx_w[0], ga_w[1], gx_w[1]]).reshape(4, nj, per, hd, hd)
    rows = []
    for h in range(per):
        blocks = [w4[:, :, h] if g == h else jnp.zeros_like(w4[:, :, h]) for g in range(per)]
        rows.append(jnp.concatenate(blocks, axis=-1))
    wg = jnp.concatenate(rows, axis=-2).astype(BF16)
    gb = 0.5 * jnp.stack([ga_b[0], gx_b[0], ga_b[1], gx_b[1]]).reshape(4, nj, 1, LANES)
    return wg, gb


def kernel(x_prompt, x_sample, state_lru, c, c_ctx, mod_w, mod_b, norm_mix, norm_ffn, lru_w_in, lru_conv_w, lru_conv_b, lru_ga_w, lru_ga_b, lru_gx_w, lru_gx_b, lru_lambda, lru_w_out, cm_w_in, cm_b_in, cm_ln_g, cm_ln_b, cm_w_s, cm_b_s, cm_w_out, ffn_w_gate, ffn_w_up, ffn_w_down, final_norm):
    bp, sp, d = x_prompt.shape
    bs, ss, _ = x_sample.shape
    dims = ((bp, sp), (bs, ss))
    light, heavy = _Tiles(dims, TM_LIGHT), _Tiles(dims, TM_HEAVY)
    xp = x_prompt.reshape(bp * sp, d)
    xs = x_sample.reshape(bs * ss, d)

    cond = jnp.concatenate([c_ctx[None], c, jnp.zeros((COND_ROWS - 1 - bs, d), F32)], 0)
    mods = _adaln(cond, mod_w, mod_b)

    w = lru_w_out.shape[1]
    xbp, xbs = _lru_in(light, xp, xs, mods, norm_mix[0:1], lru_w_in[0, :, :w].astype(BF16), 0,
                       SUBLANES)
    wg, gb = _gate_weights(lru_ga_w[0], lru_gx_w[0], lru_ga_b[0], lru_gx_b[0])
    rec = (lru_conv_w[0], lru_conv_b[0:1], wg, gb, lru_lambda[0])
    hp, st, (wi0, wo0, wg0, wu0, wd0) = _lru_rec(
        xbp, *rec, None,
        [(lru_w_in, 0), (lru_w_out, 0), (ffn_w_gate, 0), (ffn_w_up, 0), (ffn_w_down, 0)])
    hs, _, (wi1, wo1, wg1, wu1, wd1) = _lru_rec(
        xbs, *rec, jnp.transpose(state_lru[:, 0], (1, 0, 2)),
        [(cm_w_in, 0), (cm_w_out, 0), (ffn_w_gate, 1), (ffn_w_up, 1), (ffn_w_down, 1)])
    x = _lru_out(heavy, xp, xs, hp, hs, mods, norm_mix[0:1], norm_ffn[0:1], wi0, wo0, wg0, wu0,
                 wd0, 0)

    n_groups = cm_w_s.shape[1]
    gd = cm_w_out.shape[1] // n_groups
    bs_tile = jnp.repeat(cm_b_s[0].T, gd, axis=1)
    yp, ys = _cm_layer(heavy, x, mods, norm_mix[1:2], norm_ffn[1:2], final_norm[None],
                       wi1, cm_b_in[0:1], cm_ln_g[0:1], cm_ln_b[0:1],
                       cm_w_s[0].astype(BF16), bs_tile, wo1, wg1, wu1, wd1, 1)

    new_state = jnp.transpose(st, (1, 0, 2))[:, None]
    return (yp.reshape(bp, sp, d), ys.reshape(bs, ss, d), new_state)
```

```python
import functools

import jax
import jax.numpy as jnp
from jax import lax
from jax.experimental import pallas as pl
from jax.experimental.pallas import tpu as pltpu

F32 = jnp.float32
BF16 = jnp.bfloat16

EPS = 1e-6
LRU_C = 8.0
LOG2_E = 1.4426950408889634
N_MOD = 6
COND_ROWS = 16
LANES = 128
SUBLANES = 8
CHUNK = 128
TM_LIGHT = 1024
TM_HEAVY = 512
SUB = 256
GATE_ROWS = 1024
GATE_STEPS = 64
VMEM_LIMIT = 56 * 1024 * 1024


def _rmsnorm(x, g):
    y = x * lax.rsqrt(jnp.mean(x * x, axis=-1, keepdims=True) + EPS)
    return y * g


def _modulated_norm(x, g, shift, scale):
    return _rmsnorm(x, g) * (1.0 + scale) + shift


def _dot(a, b):
    return jnp.dot(a.astype(BF16), b, preferred_element_type=F32)


def _adaln_kernel(c_ref, w_ref, b_ref, o_ref):
    s = jax.nn.silu(c_ref[...])
    bias = b_ref[pl.ds(pl.program_id(0), 1), :]
    o_ref[...] = _dot(s, w_ref[...].astype(BF16)) + bias


def _adaln(cond, mod_w, mod_b):
    depth, d, n = mod_w.shape
    tn = 1024
    return pl.pallas_call(
        _adaln_kernel,
        grid=(depth, n // tn),
        in_specs=[
            pl.BlockSpec((COND_ROWS, d), lambda l, j: (0, 0)),
            pl.BlockSpec((None, d, tn), lambda l, j: (l, 0, j)),
            pl.BlockSpec((depth, tn), lambda l, j: (0, j)),
        ],
        out_specs=pl.BlockSpec((None, COND_ROWS, tn), lambda l, j: (l, 0, j)),
        out_shape=jax.ShapeDtypeStruct((depth, COND_ROWS, n), F32),
        compiler_params=pltpu.CompilerParams(
            dimension_semantics=("arbitrary", "arbitrary"), vmem_limit_bytes=VMEM_LIMIT),
        name="adaln",
    )(cond, mod_w, mod_b)


class _Tiles:
    def __init__(self, dims, tm):
        (self.bp, self.sp), (self.bs, self.ss) = dims
        assert tm % self.sp == 0 and self.ss % tm == 0 and (self.bp * self.sp) % tm == 0
        assert tm % SUB == 0
        self.tm = tm
        self.npt = self.bp * self.sp // tm
        self.nst = self.bs * self.ss // tm
        self.n = self.npt + self.nst

    def subs(self):
        return [slice(s0, s0 + SUB) for s0 in range(0, self.tm, SUB)]

    def is_prompt(self):
        return pl.program_id(0) < self.npt

    def combined(self, cols):
        return pl.BlockSpec((self.tm, cols), lambda i: (i, 0))

    def _pi(self, i):
        return jnp.minimum(i, self.npt - 1)

    def _si(self, i):
        return jnp.maximum(i - self.npt, 0)

    def flat_pair(self, cols):
        return [pl.BlockSpec((self.tm, cols), lambda i: (self._pi(i), 0)),
                pl.BlockSpec((self.tm, cols), lambda i: (self._si(i), 0))]

    def slab_pair(self, nj):
        ts = self.ss // self.tm
        return [pl.BlockSpec((nj, self.tm // self.sp, self.sp, LANES),
                             lambda i: (0, self._pi(i), 0, 0)),
                pl.BlockSpec((nj, None, self.tm, LANES),
                             lambda i: (0, self._si(i) // ts, self._si(i) % ts, 0))]

    def slab_out_pair(self, nj, pad):
        ts = self.ss // self.tm
        return [pl.BlockSpec((nj, self.tm // self.sp, self.sp + pad, LANES),
                             lambda i: (0, self._pi(i), 0, 0)),
                pl.BlockSpec((nj, None, self.ss + pad, LANES),
                             lambda i: (0, self._si(i) // ts, 0, 0))]

    def branch(self, prompt_fn, sample_fn):
        pl.when(self.is_prompt())(prompt_fn)
        pl.when(jnp.logical_not(self.is_prompt()))(sample_fn)

    def write_slabs_prompt(self, p_ref, value):
        nj, nb, rows, _ = p_ref.shape
        for j in range(nj):
            p_ref[j, :, :self.sp, :] = value[:, j * LANES:(j + 1) * LANES].reshape(nb, self.sp, LANES)
        p_ref[:, :, self.sp:, :] = jnp.zeros((nj, nb, rows - self.sp, LANES), value.dtype)

    def write_slabs_sample(self, s_ref, value):
        nj, rows, _ = s_ref.shape
        part = self._si(pl.program_id(0)) % (self.ss // self.tm)
        dst = pl.ds(pl.multiple_of(part * self.tm, self.tm), self.tm)
        for j in range(nj):
            s_ref[j, dst, :] = value[:, j * LANES:(j + 1) * LANES]
        s_ref[:, self.ss:, :] = jnp.zeros((nj, rows - self.ss, LANES), value.dtype)

    def mod(self, layer, mods):
        return pl.BlockSpec((None,) + mods.shape[1:], lambda i: (layer, 0, 0))

    def mods(self, mod_ref):
        i = pl.program_id(0)
        row = jnp.where(i < self.npt, 0, 1 + self._si(i) * self.tm // self.ss)
        m = mod_ref[pl.ds(row, 1), :]
        d = m.shape[1] // N_MOD
        return [m[:, k * d:(k + 1) * d] for k in range(N_MOD)]

    def read(self, p_ref, s_ref):
        return jnp.where(self.is_prompt(), p_ref[...].reshape(s_ref.shape), s_ref[...])


def _const_spec(shape):
    nd = len(shape)
    return pl.BlockSpec(shape, lambda i: (0,) * nd, pipeline_mode=pl.Buffered(1))


_TOKEN_PARAMS = pltpu.CompilerParams(
    dimension_semantics=("arbitrary",), vmem_limit_bytes=VMEM_LIMIT)


def _lru_in_kernel(tiles, xp_ref, xs_ref, mod_ref, g_ref, w_ref, xbp_ref, xbs_ref):
    def project(x_ref):
        m = tiles.mods(mod_ref)
        return jnp.concatenate(
            [_dot(_modulated_norm(x_ref[r, :], g_ref[...], m[0], m[1]), w_ref[...])
             for r in tiles.subs()], axis=0)

    tiles.branch(lambda: tiles.write_slabs_prompt(xbp_ref, project(xp_ref)),
                 lambda: tiles.write_slabs_sample(xbs_ref, project(xs_ref)))


def _lru_in(tiles, xp, xs, mods, g, w_x, layer, pad):
    d, w = w_x.shape
    return pl.pallas_call(
        functools.partial(_lru_in_kernel, tiles),
        grid=(tiles.n,),
        in_specs=tiles.flat_pair(d) + [tiles.mod(layer, mods), _const_spec((1, d)),
                                       _const_spec((d, w))],
        out_specs=tiles.slab_out_pair(w // LANES, pad),
        out_shape=[jax.ShapeDtypeStruct((w // LANES, tiles.bp, tiles.sp + pad, LANES), F32),
                   jax.ShapeDtypeStruct((w // LANES, tiles.bs, tiles.ss + pad, LANES), F32)],
        compiler_params=_TOKEN_PARAMS,
        name="lru_in",
    )(xp, xs, mods, g, w_x)


def _gate_rows(B):
    return max(GATE_ROWS, GATE_STEPS * B)


def _lru_rec_kernel(B, S, P, has_h0, n_cast, *refs):
    n_in = 6 + int(has_h0)
    x_ref, cw_ref, cb_ref, wg_ref, gb_ref, lam_ref = refs[:6]
    h0_ref = refs[6] if has_h0 else None
    cast_src = refs[n_in:n_in + n_cast]
    out_ref, st_ref = refs[n_in + n_cast:n_in + n_cast + 2]
    cast_dst = refs[n_in + n_cast + 2:n_in + 2 * n_cast + 2]
    xt, xhb, hfb, a_e, bx_e, a_o, bx_o, g_e, g_o = refs[n_in + 2 * n_cast + 2:]
    rows = B * S
    R = _gate_rows(B)
    n = rows // R
    steps = R // B
    assert n % 2 == 0 and n >= 4 and R % B == 0

    for src, dst in zip(cast_src, cast_dst):
        dst[...] = src[...].astype(BF16)

    for b in range(B):
        out_ref[pl.ds(b * P + S, P - S), :] = jnp.zeros((P - S, LANES), F32)

    xt[pl.ds(0, B), :] = jnp.zeros((B, LANES), F32)
    xt[pl.ds(rows + B, 2 * B), :] = jnp.zeros((2 * B, LANES), F32)

    def gather(t, _):
        xt[pl.ds(pl.multiple_of((t + 1) * B, B), B), :] = x_ref[pl.ds(t, B, stride=P), :]
        return 0
    lax.fori_loop(0, S, gather, 0, unroll=4)

    cw = 0.5 * cw_ref[...]
    cb = 0.5 * cb_ref[...]
    wgz = [jnp.concatenate([wg_ref[2 * z], wg_ref[2 * z + 1]], axis=1) for z in range(2)]
    gbz = [jnp.concatenate([gb_ref[2 * z], gb_ref[2 * z + 1]], axis=1) for z in range(2)]
    neg = -lam_ref[...]
    sp = jnp.maximum(neg, 0.0) + jnp.log1p(jnp.exp(-jnp.abs(neg)))
    c_pos = (0.5 * LRU_C) * sp
    c_exp2 = -LOG2_E * c_pos

    def preact(z, c, g_dst):
        r0 = pl.multiple_of(c * R, R)
        if z == 0:
            xh = (cw[0:1] * xt[pl.ds(r0, R), :]
                  + cw[1:2] * xt[pl.ds(r0 + B, R), :]
                  + cw[2:3] * xt[pl.ds(r0 + 2 * B, R), :]
                  + cw[3:4] * xt[pl.ds(r0 + 3 * B, R), :]) + cb
            xhb[pl.ds(r0, R), :] = xh
        else:
            xh = xhb[pl.ds(r0, R), :]
        g_dst[...] = _dot(xh, wgz[z]) + gbz[z]

    def gates(z, c, g_src, a_dst, bx_dst):
        xh = xhb[pl.ds(pl.multiple_of(c * R, R), R), :]
        r2 = jnp.tanh(g_src[:, :LANES]) + 1.0
        i2 = jnp.tanh(g_src[:, LANES:]) + 1.0
        a = jnp.exp2(r2 * c_exp2[z:z + 1])
        om = jnp.tanh(r2 * c_pos[z:z + 1]) * (a * a + 1.0)
        mult = jnp.where(om > 0.0, om * lax.rsqrt(om), 0.0)
        a_dst[...] = a
        bx_dst[...] = mult * (i2 * xh)

    def scan_f(c, a_src, bx_src, h):
        for k in range(steps):
            h = a_src[k * B:(k + 1) * B, :] * h + bx_src[k * B:(k + 1) * B, :]
            hfb[pl.ds(pl.multiple_of(c * R + k * B, B), B), :] = h
        return h

    def scan_b(c, a_src, bx_src, h):
        for k in reversed(range(steps)):
            h = a_src[k * B:(k + 1) * B, :] * h + bx_src[k * B:(k + 1) * B, :]
            hf = hfb[pl.ds(pl.multiple_of(c * R + k * B, B), B), :]
            out_ref[pl.ds(c * steps + k, B, stride=P), :] = hf + h
        return h

    def run_pass(z, chunk, scan, h):
        even, odd = (g_e, a_e, bx_e), (g_o, a_o, bx_o)
        preact(z, chunk(0), g_e)
        gates(z, chunk(0), *even)
        preact(z, chunk(1), g_o)

        def body(k, h):
            i = 2 * k + 1
            preact(z, chunk(i + 1), g_e)
            gates(z, chunk(i), *odd)
            h = scan(chunk(i - 1), a_e, bx_e, h)
            preact(z, chunk(i + 2), g_o)
            gates(z, chunk(i + 1), *even)
            return scan(chunk(i), a_o, bx_o, h)
        h = lax.fori_loop(0, n // 2 - 1, body, h)
        gates(z, chunk(n - 1), *odd)
        h = scan(chunk(n - 2), a_e, bx_e, h)
        return scan(chunk(n - 1), a_o, bx_o, h)

    if has_h0:
        h0f, h0b = h0_ref[0], h0_ref[1]
    else:
        h0f = h0b = jnp.zeros((B, LANES), F32)
    st_ref[0] = run_pass(0, lambda i: i, scan_f, h0f)
    st_ref[1] = run_pass(1, lambda i: n - 1 - i, scan_b, h0b)


def _lru_rec(xb, conv_w, conv_b, wg, gb, lam, h0, casts):
    nj, B, P, _ = xb.shape
    S = P - SUBLANES
    rows = B * S
    w = nj * LANES
    has_h0 = h0 is not None
    slab = pl.BlockSpec((None, B * P, LANES), lambda j: (j, 0, 0))
    state = pl.BlockSpec((2, B, LANES), lambda j: (0, 0, j))
    in_specs = [
        slab,
        pl.BlockSpec((4, LANES), lambda j: (0, j)),
        pl.BlockSpec((1, LANES), lambda j: (0, j)),
        pl.BlockSpec((4, None, LANES, LANES), lambda j: (0, j, 0, 0)),
        pl.BlockSpec((4, None, 1, LANES), lambda j: (0, j, 0, 0)),
        pl.BlockSpec((2, LANES), lambda j: (0, j)),
    ]
    args = [xb.reshape(nj, B * P, LANES), conv_w, conv_b, wg, gb, lam]
    if has_h0:
        in_specs.append(state)
        args.append(h0)
    out_specs = [slab, state]
    out_shape = [jax.ShapeDtypeStruct((nj, B * P, LANES), F32),
                 jax.ShapeDtypeStruct((2, B, w), F32)]
    for wt, layer in casts:
        _, r, c = wt.shape
        assert r % (nj * 2 * SUBLANES) == 0
        in_specs.append(pl.BlockSpec((None, r // nj, c), functools.partial(
            lambda j, layer: (layer, j, 0), layer=layer)))
        args.append(wt)
        out_specs.append(pl.BlockSpec((r // nj, c), lambda j: (j, 0)))
        out_shape.append(jax.ShapeDtypeStruct((r, c), BF16))
    res = pl.pallas_call(
        functools.partial(_lru_rec_kernel, B, S, P, has_h0, len(casts)),
        grid=(nj,),
        in_specs=in_specs,
        out_specs=out_specs,
        out_shape=out_shape,
        scratch_shapes=[pltpu.VMEM((rows + 3 * B, LANES), F32)]
        + [pltpu.VMEM((rows, LANES), F32)] * 2 + [pltpu.VMEM((_gate_rows(B), LANES), F32)] * 4
        + [pltpu.VMEM((_gate_rows(B), 2 * LANES), F32)] * 2,
        compiler_params=pltpu.CompilerParams(
            dimension_semantics=("arbitrary",), vmem_limit_bytes=VMEM_LIMIT),
        name="lru_rec",
    )(*args)
    return res[0].reshape(nj, B, P, LANES), res[1], res[2:]


def _ffn(x1, m, g, wg_ref, wu_ref, wd_ref):
    h = [_modulated_norm(v, g, m[3], m[4]).astype(BF16) for v in x1]
    gu = [(jnp.dot(v, wg_ref[...], preferred_element_type=F32),
           jnp.dot(v, wu_ref[...], preferred_element_type=F32)) for v in h]
    dn = [_dot(jax.nn.silu(gate) * up, wd_ref[...]) for gate, up in gu]
    return [v + m[5] * d for v, d in zip(x1, dn)]


def _lru_out_kernel(tiles, xp_ref, xs_ref, hp_ref, hs_ref, mod_ref, gm_ref, gf_ref,
                    wy_ref, wo_ref, wg_ref, wu_ref, wd_ref, o_ref):
    subs = tiles.subs()

    def body(x_ref, hsum):
        m = tiles.mods(mod_ref)
        x = [x_ref[r, :] for r in subs]
        y = [jax.nn.gelu(_dot(_modulated_norm(v, gm_ref[...], m[0], m[1]), wy_ref[...]))
             for v in x]
        x1 = [v + m[2] * _dot(h * g, wo_ref[...]) for v, h, g in zip(x, hsum, y)]
        for r, v in zip(subs, _ffn(x1, m, gf_ref[...], wg_ref, wu_ref, wd_ref)):
            o_ref[r, :] = v

    nj = hp_ref.shape[0]

    def prompt_hsum(r):
        return jnp.concatenate(
            [hp_ref[j].reshape(tiles.tm, LANES)[r] for j in range(nj)], axis=1)

    def sample_hsum(r):
        return jnp.concatenate([hs_ref[j, r, :] for j in range(nj)], axis=1)

    tiles.branch(lambda: body(xp_ref, [prompt_hsum(r) for r in subs]),
                 lambda: body(xs_ref, [sample_hsum(r) for r in subs]))


def _lru_out(tiles, xp, xs, hp, hs, mods, g_mix, g_ffn, w_in, w_out, w_gate, w_up, w_down, layer):
    d = xp.shape[1]
    w = w_out.shape[0]
    f = w_gate.shape[1]
    w_y = pl.BlockSpec((d, w), lambda i: (0, 1), pipeline_mode=pl.Buffered(1))
    return pl.pallas_call(
        functools.partial(_lru_out_kernel, tiles),
        grid=(tiles.n,),
        in_specs=tiles.flat_pair(d) + tiles.slab_pair(w // LANES) + [
            tiles.mod(layer, mods), _const_spec((1, d)), _const_spec((1, d)), w_y,
            _const_spec((w, d)), _const_spec((d, f)), _const_spec((d, f)), _const_spec((f, d))],
        out_specs=tiles.combined(d),
        out_shape=jax.ShapeDtypeStruct((tiles.n * tiles.tm, d), F32),
        compiler_params=_TOKEN_PARAMS,
        name="lru_out_ffn",
    )(xp, xs, hp, hs, mods, g_mix, g_ffn, w_in, w_out, w_gate, w_up, w_down)


def _cm_kernel(tiles, x_ref, mod_ref, gm_ref, gf_ref, gfin_ref, wi_ref, bi_ref, lng_ref, lnb_ref,
               ws_ref, bs_ref, wo_ref, wg_ref, wu_ref, wd_ref, op_ref, os_ref):
    cw = lng_ref.shape[-1]
    n_groups = ws_ref.shape[0]
    gd = cw // n_groups

    def body():
        m = tiles.mods(mod_ref)
        xs = [x_ref[r, :] for r in tiles.subs()]
        pre = [_dot(_modulated_norm(x, gm_ref[...], m[0], m[1]), wi_ref[...]) + bi_ref[...]
               for x in xs]
        x1 = []
        for x, z in zip(xs, pre):
            uv = jax.nn.gelu(z)
            u = uv[:, :cw]
            v = uv[:, cw:]
            mu = jnp.mean(v, axis=-1, keepdims=True)
            var = jnp.mean(jnp.square(v - mu), axis=-1, keepdims=True)
            v = ((v - mu) * lax.rsqrt(var + EPS) * lng_ref[...] + lnb_ref[...]).astype(BF16)
            mixed_rows = []
            for c in range(SUB // CHUNK):
                vc = v[c * CHUNK:(c + 1) * CHUNK]
                cols = [jnp.dot(ws_ref[g], vc[:, g * gd:(g + 1) * gd],
                                preferred_element_type=F32) for g in range(n_groups)]
                mixed_rows.append(jnp.concatenate(cols, axis=1) + bs_ref[...])
            mixed = jnp.concatenate(mixed_rows, axis=0)
            x1.append(x + m[2] * _dot(u * mixed, wo_ref[...]))
        x2 = _ffn(x1, m, gf_ref[...], wg_ref, wu_ref, wd_ref)
        return jnp.concatenate([_rmsnorm(v, gfin_ref[...]) for v in x2], axis=0)

    out = body()

    def put(o_ref):
        o_ref[...] = out
    tiles.branch(lambda: put(op_ref), lambda: put(os_ref))


def _cm_layer(tiles, x, mods, g_mix, g_ffn, g_final, w_in, b_in, ln_g, ln_b, w_s, b_s, w_out,
              w_gate, w_up, w_down, layer):
    d = x.shape[1]
    cw = w_out.shape[0]
    f = w_gate.shape[1]
    return pl.pallas_call(
        functools.partial(_cm_kernel, tiles),
        grid=(tiles.n,),
        in_specs=[tiles.combined(d), tiles.mod(layer, mods),
                  _const_spec((1, d)), _const_spec((1, d)), _const_spec((1, d)),
                  _const_spec((d, 2 * cw)), _const_spec((1, 2 * cw)),
                  _const_spec((1, cw)), _const_spec((1, cw)),
                  _const_spec(w_s.shape), _const_spec(b_s.shape), _const_spec((cw, d)),
                  _const_spec((d, f)), _const_spec((d, f)), _const_spec((f, d))],
        out_specs=tiles.flat_pair(d),
        out_shape=[jax.ShapeDtypeStruct((tiles.npt * tiles.tm, d), F32),
                   jax.ShapeDtypeStruct((tiles.nst * tiles.tm, d), F32)],
        compiler_params=_TOKEN_PARAMS,
        name="cm_layer",
    )(x, mods, g_mix, g_ffn, g_final, w_in, b_in, ln_g, ln_b, w_s, b_s, w_out,
      w_gate, w_up, w_down)


def _gate_weights(ga_w, gx_w, ga_b, gx_b):
    _, heads, hd, _ = ga_w.shape
    per = LANES // hd
    nj = heads // per
    w4 = jnp.stack([ga_w[0], gx_w[0], ga_w[1], gx_w[1]]).reshape(4, nj, per, hd, hd)
    rows = []
    for h in range(per):
        blocks = [w4[:, :, h] if g == h else jnp.zeros_like(w4[:, :, h]) for g in range(per)]
        rows.append(jnp.concatenate(blocks, axis=-1))
    wg = jnp.concatenate(rows, axis=-2).astype(BF16)
    gb = 0.5 * jnp.stack([ga_b[0], gx_b[0], ga_b[1], gx_b[1]]).reshape(4, nj, 1, LANES)
    return wg, gb


def kernel(x_prompt, x_sample, state_lru, c, c_ctx, mod_w, mod_b, norm_mix, norm_ffn, lru_w_in, lru_conv_w, lru_conv_b, lru_ga_w, lru_ga_b, lru_gx_w, lru_gx_b, lru_lambda, lru_w_out, cm_w_in, cm_b_in, cm_ln_g, cm_ln_b, cm_w_s, cm_b_s, cm_w_out, ffn_w_gate, ffn_w_up, ffn_w_down, final_norm):
    bp, sp, d = x_prompt.shape
    bs, ss, _ = x_sample.shape
    dims = ((bp, sp), (bs, ss))
    light, heavy = _Tiles(dims, TM_LIGHT), _Tiles(dims, TM_HEAVY)
    xp = x_prompt.reshape(bp * sp, d)
    xs = x_sample.reshape(bs * ss, d)

    cond = jnp.concatenate([c_ctx[None], c, jnp.zeros((COND_ROWS - 1 - bs, d), F32)], 0)
    mods = _adaln(cond, mod_w, mod_b)

    w = lru_w_out.shape[1]
    xbp, xbs = _lru_in(light, xp, xs, mods, norm_mix[0:1], lru_w_in[0, :, :w].astype(BF16), 0,
                       SUBLANES)
    wg, gb = _gate_weights(lru_ga_w[0], lru_gx_w[0], lru_ga_b[0], lru_gx_b[0])
    rec = (lru_conv_w[0], lru_conv_b[0:1], wg, gb, lru_lambda[0])
    hp, st, (wi0, wo0, wg0, wu0, wd0) = _lru_rec(
        xbp, *rec, None,
        [(lru_w_in, 0), (lru_w_out, 0), (ffn_w_gate, 0), (ffn_w_up, 0), (ffn_w_down, 0)])
    hs, _, (wi1, wo1, wg1, wu1, wd1) = _lru_rec(
        xbs, *rec, jnp.transpose(state_lru[:, 0], (1, 0, 2)),
        [(cm_w_in, 0), (cm_w_out, 0), (ffn_w_gate, 1), (ffn_w_up, 1), (ffn_w_down, 1)])
    x = _lru_out(heavy, xp, xs, hp, hs, mods, norm_mix[0:1], norm_ffn[0:1], wi0, wo0, wg0, wu0,
                 wd0, 0)

    n_groups = cm_w_s.shape[1]
    gd = cm_w_out.shape[1] // n_groups
    bs_tile = jnp.repeat(cm_b_s[0].T, gd, axis=1)
    yp, ys = _cm_layer(heavy, x, mods, norm_mix[1:2], norm_ffn[1:2], final_norm[None],
                       wi1, cm_b_in[0:1], cm_ln_g[0:1], cm_ln_b[0:1],
                       cm_w_s[0].astype(BF16), bs_tile, wo1, wg1, wu1, wd1, 1)

    new_state = jnp.transpose(st, (1, 0, 2))[:, None]
    return (yp.reshape(bp, sp, d), ys.reshape(bs, ss, d), new_state)
```

```python
import functools

import jax
import jax.numpy as jnp
from jax import lax
from jax.experimental import pallas as pl
from jax.experimental.pallas import tpu as pltpu

F32 = jnp.float32
BF16 = jnp.bfloat16

EPS = 1e-6
LRU_C = 8.0
LOG2_E = 1.4426950408889634
N_MOD = 6
COND_ROWS = 16
LANES = 128
SUBLANES = 8
CHUNK = 128
TM_LIGHT = 1024
TM_HEAVY = 512
SUB = 256
GATE_ROWS = 2048
GATE_STEPS = 64
VMEM_LIMIT = 56 * 1024 * 1024


def _rmsnorm(x, g):
    y = x * lax.rsqrt(jnp.mean(x * x, axis=-1, keepdims=True) + EPS)
    return y * g


def _modulated_norm(x, g, shift, scale):
    return _rmsnorm(x, g) * (1.0 + scale) + shift


def _dot(a, b):
    return jnp.dot(a.astype(BF16), b, preferred_element_type=F32)


def _adaln_kernel(c_ref, w_ref, b_ref, o_ref):
    s = jax.nn.silu(c_ref[...])
    bias = b_ref[pl.ds(pl.program_id(0), 1), :]
    o_ref[...] = _dot(s, w_ref[...].astype(BF16)) + bias


def _adaln(cond, mod_w, mod_b):
    depth, d, n = mod_w.shape
    tn = 1024
    return pl.pallas_call(
        _adaln_kernel,
        grid=(depth, n // tn),
        in_specs=[
            pl.BlockSpec((COND_ROWS, d), lambda l, j: (0, 0)),
            pl.BlockSpec((None, d, tn), lambda l, j: (l, 0, j)),
            pl.BlockSpec((depth, tn), lambda l, j: (0, j)),
        ],
        out_specs=pl.BlockSpec((None, COND_ROWS, tn), lambda l, j: (l, 0, j)),
        out_shape=jax.ShapeDtypeStruct((depth, COND_ROWS, n), F32),
        compiler_params=pltpu.CompilerParams(
            dimension_semantics=("arbitrary", "arbitrary"), vmem_limit_bytes=VMEM_LIMIT),
        name="adaln",
    )(cond, mod_w, mod_b)


class _Tiles:
    def __init__(self, dims, tm):
        (self.bp, self.sp), (self.bs, self.ss) = dims
        assert tm % self.sp == 0 and self.ss % tm == 0 and (self.bp * self.sp) % tm == 0
        assert tm % SUB == 0
        self.tm = tm
        self.npt = self.bp * self.sp // tm
        self.nst = self.bs * self.ss // tm
        self.n = self.npt + self.nst

    def subs(self):
        return [slice(s0, s0 + SUB) for s0 in range(0, self.tm, SUB)]

    def is_prompt(self):
        return pl.program_id(0) < self.npt

    def combined(self, cols):
        return pl.BlockSpec((self.tm, cols), lambda i: (i, 0))

    def _pi(self, i):
        return jnp.minimum(i, self.npt - 1)

    def _si(self, i):
        return jnp.maximum(i - self.npt, 0)

    def flat_pair(self, cols):
        return [pl.BlockSpec((self.tm, cols), lambda i: (self._pi(i), 0)),
                pl.BlockSpec((self.tm, cols), lambda i: (self._si(i), 0))]

    def slab_pair(self, nj):
        ts = self.ss // self.tm
        return [pl.BlockSpec((nj, self.tm // self.sp, self.sp, LANES),
                             lambda i: (0, self._pi(i), 0, 0)),
                pl.BlockSpec((nj, None, self.tm, LANES),
                             lambda i: (0, self._si(i) // ts, self._si(i) % ts, 0))]

    def slab_out_pair(self, nj, pad):
        ts = self.ss // self.tm
        return [pl.BlockSpec((nj, self.tm // self.sp, self.sp + pad, LANES),
                             lambda i: (0, self._pi(i), 0, 0)),
                pl.BlockSpec((nj, None, self.ss + pad, LANES),
                             lambda i: (0, self._si(i) // ts, 0, 0))]

    def branch(self, prompt_fn, sample_fn):
        pl.when(self.is_prompt())(prompt_fn)
        pl.when(jnp.logical_not(self.is_prompt()))(sample_fn)

    def write_slabs_prompt(self, p_ref, value):
        nj, nb, rows, _ = p_ref.shape
        for j in range(nj):
            p_ref[j, :, :self.sp, :] = value[:, j * LANES:(j + 1) * LANES].reshape(nb, self.sp, LANES)
        p_ref[:, :, self.sp:, :] = jnp.zeros((nj, nb, rows - self.sp, LANES), value.dtype)

    def write_slabs_sample(self, s_ref, value):
        nj, rows, _ = s_ref.shape
        part = self._si(pl.program_id(0)) % (self.ss // self.tm)
        dst = pl.ds(pl.multiple_of(part * self.tm, self.tm), self.tm)
        for j in range(nj):
            s_ref[j, dst, :] = value[:, j * LANES:(j + 1) * LANES]
        s_ref[:, self.ss:, :] = jnp.zeros((nj, rows - self.ss, LANES), value.dtype)

    def mod(self, layer, mods):
        return pl.BlockSpec((None,) + mods.shape[1:], lambda i: (layer, 0, 0))

    def mods(self, mod_ref):
        i = pl.program_id(0)
        row = jnp.where(i < self.npt, 0, 1 + self._si(i) * self.tm // self.ss)
        m = mod_ref[pl.ds(row, 1), :]
        d = m.shape[1] // N_MOD
        return [m[:, k * d:(k + 1) * d] for k in range(N_MOD)]

    def read(self, p_ref, s_ref):
        return jnp.where(self.is_prompt(), p_ref[...].reshape(s_ref.shape), s_ref[...])


def _const_spec(shape):
    nd = len(shape)
    return pl.BlockSpec(shape, lambda i: (0,) * nd, pipeline_mode=pl.Buffered(1))


_TOKEN_PARAMS = pltpu.CompilerParams(
    dimension_semantics=("arbitrary",), vmem_limit_bytes=VMEM_LIMIT)


def _lru_in_kernel(tiles, xp_ref, xs_ref, mod_ref, g_ref, w_ref, xbp_ref, xbs_ref):
    def project(x_ref):
        m = tiles.mods(mod_ref)
        return jnp.concatenate(
            [_dot(_modulated_norm(x_ref[r, :], g_ref[...], m[0], m[1]), w_ref[...])
             for r in tiles.subs()], axis=0)

    tiles.branch(lambda: tiles.write_slabs_prompt(xbp_ref, project(xp_ref)),
                 lambda: tiles.write_slabs_sample(xbs_ref, project(xs_ref)))


def _lru_in(tiles, xp, xs, mods, g, w_x, layer, pad):
    d, w = w_x.shape
    return pl.pallas_call(
        functools.partial(_lru_in_kernel, tiles),
        grid=(tiles.n,),
        in_specs=tiles.flat_pair(d) + [tiles.mod(layer, mods), _const_spec((1, d)),
                                       _const_spec((d, w))],
        out_specs=tiles.slab_out_pair(w // LANES, pad),
        out_shape=[jax.ShapeDtypeStruct((w // LANES, tiles.bp, tiles.sp + pad, LANES), F32),
                   jax.ShapeDtypeStruct((w // LANES, tiles.bs, tiles.ss + pad, LANES), F32)],
        compiler_params=_TOKEN_PARAMS,
        name="lru_in",
    )(xp, xs, mods, g, w_x)


def _gate_rows(B):
    return max(GATE_ROWS, GATE_STEPS * B)


def _lru_rec_kernel(B, S, P, has_h0, n_cast, *refs):
    n_in = 6 + int(has_h0)
    x_ref, cw_ref, cb_ref, wg_ref, gb_ref, lam_ref = refs[:6]
    h0_ref = refs[6] if has_h0 else None
    cast_src = refs[n_in:n_in + n_cast]
    out_ref, st_ref = refs[n_in + n_cast:n_in + n_cast + 2]
    cast_dst = refs[n_in + n_cast + 2:n_in + 2 * n_cast + 2]
    xt, xhb, hfb, a_e, bx_e, a_o, bx_o, g_e, g_o = refs[n_in + 2 * n_cast + 2:]
    rows = B * S
    R = _gate_rows(B)
    n = rows // R
    steps = R // B
    assert n % 2 == 0 and n >= 4 and R % B == 0

    for src, dst in zip(cast_src, cast_dst):
        dst[...] = src[...].astype(BF16)

    for b in range(B):
        out_ref[pl.ds(b * P + S, P - S), :] = jnp.zeros((P - S, LANES), F32)

    xt[pl.ds(0, B), :] = jnp.zeros((B, LANES), F32)
    xt[pl.ds(rows + B, 2 * B), :] = jnp.zeros((2 * B, LANES), F32)

    def gather(t, _):
        xt[pl.ds(pl.multiple_of((t + 1) * B, B), B), :] = x_ref[pl.ds(t, B, stride=P), :]
        return 0
    lax.fori_loop(0, S, gather, 0, unroll=16)

    cw = 0.5 * cw_ref[...]
    cb = 0.5 * cb_ref[...]
    wgz = [jnp.concatenate([wg_ref[2 * z], wg_ref[2 * z + 1]], axis=1) for z in range(2)]
    gbz = [jnp.concatenate([gb_ref[2 * z], gb_ref[2 * z + 1]], axis=1) for z in range(2)]
    neg = -lam_ref[...]
    sp = jnp.maximum(neg, 0.0) + jnp.log1p(jnp.exp(-jnp.abs(neg)))
    c_pos = (0.5 * LRU_C) * sp
    c_exp2 = -LOG2_E * c_pos

    def preact(z, c, g_dst):
        r0 = pl.multiple_of(c * R, R)
        if z == 0:
            xh = (cw[0:1] * xt[pl.ds(r0, R), :]
                  + cw[1:2] * xt[pl.ds(r0 + B, R), :]
                  + cw[2:3] * xt[pl.ds(r0 + 2 * B, R), :]
                  + cw[3:4] * xt[pl.ds(r0 + 3 * B, R), :]) + cb
            xhb[pl.ds(r0, R), :] = xh
        else:
            xh = xhb[pl.ds(r0, R), :]
        g_dst[...] = _dot(xh, wgz[z]) + gbz[z]

    def gates(z, c, g_src, a_dst, bx_dst):
        xh = xhb[pl.ds(pl.multiple_of(c * R, R), R), :]
        r2 = jnp.tanh(g_src[:, :LANES]) + 1.0
        i2 = jnp.tanh(g_src[:, LANES:]) + 1.0
        a = jnp.exp2(r2 * c_exp2[z:z + 1])
        om = jnp.tanh(r2 * c_pos[z:z + 1]) * (a * a + 1.0)
        mult = jnp.where(om > 0.0, om * lax.rsqrt(om), 0.0)
        a_dst[...] = a
        bx_dst[...] = mult * (i2 * xh)

    def scan_f(c, a_src, bx_src, h):
        for k in range(steps):
            h = a_src[k * B:(k + 1) * B, :] * h + bx_src[k * B:(k + 1) * B, :]
            hfb[pl.ds(pl.multiple_of(c * R + k * B, B), B), :] = h
        return h

    def scan_b(c, a_src, bx_src, h):
        for k in reversed(range(steps)):
            h = a_src[k * B:(k + 1) * B, :] * h + bx_src[k * B:(k + 1) * B, :]
            hf = hfb[pl.ds(pl.multiple_of(c * R + k * B, B), B), :]
            out_ref[pl.ds(c * steps + k, B, stride=P), :] = hf + h
        return h

    def run_pass(z, chunk, scan, h):
        even, odd = (g_e, a_e, bx_e), (g_o, a_o, bx_o)
        preact(z, chunk(0), g_e)
        gates(z, chunk(0), *even)
        preact(z, chunk(1), g_o)

        def body(k, h):
            i = 2 * k + 1
            preact(z, chunk(i + 1), g_e)
            gates(z, chunk(i), *odd)
            h = scan(chunk(i - 1), a_e, bx_e, h)
            preact(z, chunk(i + 2), g_o)
            gates(z, chunk(i + 1), *even)
            return scan(chunk(i), a_o, bx_o, h)
        h = lax.fori_loop(0, n // 2 - 1, body, h)
        gates(z, chunk(n - 1), *odd)
        h = scan(chunk(n - 2), a_e, bx_e, h)
        return scan(chunk(n - 1), a_o, bx_o, h)

    if has_h0:
        h0f, h0b = h0_ref[0], h0_ref[1]
    else:
        h0f = h0b = jnp.zeros((B, LANES), F32)
    st_ref[0] = run_pass(0, lambda i: i, scan_f, h0f)
    st_ref[1] = run_pass(1, lambda i: n - 1 - i, scan_b, h0b)


def _lru_rec(xb, conv_w, conv_b, wg, gb, lam, h0, casts):
    nj, B, P, _ = xb.shape
    S = P - SUBLANES
    rows = B * S
    w = nj * LANES
    has_h0 = h0 is not None
    slab = pl.BlockSpec((None, B * P, LANES), lambda j: (j, 0, 0))
    state = pl.BlockSpec((2, B, LANES), lambda j: (0, 0, j))
    in_specs = [
        slab,
        pl.BlockSpec((4, LANES), lambda j: (0, j)),
        pl.BlockSpec((1, LANES), lambda j: (0, j)),
        pl.BlockSpec((4, None, LANES, LANES), lambda j: (0, j, 0, 0)),
        pl.BlockSpec((4, None, 1, LANES), lambda j: (0, j, 0, 0)),
        pl.BlockSpec((2, LANES), lambda j: (0, j)),
    ]
    args = [xb.reshape(nj, B * P, LANES), conv_w, conv_b, wg, gb, lam]
    if has_h0:
        in_specs.append(state)
        args.append(h0)
    out_specs = [slab, state]
    out_shape = [jax.ShapeDtypeStruct((nj, B * P, LANES), F32),
                 jax.ShapeDtypeStruct((2, B, w), F32)]
    for wt, layer in casts:
        _, r, c = wt.shape
        assert r % (nj * 2 * SUBLANES) == 0
        in_specs.append(pl.BlockSpec((None, r // nj, c), functools.partial(
            lambda j, layer: (layer, j, 0), layer=layer)))
        args.append(wt)
        out_specs.append(pl.BlockSpec((r // nj, c), lambda j: (j, 0)))
        out_shape.append(jax.ShapeDtypeStruct((r, c), BF16))
    res = pl.pallas_call(
        functools.partial(_lru_rec_kernel, B, S, P, has_h0, len(casts)),
        grid=(nj,),
        in_specs=in_specs,
        out_specs=out_specs,
        out_shape=out_shape,
        scratch_shapes=[pltpu.VMEM((rows + 3 * B, LANES), F32)]
        + [pltpu.VMEM((rows, LANES), F32)] * 2 + [pltpu.VMEM((_gate_rows(B), LANES), F32)] * 4
        + [pltpu.VMEM((_gate_rows(B), 2 * LANES), F32)] * 2,
        compiler_params=pltpu.CompilerParams(
            dimension_semantics=("arbitrary",), vmem_limit_bytes=VMEM_LIMIT),
        name="lru_rec",
    )(*args)
    return res[0].reshape(nj, B, P, LANES), res[1], res[2:]


def _ffn(x1, m, g, wg_ref, wu_ref, wd_ref):
    h = [_modulated_norm(v, g, m[3], m[4]).astype(BF16) for v in x1]
    gu = [(jnp.dot(v, wg_ref[...], preferred_element_type=F32),
           jnp.dot(v, wu_ref[...], preferred_element_type=F32)) for v in h]
    dn = [_dot(jax.nn.silu(gate) * up, wd_ref[...]) for gate, up in gu]
    return [v + m[5] * d for v, d in zip(x1, dn)]


def _lru_out_kernel(tiles, xp_ref, xs_ref, hp_ref, hs_ref, mod_ref, gm_ref, gf_ref,
                    wy_ref, wo_ref, wg_ref, wu_ref, wd_ref, o_ref):
    subs = tiles.subs()

    def body(x_ref, hsum):
        m = tiles.mods(mod_ref)
        x = [x_ref[r, :] for r in subs]
        y = [jax.nn.gelu(_dot(_modulated_norm(v, gm_ref[...], m[0], m[1]), wy_ref[...]))
             for v in x]
        x1 = [v + m[2] * _dot(h * g, wo_ref[...]) for v, h, g in zip(x, hsum, y)]
        for r, v in zip(subs, _ffn(x1, m, gf_ref[...], wg_ref, wu_ref, wd_ref)):
            o_ref[r, :] = v

    nj = hp_ref.shape[0]

    def prompt_hsum(r):
        return jnp.concatenate(
            [hp_ref[j].reshape(tiles.tm, LANES)[r] for j in range(nj)], axis=1)

    def sample_hsum(r):
        return jnp.concatenate([hs_ref[j, r, :] for j in range(nj)], axis=1)

    tiles.branch(lambda: body(xp_ref, [prompt_hsum(r) for r in subs]),
                 lambda: body(xs_ref, [sample_hsum(r) for r in subs]))


def _lru_out(tiles, xp, xs, hp, hs, mods, g_mix, g_ffn, w_in, w_out, w_gate, w_up, w_down, layer):
    d = xp.shape[1]
    w = w_out.shape[0]
    f = w_gate.shape[1]
    w_y = pl.BlockSpec((d, w), lambda i: (0, 1), pipeline_mode=pl.Buffered(1))
    return pl.pallas_call(
        functools.partial(_lru_out_kernel, tiles),
        grid=(tiles.n,),
        in_specs=tiles.flat_pair(d) + tiles.slab_pair(w // LANES) + [
            tiles.mod(layer, mods), _const_spec((1, d)), _const_spec((1, d)), w_y,
            _const_spec((w, d)), _const_spec((d, f)), _const_spec((d, f)), _const_spec((f, d))],
        out_specs=tiles.combined(d),
        out_shape=jax.ShapeDtypeStruct((tiles.n * tiles.tm, d), F32),
        compiler_params=_TOKEN_PARAMS,
        name="lru_out_ffn",
    )(xp, xs, hp, hs, mods, g_mix, g_ffn, w_in, w_out, w_gate, w_up, w_down)


def _cm_kernel(tiles, x_ref, mod_ref, gm_ref, gf_ref, gfin_ref, wi_ref, bi_ref, lng_ref, lnb_ref,
               ws_ref, bs_ref, wo_ref, wg_ref, wu_ref, wd_ref, op_ref, os_ref):
    cw = lng_ref.shape[-1]
    n_groups = ws_ref.shape[0]
    gd = cw // n_groups

    def body():
        m = tiles.mods(mod_ref)
        xs = [x_ref[r, :] for r in tiles.subs()]
        pre = [_dot(_modulated_norm(x, gm_ref[...], m[0], m[1]), wi_ref[...]) + bi_ref[...]
               for x in xs]
        x1 = []
        for x, z in zip(xs, pre):
            uv = jax.nn.gelu(z)
            u = uv[:, :cw]
            v = uv[:, cw:]
            mu = jnp.mean(v, axis=-1, keepdims=True)
            var = jnp.mean(jnp.square(v - mu), axis=-1, keepdims=True)
            v = ((v - mu) * lax.rsqrt(var + EPS) * lng_ref[...] + lnb_ref[...]).astype(BF16)
            mixed_rows = []
            for c in range(SUB // CHUNK):
                vc = v[c * CHUNK:(c + 1) * CHUNK]
                cols = [jnp.dot(ws_ref[g], vc[:, g * gd:(g + 1) * gd],
                                preferred_element_type=F32) for g in range(n_groups)]
                mixed_rows.append(jnp.concatenate(cols, axis=1) + bs_ref[...])
            mixed = jnp.concatenate(mixed_rows, axis=0)
            x1.append(x + m[2] * _dot(u * mixed, wo_ref[...]))
        x2 = _ffn(x1, m, gf_ref[...], wg_ref, wu_ref, wd_ref)
        return jnp.concatenate([_rmsnorm(v, gfin_ref[...]) for v in x2], axis=0)

    out = body()

    def put(o_ref):
        o_ref[...] = out
    tiles.branch(lambda: put(op_ref), lambda: put(os_ref))


def _cm_layer(tiles, x, mods, g_mix, g_ffn, g_final, w_in, b_in, ln_g, ln_b, w_s, b_s, w_out,
              w_gate, w_up, w_down, layer):
    d = x.shape[1]
    cw = w_out.shape[0]
    f = w_gate.shape[1]
    return pl.pallas_call(
        functools.partial(_cm_kernel, tiles),
        grid=(tiles.n,),
        in_specs=[tiles.combined(d), tiles.mod(layer, mods),
                  _const_spec((1, d)), _const_spec((1, d)), _const_spec((1, d)),
                  _const_spec((d, 2 * cw)), _const_spec((1, 2 * cw)),
                  _const_spec((1, cw)), _const_spec((1, cw)),
                  _const_spec(w_s.shape), _const_spec(b_s.shape), _const_spec((cw, d)),
                  _const_spec((d, f)), _const_spec((d, f)), _const_spec((f, d))],
        out_specs=tiles.flat_pair(d),
        out_shape=[jax.ShapeDtypeStruct((tiles.npt * tiles.tm, d), F32),
                   jax.ShapeDtypeStruct((tiles.nst * tiles.tm, d), F32)],
        compiler_params=_TOKEN_PARAMS,
        name="cm_layer",
    )(x, mods, g_mix, g_ffn, g_final, w_in, b_in, ln_g, ln_b, w_s, b_s, w_out,
      w_gate, w_up, w_down)


def _gate_weights(ga_w, gx_w, ga_b, gx_b):
    _, heads, hd, _ = ga_w.shape
    per = LANES // hd
    nj = heads // per
    w4 = jnp.stack([ga_w[0], gx_w[0], ga_w[1], gx_w[1]]).reshape(4, nj, per, hd, hd)
    rows = []
    for h in range(per):
        blocks = [w4[:, :, h] if g == h else jnp.zeros_like(w4[:, :, h]) for g in range(per)]
        rows.append(jnp.concatenate(blocks, axis=-1))
    wg = jnp.concatenate(rows, axis=-2).astype(BF16)
    gb = 0.5 * jnp.stack([ga_b[0], gx_b[0], ga_b[1], gx_b[1]]).reshape(4, nj, 1, LANES)
    return wg, gb


def kernel(x_prompt, x_sample, state_lru, c, c_ctx, mod_w, mod_b, norm_mix, norm_ffn, lru_w_in, lru_conv_w, lru_conv_b, lru_ga_w, lru_ga_b, lru_gx_w, lru_gx_b, lru_lambda, lru_w_out, cm_w_in, cm_b_in, cm_ln_g, cm_ln_b, cm_w_s, cm_b_s, cm_w_out, ffn_w_gate, ffn_w_up, ffn_w_down, final_norm):
    bp, sp, d = x_prompt.shape
    bs, ss, _ = x_sample.shape
    dims = ((bp, sp), (bs, ss))
    light, heavy = _Tiles(dims, TM_LIGHT), _Tiles(dims, TM_HEAVY)
    xp = x_prompt.reshape(bp * sp, d)
    xs = x_sample.reshape(bs * ss, d)

    cond = jnp.concatenate([c_ctx[None], c, jnp.zeros((COND_ROWS - 1 - bs, d), F32)], 0)
    mods = _adaln(cond, mod_w, mod_b)

    w = lru_w_out.shape[1]
    xbp, xbs = _lru_in(light, xp, xs, mods, norm_mix[0:1], lru_w_in[0, :, :w].astype(BF16), 0,
                       SUBLANES)
    wg, gb = _gate_weights(lru_ga_w[0], lru_gx_w[0], lru_ga_b[0], lru_gx_b[0])
    rec = (lru_conv_w[0], lru_conv_b[0:1], wg, gb, lru_lambda[0])
    hp, st, (wi0, wo0, wg0, wu0, wd0) = _lru_rec(
        xbp, *rec, None,
        [(lru_w_in, 0), (lru_w_out, 0), (ffn_w_gate, 0), (ffn_w_up, 0), (ffn_w_down, 0)])
    hs, _, (wi1, wo1, wg1, wu1, wd1) = _lru_rec(
        xbs, *rec, jnp.transpose(state_lru[:, 0], (1, 0, 2)),
        [(cm_w_in, 0), (cm_w_out, 0), (ffn_w_gate, 1), (ffn_w_up, 1), (ffn_w_down, 1)])
    x = _lru_out(heavy, xp, xs, hp, hs, mods, norm_mix[0:1], norm_ffn[0:1], wi0, wo0, wg0, wu0,
                 wd0, 0)

    n_groups = cm_w_s.shape[1]
    gd = cm_w_out.shape[1] // n_groups
    bs_tile = jnp.repeat(cm_b_s[0].T, gd, axis=1)
    yp, ys = _cm_layer(heavy, x, mods, norm_mix[1:2], norm_ffn[1:2], final_norm[None],
                       wi1, cm_b_in[0:1], cm_ln_g[0:1], cm_ln_b[0:1],
                       cm_w_s[0].astype(BF16), bs_tile, wo1, wg1, wu1, wd1, 1)

    new_state = jnp.transpose(st, (1, 0, 2))[:, None]
    return (yp.reshape(bp, sp, d), ys.reshape(bs, ss, d), new_state)
```

```python
import functools

import jax
import jax.numpy as jnp
from jax import lax
from jax.experimental import pallas as pl
from jax.experimental.pallas import tpu as pltpu

F32 = jnp.float32
BF16 = jnp.bfloat16

EPS = 1e-6
LRU_C = 8.0
LOG2_E = 1.4426950408889634
N_MOD = 6
COND_ROWS = 16
LANES = 128
SUBLANES = 8
CHUNK = 128
TM_LIGHT = 1024
TM_HEAVY = 512
SUB = 256
GATE_ROWS = 2048
GATE_STEPS = 64
BIAS_PARTS = 3
VMEM_LIMIT = 56 * 1024 * 1024


def _rmsnorm(x, g):
    y = x * lax.rsqrt(jnp.mean(x * x, axis=-1, keepdims=True) + EPS)
    return y * g


def _modulated_norm(x, g, shift, scale):
    return _rmsnorm(x, g) * (1.0 + scale) + shift


def _dot(a, b):
    return jnp.dot(a.astype(BF16), b, preferred_element_type=F32)


def _adaln_kernel(c_ref, w_ref, b_ref, o_ref):
    s = jax.nn.silu(c_ref[...])
    bias = b_ref[pl.ds(pl.program_id(0), 1), :]
    o_ref[...] = _dot(s, w_ref[...].astype(BF16)) + bias


def _adaln(cond, mod_w, mod_b):
    depth, d, n = mod_w.shape
    tn = 1024
    return pl.pallas_call(
        _adaln_kernel,
        grid=(depth, n // tn),
        in_specs=[
            pl.BlockSpec((COND_ROWS, d), lambda l, j: (0, 0)),
            pl.BlockSpec((None, d, tn), lambda l, j: (l, 0, j)),
            pl.BlockSpec((depth, tn), lambda l, j: (0, j)),
        ],
        out_specs=pl.BlockSpec((None, COND_ROWS, tn), lambda l, j: (l, 0, j)),
        out_shape=jax.ShapeDtypeStruct((depth, COND_ROWS, n), F32),
        compiler_params=pltpu.CompilerParams(
            dimension_semantics=("arbitrary", "arbitrary"), vmem_limit_bytes=VMEM_LIMIT),
        name="adaln",
    )(cond, mod_w, mod_b)


class _Tiles:
    def __init__(self, dims, tm):
        (self.bp, self.sp), (self.bs, self.ss) = dims
        assert tm % self.sp == 0 and self.ss % tm == 0 and (self.bp * self.sp) % tm == 0
        assert tm % SUB == 0
        self.tm = tm
        self.npt = self.bp * self.sp // tm
        self.nst = self.bs * self.ss // tm
        self.n = self.npt + self.nst

    def subs(self):
        return [slice(s0, s0 + SUB) for s0 in range(0, self.tm, SUB)]

    def is_prompt(self):
        return pl.program_id(0) < self.npt

    def combined(self, cols):
        return pl.BlockSpec((self.tm, cols), lambda i: (i, 0))

    def _pi(self, i):
        return jnp.minimum(i, self.npt - 1)

    def _si(self, i):
        return jnp.maximum(i - self.npt, 0)

    def flat_pair(self, cols):
        return [pl.BlockSpec((self.tm, cols), lambda i: (self._pi(i), 0)),
                pl.BlockSpec((self.tm, cols), lambda i: (self._si(i), 0))]

    def slab_pair(self, nj):
        ts = self.ss // self.tm
        return [pl.BlockSpec((nj, self.tm // self.sp, self.sp, LANES),
                             lambda i: (0, self._pi(i), 0, 0)),
                pl.BlockSpec((nj, None, self.tm, LANES),
                             lambda i: (0, self._si(i) // ts, self._si(i) % ts, 0))]

    def slab_out_pair(self, nj, pad):
        ts = self.ss // self.tm
        return [pl.BlockSpec((nj, self.tm // self.sp, self.sp + pad, LANES),
                             lambda i: (0, self._pi(i), 0, 0)),
                pl.BlockSpec((nj, None, self.ss + pad, LANES),
                             lambda i: (0, self._si(i) // ts, 0, 0))]

    def branch(self, prompt_fn, sample_fn):
        pl.when(self.is_prompt())(prompt_fn)
        pl.when(jnp.logical_not(self.is_prompt()))(sample_fn)

    def write_slabs_prompt(self, p_ref, value):
        nj, nb, rows, _ = p_ref.shape
        for j in range(nj):
            p_ref[j, :, :self.sp, :] = value[:, j * LANES:(j + 1) * LANES].reshape(nb, self.sp, LANES)
        p_ref[:, :, self.sp:, :] = jnp.zeros((nj, nb, rows - self.sp, LANES), value.dtype)

    def write_slabs_sample(self, s_ref, value):
        nj, rows, _ = s_ref.shape
        part = self._si(pl.program_id(0)) % (self.ss // self.tm)
        dst = pl.ds(pl.multiple_of(part * self.tm, self.tm), self.tm)
        for j in range(nj):
            s_ref[j, dst, :] = value[:, j * LANES:(j + 1) * LANES]
        s_ref[:, self.ss:, :] = jnp.zeros((nj, rows - self.ss, LANES), value.dtype)

    def mod(self, layer, mods):
        return pl.BlockSpec((None,) + mods.shape[1:], lambda i: (layer, 0, 0))

    def mods(self, mod_ref):
        i = pl.program_id(0)
        row = jnp.where(i < self.npt, 0, 1 + self._si(i) * self.tm // self.ss)
        m = mod_ref[pl.ds(row, 1), :]
        d = m.shape[1] // N_MOD
        return [m[:, k * d:(k + 1) * d] for k in range(N_MOD)]

    def read(self, p_ref, s_ref):
        return jnp.where(self.is_prompt(), p_ref[...].reshape(s_ref.shape), s_ref[...])


def _const_spec(shape):
    nd = len(shape)
    return pl.BlockSpec(shape, lambda i: (0,) * nd, pipeline_mode=pl.Buffered(1))


_TOKEN_PARAMS = pltpu.CompilerParams(
    dimension_semantics=("arbitrary",), vmem_limit_bytes=VMEM_LIMIT)


def _lru_in_kernel(tiles, xp_ref, xs_ref, mod_ref, g_ref, w_ref, xbp_ref, xbs_ref):
    def project(x_ref):
        m = tiles.mods(mod_ref)
        return jnp.concatenate(
            [_dot(_modulated_norm(x_ref[r, :], g_ref[...], m[0], m[1]), w_ref[...])
             for r in tiles.subs()], axis=0)

    tiles.branch(lambda: tiles.write_slabs_prompt(xbp_ref, project(xp_ref)),
                 lambda: tiles.write_slabs_sample(xbs_ref, project(xs_ref)))


def _lru_in(tiles, xp, xs, mods, g, w_x, layer, pad):
    d, w = w_x.shape
    return pl.pallas_call(
        functools.partial(_lru_in_kernel, tiles),
        grid=(tiles.n,),
        in_specs=tiles.flat_pair(d) + [tiles.mod(layer, mods), _const_spec((1, d)),
                                       _const_spec((d, w))],
        out_specs=tiles.slab_out_pair(w // LANES, pad),
        out_shape=[jax.ShapeDtypeStruct((w // LANES, tiles.bp, tiles.sp + pad, LANES), F32),
                   jax.ShapeDtypeStruct((w // LANES, tiles.bs, tiles.ss + pad, LANES), F32)],
        compiler_params=_TOKEN_PARAMS,
        name="lru_in",
    )(xp, xs, mods, g, w_x)


def _gate_rows(B):
    return max(GATE_ROWS, GATE_STEPS * B)


def _lru_rec_kernel(B, S, P, has_h0, n_cast, *refs):
    n_in = 5 + int(has_h0)
    x_ref, cw_ref, cb_ref, wg_ref, lam_ref = refs[:5]
    h0_ref = refs[5] if has_h0 else None
    cast_src = refs[n_in:n_in + n_cast]
    out_ref, st_ref = refs[n_in + n_cast:n_in + n_cast + 2]
    cast_dst = refs[n_in + n_cast + 2:n_in + 2 * n_cast + 2]
    xt, xhb, hfb, a_e, bx_e, a_o, bx_o, g_e, g_o = refs[n_in + 2 * n_cast + 2:]
    rows = B * S
    R = _gate_rows(B)
    n = rows // R
    steps = R // B
    assert n % 2 == 0 and n >= 4 and R % B == 0

    for src, dst in zip(cast_src, cast_dst):
        dst[...] = src[...].astype(BF16)

    for b in range(B):
        out_ref[pl.ds(b * P + S, P - S), :] = jnp.zeros((P - S, LANES), F32)

    xt[pl.ds(0, B), :] = jnp.zeros((B, LANES), F32)
    xt[pl.ds(rows + B, 2 * B), :] = jnp.zeros((2 * B, LANES), F32)

    def gather(t, _):
        xt[pl.ds(pl.multiple_of((t + 1) * B, B), B), :] = x_ref[pl.ds(t, B, stride=P), :]
        return 0
    lax.fori_loop(0, S, gather, 0, unroll=16)

    cw = 0.5 * cw_ref[...]
    cb = 0.5 * cb_ref[...]
    wgz = [jnp.concatenate([wg_ref[2 * z], wg_ref[2 * z + 1]], axis=1) for z in range(2)]
    lane = lax.broadcasted_iota(jnp.int32, (R, LANES), 1)
    bias_cols = jnp.where(lane < BIAS_PARTS, 1.0, 0.0).astype(BF16)
    neg = -lam_ref[...]
    sp = jnp.maximum(neg, 0.0) + jnp.log1p(jnp.exp(-jnp.abs(neg)))
    c_pos = (0.5 * LRU_C) * sp
    c_exp2 = -LOG2_E * c_pos

    def preact(z, c, g_dst):
        r0 = pl.multiple_of(c * R, R)
        if z == 0:
            xh = (cw[0:1] * xt[pl.ds(r0, R), :]
                  + cw[1:2] * xt[pl.ds(r0 + B, R), :]
                  + cw[2:3] * xt[pl.ds(r0 + 2 * B, R), :]
                  + cw[3:4] * xt[pl.ds(r0 + 3 * B, R), :]) + cb
            xhb[pl.ds(r0, R), :] = xh
        else:
            xh = xhb[pl.ds(r0, R), :]
        lhs = jnp.concatenate([xh.astype(BF16), bias_cols], axis=1)
        g_dst[...] = jnp.dot(lhs, wgz[z], preferred_element_type=F32)

    def gates(z, c, g_src, a_dst, bx_dst):
        xh = xhb[pl.ds(pl.multiple_of(c * R, R), R), :]
        r2 = jnp.tanh(g_src[:, :LANES]) + 1.0
        i2 = jnp.tanh(g_src[:, LANES:]) + 1.0
        a = jnp.exp2(r2 * c_exp2[z:z + 1])
        om = jnp.tanh(r2 * c_pos[z:z + 1]) * (a * a + 1.0)
        mult = jnp.where(om > 0.0, om * lax.rsqrt(om), 0.0)
        a_dst[...] = a
        bx_dst[...] = mult * (i2 * xh)

    def scan_f(c, a_src, bx_src, h):
        for k in range(steps):
            h = a_src[k * B:(k + 1) * B, :] * h + bx_src[k * B:(k + 1) * B, :]
            hfb[pl.ds(pl.multiple_of(c * R + k * B, B), B), :] = h
        return h

    def scan_b(c, a_src, bx_src, h):
        for k in reversed(range(steps)):
            h = a_src[k * B:(k + 1) * B, :] * h + bx_src[k * B:(k + 1) * B, :]
            hf = hfb[pl.ds(pl.multiple_of(c * R + k * B, B), B), :]
            out_ref[pl.ds(c * steps + k, B, stride=P), :] = hf + h
        return h

    def run_pass(z, chunk, scan, h):
        even, odd = (g_e, a_e, bx_e), (g_o, a_o, bx_o)
        preact(z, chunk(0), g_e)
        gates(z, chunk(0), *even)
        preact(z, chunk(1), g_o)

        def body(k, h):
            i = 2 * k + 1
            preact(z, chunk(i + 1), g_e)
            gates(z, chunk(i), *odd)
            h = scan(chunk(i - 1), a_e, bx_e, h)
            preact(z, chunk(i + 2), g_o)
            gates(z, chunk(i + 1), *even)
            return scan(chunk(i), a_o, bx_o, h)
        h = lax.fori_loop(0, n // 2 - 1, body, h)
        gates(z, chunk(n - 1), *odd)
        h = scan(chunk(n - 2), a_e, bx_e, h)
        return scan(chunk(n - 1), a_o, bx_o, h)

    if has_h0:
        h0f, h0b = h0_ref[0], h0_ref[1]
    else:
        h0f = h0b = jnp.zeros((B, LANES), F32)
    st_ref[0] = run_pass(0, lambda i: i, scan_f, h0f)
    st_ref[1] = run_pass(1, lambda i: n - 1 - i, scan_b, h0b)


def _lru_rec(xb, conv_w, conv_b, wg, lam, h0, casts):
    nj, B, P, _ = xb.shape
    S = P - SUBLANES
    rows = B * S
    w = nj * LANES
    has_h0 = h0 is not None
    slab = pl.BlockSpec((None, B * P, LANES), lambda j: (j, 0, 0))
    state = pl.BlockSpec((2, B, LANES), lambda j: (0, 0, j))
    in_specs = [
        slab,
        pl.BlockSpec((4, LANES), lambda j: (0, j)),
        pl.BlockSpec((1, LANES), lambda j: (0, j)),
        pl.BlockSpec((4, None, 2 * LANES, LANES), lambda j: (0, j, 0, 0)),
        pl.BlockSpec((2, LANES), lambda j: (0, j)),
    ]
    args = [xb.reshape(nj, B * P, LANES), conv_w, conv_b, wg, lam]
    if has_h0:
        in_specs.append(state)
        args.append(h0)
    out_specs = [slab, state]
    out_shape = [jax.ShapeDtypeStruct((nj, B * P, LANES), F32),
                 jax.ShapeDtypeStruct((2, B, w), F32)]
    for wt, layer in casts:
        _, r, c = wt.shape
        assert r % (nj * 2 * SUBLANES) == 0
        in_specs.append(pl.BlockSpec((None, r // nj, c), functools.partial(
            lambda j, layer: (layer, j, 0), layer=layer)))
        args.append(wt)
        out_specs.append(pl.BlockSpec((r // nj, c), lambda j: (j, 0)))
        out_shape.append(jax.ShapeDtypeStruct((r, c), BF16))
    res = pl.pallas_call(
        functools.partial(_lru_rec_kernel, B, S, P, has_h0, len(casts)),
        grid=(nj,),
        in_specs=in_specs,
        out_specs=out_specs,
        out_shape=out_shape,
        scratch_shapes=[pltpu.VMEM((rows + 3 * B, LANES), F32)]
        + [pltpu.VMEM((rows, LANES), F32)] * 2 + [pltpu.VMEM((_gate_rows(B), LANES), F32)] * 4
        + [pltpu.VMEM((_gate_rows(B), 2 * LANES), F32)] * 2,
        compiler_params=pltpu.CompilerParams(
            dimension_semantics=("arbitrary",), vmem_limit_bytes=VMEM_LIMIT),
        name="lru_rec",
    )(*args)
    return res[0].reshape(nj, B, P, LANES), res[1], res[2:]


def _ffn(x1, m, g, wg_ref, wu_ref, wd_ref):
    h = [_modulated_norm(v, g, m[3], m[4]).astype(BF16) for v in x1]
    gu = [(jnp.dot(v, wg_ref[...], preferred_element_type=F32),
           jnp.dot(v, wu_ref[...], preferred_element_type=F32)) for v in h]
    dn = [_dot(jax.nn.silu(gate) * up, wd_ref[...]) for gate, up in gu]
    return [v + m[5] * d for v, d in zip(x1, dn)]


def _lru_out_kernel(tiles, xp_ref, xs_ref, hp_ref, hs_ref, mod_ref, gm_ref, gf_ref,
                    wy_ref, wo_ref, wg_ref, wu_ref, wd_ref, o_ref):
    subs = tiles.subs()

    def body(x_ref, hsum):
        m = tiles.mods(mod_ref)
        x = [x_ref[r, :] for r in subs]
        y = [jax.nn.gelu(_dot(_modulated_norm(v, gm_ref[...], m[0], m[1]), wy_ref[...]))
             for v in x]
        x1 = [v + m[2] * _dot(h * g, wo_ref[...]) for v, h, g in zip(x, hsum, y)]
        for r, v in zip(subs, _ffn(x1, m, gf_ref[...], wg_ref, wu_ref, wd_ref)):
            o_ref[r, :] = v

    nj = hp_ref.shape[0]

    def prompt_hsum(r):
        return jnp.concatenate(
            [hp_ref[j].reshape(tiles.tm, LANES)[r] for j in range(nj)], axis=1)

    def sample_hsum(r):
        return jnp.concatenate([hs_ref[j, r, :] for j in range(nj)], axis=1)

    tiles.branch(lambda: body(xp_ref, [prompt_hsum(r) for r in subs]),
                 lambda: body(xs_ref, [sample_hsum(r) for r in subs]))


def _lru_out(tiles, xp, xs, hp, hs, mods, g_mix, g_ffn, w_in, w_out, w_gate, w_up, w_down, layer):
    d = xp.shape[1]
    w = w_out.shape[0]
    f = w_gate.shape[1]
    w_y = pl.BlockSpec((d, w), lambda i: (0, 1), pipeline_mode=pl.Buffered(1))
    return pl.pallas_call(
        functools.partial(_lru_out_kernel, tiles),
        grid=(tiles.n,),
        in_specs=tiles.flat_pair(d) + tiles.slab_pair(w // LANES) + [
            tiles.mod(layer, mods), _const_spec((1, d)), _const_spec((1, d)), w_y,
            _const_spec((w, d)), _const_spec((d, f)), _const_spec((d, f)), _const_spec((f, d))],
        out_specs=tiles.combined(d),
        out_shape=jax.ShapeDtypeStruct((tiles.n * tiles.tm, d), F32),
        compiler_params=_TOKEN_PARAMS,
        name="lru_out_ffn",
    )(xp, xs, hp, hs, mods, g_mix, g_ffn, w_in, w_out, w_gate, w_up, w_down)


def _cm_kernel(tiles, x_ref, mod_ref, gm_ref, gf_ref, gfin_ref, wi_ref, bi_ref, lng_ref, lnb_ref,
               ws_ref, bs_ref, wo_ref, wg_ref, wu_ref, wd_ref, op_ref, os_ref):
    cw = lng_ref.shape[-1]
    n_groups = ws_ref.shape[0]
    gd = cw // n_groups

    def body():
        m = tiles.mods(mod_ref)
        xs = [x_ref[r, :] for r in tiles.subs()]
        pre = [_dot(_modulated_norm(x, gm_ref[...], m[0], m[1]), wi_ref[...]) + bi_ref[...]
               for x in xs]
        x1 = []
        for x, z in zip(xs, pre):
            uv = jax.nn.gelu(z)
            u = uv[:, :cw]
            v = uv[:, cw:]
            mu = jnp.mean(v, axis=-1, keepdims=True)
            var = jnp.mean(jnp.square(v - mu), axis=-1, keepdims=True)
            v = ((v - mu) * lax.rsqrt(var + EPS) * lng_ref[...] + lnb_ref[...]).astype(BF16)
            mixed_rows = []
            for c in range(SUB // CHUNK):
                vc = v[c * CHUNK:(c + 1) * CHUNK]
                cols = [jnp.dot(ws_ref[g], vc[:, g * gd:(g + 1) * gd],
                                preferred_element_type=F32) for g in range(n_groups)]
                mixed_rows.append(jnp.concatenate(cols, axis=1) + bs_ref[...])
            mixed = jnp.concatenate(mixed_rows, axis=0)
            x1.append(x + m[2] * _dot(u * mixed, wo_ref[...]))
        x2 = _ffn(x1, m, gf_ref[...], wg_ref, wu_ref, wd_ref)
        return jnp.concatenate([_rmsnorm(v, gfin_ref[...]) for v in x2], axis=0)

    out = body()

    def put(o_ref):
        o_ref[...] = out
    tiles.branch(lambda: put(op_ref), lambda: put(os_ref))


def _cm_layer(tiles, x, mods, g_mix, g_ffn, g_final, w_in, b_in, ln_g, ln_b, w_s, b_s, w_out,
              w_gate, w_up, w_down, layer):
    d = x.shape[1]
    cw = w_out.shape[0]
    f = w_gate.shape[1]
    return pl.pallas_call(
        functools.partial(_cm_kernel, tiles),
        grid=(tiles.n,),
        in_specs=[tiles.combined(d), tiles.mod(layer, mods),
                  _const_spec((1, d)), _const_spec((1, d)), _const_spec((1, d)),
                  _const_spec((d, 2 * cw)), _const_spec((1, 2 * cw)),
                  _const_spec((1, cw)), _const_spec((1, cw)),
                  _const_spec(w_s.shape), _const_spec(b_s.shape), _const_spec((cw, d)),
                  _const_spec((d, f)), _const_spec((d, f)), _const_spec((f, d))],
        out_specs=tiles.flat_pair(d),
        out_shape=[jax.ShapeDtypeStruct((tiles.npt * tiles.tm, d), F32),
                   jax.ShapeDtypeStruct((tiles.nst * tiles.tm, d), F32)],
        compiler_params=_TOKEN_PARAMS,
        name="cm_layer",
    )(x, mods, g_mix, g_ffn, g_final, w_in, b_in, ln_g, ln_b, w_s, b_s, w_out,
      w_gate, w_up, w_down)


def _gate_weights(ga_w, gx_w, ga_b, gx_b):
    _, heads, hd, _ = ga_w.shape
    per = LANES // hd
    nj = heads // per
    w4 = jnp.stack([ga_w[0], gx_w[0], ga_w[1], gx_w[1]]).reshape(4, nj, per, hd, hd)
    rows = []
    for h in range(per):
        blocks = [w4[:, :, h] if g == h else jnp.zeros_like(w4[:, :, h]) for g in range(per)]
        rows.append(jnp.concatenate(blocks, axis=-1))
    rest = 0.5 * jnp.stack([ga_b[0], gx_b[0], ga_b[1], gx_b[1]]).reshape(4, nj, 1, LANES)
    for _ in range(BIAS_PARTS):
        scaled = rest * (2.0 ** 16 + 1.0)
        part = scaled - (scaled - rest)
        rows.append(part)
        rest = rest - part
    rows.append(jnp.zeros((4, nj, LANES - BIAS_PARTS, LANES), F32))
    return jnp.concatenate(rows, axis=-2).astype(BF16)


def kernel(x_prompt, x_sample, state_lru, c, c_ctx, mod_w, mod_b, norm_mix, norm_ffn, lru_w_in, lru_conv_w, lru_conv_b, lru_ga_w, lru_ga_b, lru_gx_w, lru_gx_b, lru_lambda, lru_w_out, cm_w_in, cm_b_in, cm_ln_g, cm_ln_b, cm_w_s, cm_b_s, cm_w_out, ffn_w_gate, ffn_w_up, ffn_w_down, final_norm):
    bp, sp, d = x_prompt.shape
    bs, ss, _ = x_sample.shape
    dims = ((bp, sp), (bs, ss))
    light, heavy = _Tiles(dims, TM_LIGHT), _Tiles(dims, TM_HEAVY)
    xp = x_prompt.reshape(bp * sp, d)
    xs = x_sample.reshape(bs * ss, d)

    cond = jnp.concatenate([c_ctx[None], c, jnp.zeros((COND_ROWS - 1 - bs, d), F32)], 0)
    mods = _adaln(cond, mod_w, mod_b)

    w = lru_w_out.shape[1]
    xbp, xbs = _lru_in(light, xp, xs, mods, norm_mix[0:1], lru_w_in[0, :, :w].astype(BF16), 0,
                       SUBLANES)
    wg = _gate_weights(lru_ga_w[0], lru_gx_w[0], lru_ga_b[0], lru_gx_b[0])
    rec = (lru_conv_w[0], lru_conv_b[0:1], wg, lru_lambda[0])
    hp, st, (wi0, wo0, wg0, wu0, wd0) = _lru_rec(
        xbp, *rec, None,
        [(lru_w_in, 0), (lru_w_out, 0), (ffn_w_gate, 0), (ffn_w_up, 0), (ffn_w_down, 0)])
    hs, _, (wi1, wo1, wg1, wu1, wd1) = _lru_rec(
        xbs, *rec, jnp.transpose(state_lru[:, 0], (1, 0, 2)),
        [(cm_w_in, 0), (cm_w_out, 0), (ffn_w_gate, 1), (ffn_w_up, 1), (ffn_w_down, 1)])
    x = _lru_out(heavy, xp, xs, hp, hs, mods, norm_mix[0:1], norm_ffn[0:1], wi0, wo0, wg0, wu0,
                 wd0, 0)

    n_groups = cm_w_s.shape[1]
    gd = cm_w_out.shape[1] // n_groups
    bs_tile = jnp.repeat(cm_b_s[0].T, gd, axis=1)
    yp, ys = _cm_layer(heavy, x, mods, norm_mix[1:2], norm_ffn[1:2], final_norm[None],
                       wi1, cm_b_in[0:1], cm_ln_g[0:1], cm_ln_b[0:1],
                       cm_w_s[0].astype(BF16), bs_tile, wo1, wg1, wu1, wd1, 1)

    new_state = jnp.transpose(st, (1, 0, 2))[:, None]
    return (yp.reshape(bp, sp, d), ys.reshape(bs, ss, d), new_state)
```

```python
import functools

import jax
import jax.numpy as jnp
from jax import lax
from jax.experimental import pallas as pl
from jax.experimental.pallas import tpu as pltpu

F32 = jnp.float32
BF16 = jnp.bfloat16

EPS = 1e-6
LRU_C = 8.0
LOG2_E = 1.4426950408889634
N_MOD = 6
COND_ROWS = 16
LANES = 128
SUBLANES = 8
CHUNK = 128
TM_LIGHT = 1024
TM_HEAVY = 512
SUB = 256
GATE_ROWS = 2048
GATE_STEPS = 64
BIAS_PARTS = 3
BIAS_ROWS = 16
VMEM_LIMIT = 56 * 1024 * 1024


def _rmsnorm(x, g):
    y = x * lax.rsqrt(jnp.mean(x * x, axis=-1, keepdims=True) + EPS)
    return y * g


def _modulated_norm(x, g, shift, scale):
    return _rmsnorm(x, g) * (1.0 + scale) + shift


def _dot(a, b):
    return jnp.dot(a.astype(BF16), b, preferred_element_type=F32)


def _adaln_kernel(c_ref, w_ref, b_ref, o_ref):
    s = jax.nn.silu(c_ref[...])
    bias = b_ref[pl.ds(pl.program_id(0), 1), :]
    o_ref[...] = _dot(s, w_ref[...].astype(BF16)) + bias


def _adaln(cond, mod_w, mod_b):
    depth, d, n = mod_w.shape
    tn = 1024
    return pl.pallas_call(
        _adaln_kernel,
        grid=(depth, n // tn),
        in_specs=[
            pl.BlockSpec((COND_ROWS, d), lambda l, j: (0, 0)),
            pl.BlockSpec((None, d, tn), lambda l, j: (l, 0, j)),
            pl.BlockSpec((depth, tn), lambda l, j: (0, j)),
        ],
        out_specs=pl.BlockSpec((None, COND_ROWS, tn), lambda l, j: (l, 0, j)),
        out_shape=jax.ShapeDtypeStruct((depth, COND_ROWS, n), F32),
        compiler_params=pltpu.CompilerParams(
            dimension_semantics=("arbitrary", "arbitrary"), vmem_limit_bytes=VMEM_LIMIT),
        name="adaln",
    )(cond, mod_w, mod_b)


class _Tiles:
    def __init__(self, dims, tm):
        (self.bp, self.sp), (self.bs, self.ss) = dims
        assert tm % self.sp == 0 and self.ss % tm == 0 and (self.bp * self.sp) % tm == 0
        assert tm % SUB == 0
        self.tm = tm
        self.npt = self.bp * self.sp // tm
        self.nst = self.bs * self.ss // tm
        self.n = self.npt + self.nst

    def subs(self):
        return [slice(s0, s0 + SUB) for s0 in range(0, self.tm, SUB)]

    def is_prompt(self):
        return pl.program_id(0) < self.npt

    def combined(self, cols):
        return pl.BlockSpec((self.tm, cols), lambda i: (i, 0))

    def _pi(self, i):
        return jnp.minimum(i, self.npt - 1)

    def _si(self, i):
        return jnp.maximum(i - self.npt, 0)

    def flat_pair(self, cols):
        return [pl.BlockSpec((self.tm, cols), lambda i: (self._pi(i), 0)),
                pl.BlockSpec((self.tm, cols), lambda i: (self._si(i), 0))]

    def slab_pair(self, nj):
        ts = self.ss // self.tm
        return [pl.BlockSpec((nj, self.tm // self.sp, self.sp, LANES),
                             lambda i: (0, self._pi(i), 0, 0)),
                pl.BlockSpec((nj, None, self.tm, LANES),
                             lambda i: (0, self._si(i) // ts, self._si(i) % ts, 0))]

    def slab_out_pair(self, nj, pad):
        ts = self.ss // self.tm
        return [pl.BlockSpec((nj, self.tm // self.sp, self.sp + pad, LANES),
                             lambda i: (0, self._pi(i), 0, 0)),
                pl.BlockSpec((nj, None, self.ss + pad, LANES),
                             lambda i: (0, self._si(i) // ts, 0, 0))]

    def branch(self, prompt_fn, sample_fn):
        pl.when(self.is_prompt())(prompt_fn)
        pl.when(jnp.logical_not(self.is_prompt()))(sample_fn)

    def write_slabs_prompt(self, p_ref, value):
        nj, nb, rows, _ = p_ref.shape
        for j in range(nj):
            p_ref[j, :, :self.sp, :] = value[:, j * LANES:(j + 1) * LANES].reshape(nb, self.sp, LANES)
        p_ref[:, :, self.sp:, :] = jnp.zeros((nj, nb, rows - self.sp, LANES), value.dtype)

    def write_slabs_sample(self, s_ref, value):
        nj, rows, _ = s_ref.shape
        part = self._si(pl.program_id(0)) % (self.ss // self.tm)
        dst = pl.ds(pl.multiple_of(part * self.tm, self.tm), self.tm)
        for j in range(nj):
            s_ref[j, dst, :] = value[:, j * LANES:(j + 1) * LANES]
        s_ref[:, self.ss:, :] = jnp.zeros((nj, rows - self.ss, LANES), value.dtype)

    def mod(self, layer, mods):
        return pl.BlockSpec((None,) + mods.shape[1:], lambda i: (layer, 0, 0))

    def mods(self, mod_ref):
        i = pl.program_id(0)
        row = jnp.where(i < self.npt, 0, 1 + self._si(i) * self.tm // self.ss)
        m = mod_ref[pl.ds(row, 1), :]
        d = m.shape[1] // N_MOD
        return [m[:, k * d:(k + 1) * d] for k in range(N_MOD)]

    def read(self, p_ref, s_ref):
        return jnp.where(self.is_prompt(), p_ref[...].reshape(s_ref.shape), s_ref[...])


def _const_spec(shape):
    nd = len(shape)
    return pl.BlockSpec(shape, lambda i: (0,) * nd, pipeline_mode=pl.Buffered(1))


_TOKEN_PARAMS = pltpu.CompilerParams(
    dimension_semantics=("arbitrary",), vmem_limit_bytes=VMEM_LIMIT)


def _lru_in_kernel(tiles, xp_ref, xs_ref, mod_ref, g_ref, w_ref, xbp_ref, xbs_ref):
    def project(x_ref):
        m = tiles.mods(mod_ref)
        return jnp.concatenate(
            [_dot(_modulated_norm(x_ref[r, :], g_ref[...], m[0], m[1]), w_ref[...])
             for r in tiles.subs()], axis=0)

    tiles.branch(lambda: tiles.write_slabs_prompt(xbp_ref, project(xp_ref)),
                 lambda: tiles.write_slabs_sample(xbs_ref, project(xs_ref)))


def _lru_in(tiles, xp, xs, mods, g, w_x, layer, pad):
    d, w = w_x.shape
    return pl.pallas_call(
        functools.partial(_lru_in_kernel, tiles),
        grid=(tiles.n,),
        in_specs=tiles.flat_pair(d) + [tiles.mod(layer, mods), _const_spec((1, d)),
                                       _const_spec((d, w))],
        out_specs=tiles.slab_out_pair(w // LANES, pad),
        out_shape=[jax.ShapeDtypeStruct((w // LANES, tiles.bp, tiles.sp + pad, LANES), F32),
                   jax.ShapeDtypeStruct((w // LANES, tiles.bs, tiles.ss + pad, LANES), F32)],
        compiler_params=_TOKEN_PARAMS,
        name="lru_in",
    )(xp, xs, mods, g, w_x)


def _gate_rows(B):
    return max(GATE_ROWS, GATE_STEPS * B)


def _lru_rec_kernel(B, S, P, has_h0, n_cast, *refs):
    n_in = 6 + int(has_h0)
    x_ref, cw_ref, cb_ref, wg_ref, gb_ref, lam_ref = refs[:6]
    h0_ref = refs[6] if has_h0 else None
    cast_src = refs[n_in:n_in + n_cast]
    out_ref, st_ref = refs[n_in + n_cast:n_in + n_cast + 2]
    cast_dst = refs[n_in + n_cast + 2:n_in + 2 * n_cast + 2]
    xt, xhb, hfb, a_e, bx_e, a_o, bx_o, g_e, g_o = refs[n_in + 2 * n_cast + 2:]
    rows = B * S
    R = _gate_rows(B)
    n = rows // R
    steps = R // B
    assert n % 2 == 0 and n >= 4 and R % B == 0

    for src, dst in zip(cast_src, cast_dst):
        dst[...] = src[...].astype(BF16)

    for b in range(B):
        out_ref[pl.ds(b * P + S, P - S), :] = jnp.zeros((P - S, LANES), F32)

    xt[pl.ds(0, B), :] = jnp.zeros((B, LANES), F32)
    xt[pl.ds(rows + B, 2 * B), :] = jnp.zeros((2 * B, LANES), F32)

    def gather(t, _):
        xt[pl.ds(pl.multiple_of((t + 1) * B, B), B), :] = x_ref[pl.ds(t, B, stride=P), :]
        return 0
    lax.fori_loop(0, S, gather, 0, unroll=16)

    cw = 0.5 * cw_ref[...]
    cb = 0.5 * cb_ref[...]
    def gate_operand(k):
        pad = jnp.zeros((LANES - gb_ref.shape[1], LANES), BF16)
        return jnp.concatenate([wg_ref[k], gb_ref[k], pad], axis=0)
    wgz = [jnp.concatenate([gate_operand(2 * z), gate_operand(2 * z + 1)], axis=1)
           for z in range(2)]
    lane = lax.broadcasted_iota(jnp.int32, (R, LANES), 1)
    bias_cols = jnp.where(lane < BIAS_PARTS, 1.0, 0.0).astype(BF16)
    neg = -lam_ref[...]
    sp = jnp.maximum(neg, 0.0) + jnp.log1p(jnp.exp(-jnp.abs(neg)))
    c_pos = (0.5 * LRU_C) * sp
    c_exp2 = -LOG2_E * c_pos

    def preact(z, c, g_dst):
        r0 = pl.multiple_of(c * R, R)
        if z == 0:
            xh = (cw[0:1] * xt[pl.ds(r0, R), :]
                  + cw[1:2] * xt[pl.ds(r0 + B, R), :]
                  + cw[2:3] * xt[pl.ds(r0 + 2 * B, R), :]
                  + cw[3:4] * xt[pl.ds(r0 + 3 * B, R), :]) + cb
            xhb[pl.ds(r0, R), :] = xh
        else:
            xh = xhb[pl.ds(r0, R), :]
        lhs = jnp.concatenate([xh.astype(BF16), bias_cols], axis=1)
        g_dst[...] = jnp.dot(lhs, wgz[z], preferred_element_type=F32)

    def gates(z, c, g_src, a_dst, bx_dst):
        xh = xhb[pl.ds(pl.multiple_of(c * R, R), R), :]
        r2 = jnp.tanh(g_src[:, :LANES]) + 1.0
        i2 = jnp.tanh(g_src[:, LANES:]) + 1.0
        a = jnp.exp2(r2 * c_exp2[z:z + 1])
        om = jnp.tanh(r2 * c_pos[z:z + 1]) * (a * a + 1.0)
        mult = jnp.where(om > 0.0, om * lax.rsqrt(om), 0.0)
        a_dst[...] = a
        bx_dst[...] = mult * (i2 * xh)

    def scan_f(c, a_src, bx_src, h):
        for k in range(steps):
            h = a_src[k * B:(k + 1) * B, :] * h + bx_src[k * B:(k + 1) * B, :]
            hfb[pl.ds(pl.multiple_of(c * R + k * B, B), B), :] = h
        return h

    def scan_b(c, a_src, bx_src, h):
        for k in reversed(range(steps)):
            h = a_src[k * B:(k + 1) * B, :] * h + bx_src[k * B:(k + 1) * B, :]
            hf = hfb[pl.ds(pl.multiple_of(c * R + k * B, B), B), :]
            out_ref[pl.ds(c * steps + k, B, stride=P), :] = hf + h
        return h

    def run_pass(z, chunk, scan, h):
        even, odd = (g_e, a_e, bx_e), (g_o, a_o, bx_o)
        preact(z, chunk(0), g_e)
        gates(z, chunk(0), *even)
        preact(z, chunk(1), g_o)

        def body(k, h):
            i = 2 * k + 1
            preact(z, chunk(i + 1), g_e)
            gates(z, chunk(i), *odd)
            h = scan(chunk(i - 1), a_e, bx_e, h)
            preact(z, chunk(i + 2), g_o)
            gates(z, chunk(i + 1), *even)
            return scan(chunk(i), a_o, bx_o, h)
        h = lax.fori_loop(0, n // 2 - 1, body, h)
        gates(z, chunk(n - 1), *odd)
        h = scan(chunk(n - 2), a_e, bx_e, h)
        return scan(chunk(n - 1), a_o, bx_o, h)

    if has_h0:
        h0f, h0b = h0_ref[0], h0_ref[1]
    else:
        h0f = h0b = jnp.zeros((B, LANES), F32)
    st_ref[0] = run_pass(0, lambda i: i, scan_f, h0f)
    st_ref[1] = run_pass(1, lambda i: n - 1 - i, scan_b, h0b)


def _lru_rec(xb, conv_w, conv_b, wg, gb, lam, h0, casts):
    nj, B, P, _ = xb.shape
    S = P - SUBLANES
    rows = B * S
    w = nj * LANES
    has_h0 = h0 is not None
    slab = pl.BlockSpec((None, B * P, LANES), lambda j: (j, 0, 0))
    state = pl.BlockSpec((2, B, LANES), lambda j: (0, 0, j))
    in_specs = [
        slab,
        pl.BlockSpec((4, LANES), lambda j: (0, j)),
        pl.BlockSpec((1, LANES), lambda j: (0, j)),
        pl.BlockSpec((4, None, LANES, LANES), lambda j: (0, j, 0, 0)),
        pl.BlockSpec((4, None, BIAS_ROWS, LANES), lambda j: (0, j, 0, 0)),
        pl.BlockSpec((2, LANES), lambda j: (0, j)),
    ]
    args = [xb.reshape(nj, B * P, LANES), conv_w, conv_b, wg, gb, lam]
    if has_h0:
        in_specs.append(state)
        args.append(h0)
    out_specs = [slab, state]
    out_shape = [jax.ShapeDtypeStruct((nj, B * P, LANES), F32),
                 jax.ShapeDtypeStruct((2, B, w), F32)]
    for wt, layer in casts:
        _, r, c = wt.shape
        assert r % (nj * 2 * SUBLANES) == 0
        in_specs.append(pl.BlockSpec((None, r // nj, c), functools.partial(
            lambda j, layer: (layer, j, 0), layer=layer)))
        args.append(wt)
        out_specs.append(pl.BlockSpec((r // nj, c), lambda j: (j, 0)))
        out_shape.append(jax.ShapeDtypeStruct((r, c), BF16))
    res = pl.pallas_call(
        functools.partial(_lru_rec_kernel, B, S, P, has_h0, len(casts)),
        grid=(nj,),
        in_specs=in_specs,
        out_specs=out_specs,
        out_shape=out_shape,
        scratch_shapes=[pltpu.VMEM((rows + 3 * B, LANES), F32)]
        + [pltpu.VMEM((rows, LANES), F32)] * 2 + [pltpu.VMEM((_gate_rows(B), LANES), F32)] * 4
        + [pltpu.VMEM((_gate_rows(B), 2 * LANES), F32)] * 2,
        compiler_params=pltpu.CompilerParams(
            dimension_semantics=("arbitrary",), vmem_limit_bytes=VMEM_LIMIT),
        name="lru_rec",
    )(*args)
    return res[0].reshape(nj, B, P, LANES), res[1], res[2:]


def _ffn(x1, m, g, wg_ref, wu_ref, wd_ref):
    h = [_modulated_norm(v, g, m[3], m[4]).astype(BF16) for v in x1]
    gu = [(jnp.dot(v, wg_ref[...], preferred_element_type=F32),
           jnp.dot(v, wu_ref[...], preferred_element_type=F32)) for v in h]
    dn = [_dot(jax.nn.silu(gate) * up, wd_ref[...]) for gate, up in gu]
    return [v + m[5] * d for v, d in zip(x1, dn)]


def _lru_out_kernel(tiles, xp_ref, xs_ref, hp_ref, hs_ref, mod_ref, gm_ref, gf_ref,
                    wy_ref, wo_ref, wg_ref, wu_ref, wd_ref, o_ref):
    subs = tiles.subs()

    def body(x_ref, hsum):
        m = tiles.mods(mod_ref)
        x = [x_ref[r, :] for r in subs]
        y = [jax.nn.gelu(_dot(_modulated_norm(v, gm_ref[...], m[0], m[1]), wy_ref[...]))
             for v in x]
        x1 = [v + m[2] * _dot(h * g, wo_ref[...]) for v, h, g in zip(x, hsum, y)]
        for r, v in zip(subs, _ffn(x1, m, gf_ref[...], wg_ref, wu_ref, wd_ref)):
            o_ref[r, :] = v

    nj = hp_ref.shape[0]

    def prompt_hsum(r):
        return jnp.concatenate(
            [hp_ref[j].reshape(tiles.tm, LANES)[r] for j in range(nj)], axis=1)

    def sample_hsum(r):
        return jnp.concatenate([hs_ref[j, r, :] for j in range(nj)], axis=1)

    tiles.branch(lambda: body(xp_ref, [prompt_hsum(r) for r in subs]),
                 lambda: body(xs_ref, [sample_hsum(r) for r in subs]))


def _lru_out(tiles, xp, xs, hp, hs, mods, g_mix, g_ffn, w_in, w_out, w_gate, w_up, w_down, layer):
    d = xp.shape[1]
    w = w_out.shape[0]
    f = w_gate.shape[1]
    w_y = pl.BlockSpec((d, w), lambda i: (0, 1), pipeline_mode=pl.Buffered(1))
    return pl.pallas_call(
        functools.partial(_lru_out_kernel, tiles),
        grid=(tiles.n,),
        in_specs=tiles.flat_pair(d) + tiles.slab_pair(w // LANES) + [
            tiles.mod(layer, mods), _const_spec((1, d)), _const_spec((1, d)), w_y,
            _const_spec((w, d)), _const_spec((d, f)), _const_spec((d, f)), _const_spec((f, d))],
        out_specs=tiles.combined(d),
        out_shape=jax.ShapeDtypeStruct((tiles.n * tiles.tm, d), F32),
        compiler_params=_TOKEN_PARAMS,
        name="lru_out_ffn",
    )(xp, xs, hp, hs, mods, g_mix, g_ffn, w_in, w_out, w_gate, w_up, w_down)


def _cm_kernel(tiles, x_ref, mod_ref, gm_ref, gf_ref, gfin_ref, wi_ref, bi_ref, lng_ref, lnb_ref,
               ws_ref, bs_ref, wo_ref, wg_ref, wu_ref, wd_ref, op_ref, os_ref):
    cw = lng_ref.shape[-1]
    n_groups = ws_ref.shape[0]
    gd = cw // n_groups

    def body():
        m = tiles.mods(mod_ref)
        xs = [x_ref[r, :] for r in tiles.subs()]
        pre = [_dot(_modulated_norm(x, gm_ref[...], m[0], m[1]), wi_ref[...]) + bi_ref[...]
               for x in xs]
        x1 = []
        for x, z in zip(xs, pre):
            uv = jax.nn.gelu(z)
            u = uv[:, :cw]
            v = uv[:, cw:]
            mu = jnp.mean(v, axis=-1, keepdims=True)
            var = jnp.mean(jnp.square(v - mu), axis=-1, keepdims=True)
            v = ((v - mu) * lax.rsqrt(var + EPS) * lng_ref[...] + lnb_ref[...]).astype(BF16)
            mixed_rows = []
            for c in range(SUB // CHUNK):
                vc = v[c * CHUNK:(c + 1) * CHUNK]
                cols = [jnp.dot(ws_ref[g], vc[:, g * gd:(g + 1) * gd],
                                preferred_element_type=F32) for g in range(n_groups)]
                mixed_rows.append(jnp.concatenate(cols, axis=1) + bs_ref[...])
            mixed = jnp.concatenate(mixed_rows, axis=0)
            x1.append(x + m[2] * _dot(u * mixed, wo_ref[...]))
        x2 = _ffn(x1, m, gf_ref[...], wg_ref, wu_ref, wd_ref)
        return jnp.concatenate([_rmsnorm(v, gfin_ref[...]) for v in x2], axis=0)

    out = body()

    def put(o_ref):
        o_ref[...] = out
    tiles.branch(lambda: put(op_ref), lambda: put(os_ref))


def _cm_layer(tiles, x, mods, g_mix, g_ffn, g_final, w_in, b_in, ln_g, ln_b, w_s, b_s, w_out,
              w_gate, w_up, w_down, layer):
    d = x.shape[1]
    cw = w_out.shape[0]
    f = w_gate.shape[1]
    return pl.pallas_call(
        functools.partial(_cm_kernel, tiles),
        grid=(tiles.n,),
        in_specs=[tiles.combined(d), tiles.mod(layer, mods),
                  _const_spec((1, d)), _const_spec((1, d)), _const_spec((1, d)),
                  _const_spec((d, 2 * cw)), _const_spec((1, 2 * cw)),
                  _const_spec((1, cw)), _const_spec((1, cw)),
                  _const_spec(w_s.shape), _const_spec(b_s.shape), _const_spec((cw, d)),
                  _const_spec((d, f)), _const_spec((d, f)), _const_spec((f, d))],
        out_specs=tiles.flat_pair(d),
        out_shape=[jax.ShapeDtypeStruct((tiles.npt * tiles.tm, d), F32),
                   jax.ShapeDtypeStruct((tiles.nst * tiles.tm, d), F32)],
        compiler_params=_TOKEN_PARAMS,
        name="cm_layer",
    )(x, mods, g_mix, g_ffn, g_final, w_in, b_in, ln_g, ln_b, w_s, b_s, w_out,
      w_gate, w_up, w_down)


def _gate_weights(ga_w, gx_w, ga_b, gx_b):
    _, heads, hd, _ = ga_w.shape
    per = LANES // hd
    nj = heads // per
    w4 = jnp.stack([ga_w[0], gx_w[0], ga_w[1], gx_w[1]]).reshape(4, nj, per, hd, hd)
    rows = []
    for h in range(per):
        blocks = [w4[:, :, h] if g == h else jnp.zeros_like(w4[:, :, h]) for g in range(per)]
        rows.append(jnp.concatenate(blocks, axis=-1))
    wg = jnp.concatenate(rows, axis=-2).astype(BF16)
    rest = 0.5 * jnp.stack([ga_b[0], gx_b[0], ga_b[1], gx_b[1]]).reshape(4, nj, 1, LANES)
    parts = []
    for _ in range(BIAS_PARTS):
        scaled = rest * (2.0 ** 16 + 1.0)
        part = scaled - (scaled - rest)
        parts.append(part)
        rest = rest - part
    parts.append(jnp.zeros((4, nj, BIAS_ROWS - BIAS_PARTS, LANES), F32))
    return wg, jnp.concatenate(parts, axis=-2).astype(BF16)


def kernel(x_prompt, x_sample, state_lru, c, c_ctx, mod_w, mod_b, norm_mix, norm_ffn, lru_w_in, lru_conv_w, lru_conv_b, lru_ga_w, lru_ga_b, lru_gx_w, lru_gx_b, lru_lambda, lru_w_out, cm_w_in, cm_b_in, cm_ln_g, cm_ln_b, cm_w_s, cm_b_s, cm_w_out, ffn_w_gate, ffn_w_up, ffn_w_down, final_norm):
    bp, sp, d = x_prompt.shape
    bs, ss, _ = x_sample.shape
    dims = ((bp, sp), (bs, ss))
    light, heavy = _Tiles(dims, TM_LIGHT), _Tiles(dims, TM_HEAVY)
    xp = x_prompt.reshape(bp * sp, d)
    xs = x_sample.reshape(bs * ss, d)

    cond = jnp.concatenate([c_ctx[None], c, jnp.zeros((COND_ROWS - 1 - bs, d), F32)], 0)
    mods = _adaln(cond, mod_w, mod_b)

    w = lru_w_out.shape[1]
    xbp, xbs = _lru_in(light, xp, xs, mods, norm_mix[0:1], lru_w_in[0, :, :w].astype(BF16), 0,
                       SUBLANES)
    wg, gb = _gate_weights(lru_ga_w[0], lru_gx_w[0], lru_ga_b[0], lru_gx_b[0])
    rec = (lru_conv_w[0], lru_conv_b[0:1], wg, gb, lru_lambda[0])
    hp, st, (wi0, wo0, wg0, wu0, wd0) = _lru_rec(
        xbp, *rec, None,
        [(lru_w_in, 0), (lru_w_out, 0), (ffn_w_gate, 0), (ffn_w_up, 0), (ffn_w_down, 0)])
    hs, _, (wi1, wo1, wg1, wu1, wd1) = _lru_rec(
        xbs, *rec, jnp.transpose(state_lru[:, 0], (1, 0, 2)),
        [(cm_w_in, 0), (cm_w_out, 0), (ffn_w_gate, 1), (ffn_w_up, 1), (ffn_w_down, 1)])
    x = _lru_out(heavy, xp, xs, hp, hs, mods, norm_mix[0:1], norm_ffn[0:1], wi0, wo0, wg0, wu0,
                 wd0, 0)

    n_groups = cm_w_s.shape[1]
    gd = cm_w_out.shape[1] // n_groups
    bs_tile = jnp.repeat(cm_b_s[0].T, gd, axis=1)
    yp, ys = _cm_layer(heavy, x, mods, norm_mix[1:2], norm_ffn[1:2], final_norm[None],
                       wi1, cm_b_in[0:1], cm_ln_g[0:1], cm_ln_b[0:1],
                       cm_w_s[0].astype(BF16), bs_tile, wo1, wg1, wu1, wd1, 1)

    new_state = jnp.transpose(st, (1, 0, 2))[:, None]
    return (yp.reshape(bp, sp, d), ys.reshape(bs, ss, d), new_state)
```

```python
import functools

import jax
import jax.numpy as jnp
from jax import lax
from jax.experimental import pallas as pl
from jax.experimental.pallas import tpu as pltpu

F32 = jnp.float32
BF16 = jnp.bfloat16

EPS = 1e-6
LRU_C = 8.0
LOG2_E = 1.4426950408889634
N_MOD = 6
COND_ROWS = 16
LANES = 128
SUBLANES = 8
CHUNK = 128
TM_LIGHT = 1024
TM_HEAVY = 512
SUB = 256
GATE_ROWS = 2048
GATE_STEPS = 64
BIAS_PARTS = 3
BIAS_ROWS = 16
VMEM_LIMIT = 56 * 1024 * 1024


def _rmsnorm(x, g):
    y = x * lax.rsqrt(jnp.mean(x * x, axis=-1, keepdims=True) + EPS)
    return y * g


def _modulated_norm(x, g, shift, scale):
    return _rmsnorm(x, g) * (1.0 + scale) + shift


def _dot(a, b):
    return jnp.dot(a.astype(BF16), b, preferred_element_type=F32)


def _adaln_kernel(c_ref, w_ref, b_ref, o_ref):
    s = jax.nn.silu(c_ref[...])
    bias = b_ref[pl.ds(pl.program_id(0), 1), :]
    o_ref[...] = _dot(s, w_ref[...].astype(BF16)) + bias


def _adaln(cond, mod_w, mod_b):
    depth, d, n = mod_w.shape
    tn = 2048
    return pl.pallas_call(
        _adaln_kernel,
        grid=(depth, n // tn),
        in_specs=[
            pl.BlockSpec((COND_ROWS, d), lambda l, j: (0, 0)),
            pl.BlockSpec((None, d, tn), lambda l, j: (l, 0, j)),
            pl.BlockSpec((depth, tn), lambda l, j: (0, j)),
        ],
        out_specs=pl.BlockSpec((None, COND_ROWS, tn), lambda l, j: (l, 0, j)),
        out_shape=jax.ShapeDtypeStruct((depth, COND_ROWS, n), F32),
        compiler_params=pltpu.CompilerParams(
            dimension_semantics=("arbitrary", "arbitrary"), vmem_limit_bytes=VMEM_LIMIT),
        name="adaln",
    )(cond, mod_w, mod_b)


class _Tiles:
    def __init__(self, dims, tm):
        (self.bp, self.sp), (self.bs, self.ss) = dims
        assert tm % self.sp == 0 and self.ss % tm == 0 and (self.bp * self.sp) % tm == 0
        assert tm % SUB == 0
        self.tm = tm
        self.npt = self.bp * self.sp // tm
        self.nst = self.bs * self.ss // tm
        self.n = self.npt + self.nst

    def subs(self):
        return [slice(s0, s0 + SUB) for s0 in range(0, self.tm, SUB)]

    def is_prompt(self):
        return pl.program_id(0) < self.npt

    def combined(self, cols):
        return pl.BlockSpec((self.tm, cols), lambda i: (i, 0))

    def _pi(self, i):
        return jnp.minimum(i, self.npt - 1)

    def _si(self, i):
        return jnp.maximum(i - self.npt, 0)

    def flat_pair(self, cols):
        return [pl.BlockSpec((self.tm, cols), lambda i: (self._pi(i), 0)),
                pl.BlockSpec((self.tm, cols), lambda i: (self._si(i), 0))]

    def slab_pair(self, nj):
        ts = self.ss // self.tm
        return [pl.BlockSpec((nj, self.tm // self.sp, self.sp, LANES),
                             lambda i: (0, self._pi(i), 0, 0)),
                pl.BlockSpec((nj, None, self.tm, LANES),
                             lambda i: (0, self._si(i) // ts, self._si(i) % ts, 0))]

    def slab_out_pair(self, nj, pad):
        ts = self.ss // self.tm
        return [pl.BlockSpec((nj, self.tm // self.sp, self.sp + pad, LANES),
                             lambda i: (0, self._pi(i), 0, 0)),
                pl.BlockSpec((nj, None, self.ss + pad, LANES),
                             lambda i: (0, self._si(i) // ts, 0, 0))]

    def branch(self, prompt_fn, sample_fn):
        pl.when(self.is_prompt())(prompt_fn)
        pl.when(jnp.logical_not(self.is_prompt()))(sample_fn)

    def write_slabs_prompt(self, p_ref, value):
        nj, nb, rows, _ = p_ref.shape
        for j in range(nj):
            p_ref[j, :, :self.sp, :] = value[:, j * LANES:(j + 1) * LANES].reshape(nb, self.sp, LANES)
        p_ref[:, :, self.sp:, :] = jnp.zeros((nj, nb, rows - self.sp, LANES), value.dtype)

    def write_slabs_sample(self, s_ref, value):
        nj, rows, _ = s_ref.shape
        part = self._si(pl.program_id(0)) % (self.ss // self.tm)
        dst = pl.ds(pl.multiple_of(part * self.tm, self.tm), self.tm)
        for j in range(nj):
            s_ref[j, dst, :] = value[:, j * LANES:(j + 1) * LANES]
        s_ref[:, self.ss:, :] = jnp.zeros((nj, rows - self.ss, LANES), value.dtype)

    def mod(self, layer, mods):
        return pl.BlockSpec((None,) + mods.shape[1:], lambda i: (layer, 0, 0))

    def mods(self, mod_ref):
        i = pl.program_id(0)
        row = jnp.where(i < self.npt, 0, 1 + self._si(i) * self.tm // self.ss)
        m = mod_ref[pl.ds(row, 1), :]
        d = m.shape[1] // N_MOD
        return [m[:, k * d:(k + 1) * d] for k in range(N_MOD)]

    def read(self, p_ref, s_ref):
        return jnp.where(self.is_prompt(), p_ref[...].reshape(s_ref.shape), s_ref[...])


def _const_spec(shape):
    nd = len(shape)
    return pl.BlockSpec(shape, lambda i: (0,) * nd, pipeline_mode=pl.Buffered(1))


_TOKEN_PARAMS = pltpu.CompilerParams(
    dimension_semantics=("arbitrary",), vmem_limit_bytes=VMEM_LIMIT)


def _lru_in_kernel(tiles, xp_ref, xs_ref, mod_ref, g_ref, w_ref, xbp_ref, xbs_ref):
    def project(x_ref):
        m = tiles.mods(mod_ref)
        return jnp.concatenate(
            [_dot(_modulated_norm(x_ref[r, :], g_ref[...], m[0], m[1]), w_ref[...])
             for r in tiles.subs()], axis=0)

    tiles.branch(lambda: tiles.write_slabs_prompt(xbp_ref, project(xp_ref)),
                 lambda: tiles.write_slabs_sample(xbs_ref, project(xs_ref)))


def _lru_in(tiles, xp, xs, mods, g, w_x, layer, pad):
    d, w = w_x.shape
    return pl.pallas_call(
        functools.partial(_lru_in_kernel, tiles),
        grid=(tiles.n,),
        in_specs=tiles.flat_pair(d) + [tiles.mod(layer, mods), _const_spec((1, d)),
                                       _const_spec((d, w))],
        out_specs=tiles.slab_out_pair(w // LANES, pad),
        out_shape=[jax.ShapeDtypeStruct((w // LANES, tiles.bp, tiles.sp + pad, LANES), F32),
                   jax.ShapeDtypeStruct((w // LANES, tiles.bs, tiles.ss + pad, LANES), F32)],
        compiler_params=_TOKEN_PARAMS,
        name="lru_in",
    )(xp, xs, mods, g, w_x)


def _gate_rows(B):
    return max(GATE_ROWS, GATE_STEPS * B)


def _lru_rec_kernel(B, S, P, has_h0, n_cast, *refs):
    n_in = 6 + int(has_h0)
    x_ref, cw_ref, cb_ref, wg_ref, gb_ref, lam_ref = refs[:6]
    h0_ref = refs[6] if has_h0 else None
    cast_src = refs[n_in:n_in + n_cast]
    out_ref, st_ref = refs[n_in + n_cast:n_in + n_cast + 2]
    cast_dst = refs[n_in + n_cast + 2:n_in + 2 * n_cast + 2]
    xt, xhb, hfb, a_e, bx_e, a_o, bx_o, g_e, g_o = refs[n_in + 2 * n_cast + 2:]
    rows = B * S
    R = _gate_rows(B)
    n = rows // R
    steps = R // B
    assert n % 2 == 0 and n >= 4 and R % B == 0

    for src, dst in zip(cast_src, cast_dst):
        dst[...] = src[...].astype(BF16)

    for b in range(B):
        out_ref[pl.ds(b * P + S, P - S), :] = jnp.zeros((P - S, LANES), F32)

    xt[pl.ds(0, B), :] = jnp.zeros((B, LANES), F32)
    xt[pl.ds(rows + B, 2 * B), :] = jnp.zeros((2 * B, LANES), F32)

    def gather(t, _):
        xt[pl.ds(pl.multiple_of((t + 1) * B, B), B), :] = x_ref[pl.ds(t, B, stride=P), :]
        return 0
    lax.fori_loop(0, S, gather, 0, unroll=64)

    cw = 0.5 * cw_ref[...]
    cb = 0.5 * cb_ref[...]
    def gate_operand(k):
        pad = jnp.zeros((LANES - gb_ref.shape[1], LANES), BF16)
        return jnp.concatenate([wg_ref[k], gb_ref[k], pad], axis=0)
    wgz = [jnp.concatenate([gate_operand(2 * z), gate_operand(2 * z + 1)], axis=1)
           for z in range(2)]
    lane = lax.broadcasted_iota(jnp.int32, (R, LANES), 1)
    bias_cols = jnp.where(lane < BIAS_PARTS, 1.0, 0.0).astype(BF16)
    neg = -lam_ref[...]
    sp = jnp.maximum(neg, 0.0) + jnp.log1p(jnp.exp(-jnp.abs(neg)))
    c_pos = (0.5 * LRU_C) * sp
    c_exp2 = -LOG2_E * c_pos

    def preact(z, c, g_dst):
        r0 = pl.multiple_of(c * R, R)
        if z == 0:
            xh = (cw[0:1] * xt[pl.ds(r0, R), :]
                  + cw[1:2] * xt[pl.ds(r0 + B, R), :]
                  + cw[2:3] * xt[pl.ds(r0 + 2 * B, R), :]
                  + cw[3:4] * xt[pl.ds(r0 + 3 * B, R), :]) + cb
            xhb[pl.ds(r0, R), :] = xh
        else:
            xh = xhb[pl.ds(r0, R), :]
        lhs = jnp.concatenate([xh.astype(BF16), bias_cols], axis=1)
        g_dst[...] = jnp.dot(lhs, wgz[z], preferred_element_type=F32)

    def gates(z, c, g_src, a_dst, bx_dst):
        xh = xhb[pl.ds(pl.multiple_of(c * R, R), R), :]
        r2 = jnp.tanh(g_src[:, :LANES]) + 1.0
        i2 = jnp.tanh(g_src[:, LANES:]) + 1.0
        a = jnp.exp2(r2 * c_exp2[z:z + 1])
        om = jnp.tanh(r2 * c_pos[z:z + 1]) * (a * a + 1.0)
        mult = jnp.where(om > 0.0, om * lax.rsqrt(om), 0.0)
        a_dst[...] = a
        bx_dst[...] = mult * (i2 * xh)

    def scan_f(c, a_src, bx_src, h):
        for k in range(steps):
            h = a_src[k * B:(k + 1) * B, :] * h + bx_src[k * B:(k + 1) * B, :]
            hfb[pl.ds(pl.multiple_of(c * R + k * B, B), B), :] = h
        return h

    def scan_b(c, a_src, bx_src, h):
        for k in reversed(range(steps)):
            h = a_src[k * B:(k + 1) * B, :] * h + bx_src[k * B:(k + 1) * B, :]
            hf = hfb[pl.ds(pl.multiple_of(c * R + k * B, B), B), :]
            out_ref[pl.ds(c * steps + k, B, stride=P), :] = hf + h
        return h

    def run_pass(z, chunk, scan, h):
        even, odd = (g_e, a_e, bx_e), (g_o, a_o, bx_o)
        preact(z, chunk(0), g_e)
        gates(z, chunk(0), *even)
        preact(z, chunk(1), g_o)

        def body(k, h):
            i = 2 * k + 1
            preact(z, chunk(i + 1), g_e)
            gates(z, chunk(i), *odd)
            h = scan(chunk(i - 1), a_e, bx_e, h)
            preact(z, chunk(i + 2), g_o)
            gates(z, chunk(i + 1), *even)
            return scan(chunk(i), a_o, bx_o, h)
        h = lax.fori_loop(0, n // 2 - 1, body, h)
        gates(z, chunk(n - 1), *odd)
        h = scan(chunk(n - 2), a_e, bx_e, h)
        return scan(chunk(n - 1), a_o, bx_o, h)

    if has_h0:
        h0f, h0b = h0_ref[0], h0_ref[1]
    else:
        h0f = h0b = jnp.zeros((B, LANES), F32)
    st_ref[0] = run_pass(0, lambda i: i, scan_f, h0f)
    st_ref[1] = run_pass(1, lambda i: n - 1 - i, scan_b, h0b)


def _lru_rec(xb, conv_w, conv_b, wg, gb, lam, h0, casts):
    nj, B, P, _ = xb.shape
    S = P - SUBLANES
    rows = B * S
    w = nj * LANES
    has_h0 = h0 is not None
    slab = pl.BlockSpec((None, B * P, LANES), lambda j: (j, 0, 0))
    state = pl.BlockSpec((2, B, LANES), lambda j: (0, 0, j))
    in_specs = [
        slab,
        pl.BlockSpec((4, LANES), lambda j: (0, j)),
        pl.BlockSpec((1, LANES), lambda j: (0, j)),
        pl.BlockSpec((4, None, LANES, LANES), lambda j: (0, j, 0, 0)),
        pl.BlockSpec((4, None, BIAS_ROWS, LANES), lambda j: (0, j, 0, 0)),
        pl.BlockSpec((2, LANES), lambda j: (0, j)),
    ]
    args = [xb.reshape(nj, B * P, LANES), conv_w, conv_b, wg, gb, lam]
    if has_h0:
        in_specs.append(state)
        args.append(h0)
    out_specs = [slab, state]
    out_shape = [jax.ShapeDtypeStruct((nj, B * P, LANES), F32),
                 jax.ShapeDtypeStruct((2, B, w), F32)]
    for wt, layer in casts:
        _, r, c = wt.shape
        assert r % (nj * 2 * SUBLANES) == 0
        in_specs.append(pl.BlockSpec((None, r // nj, c), functools.partial(
            lambda j, layer: (layer, j, 0), layer=layer)))
        args.append(wt)
        out_specs.append(pl.BlockSpec((r // nj, c), lambda j: (j, 0)))
        out_shape.append(jax.ShapeDtypeStruct((r, c), BF16))
    res = pl.pallas_call(
        functools.partial(_lru_rec_kernel, B, S, P, has_h0, len(casts)),
        grid=(nj,),
        in_specs=in_specs,
        out_specs=out_specs,
        out_shape=out_shape,
        scratch_shapes=[pltpu.VMEM((rows + 3 * B, LANES), F32)]
        + [pltpu.VMEM((rows, LANES), F32)] * 2 + [pltpu.VMEM((_gate_rows(B), LANES), F32)] * 4
        + [pltpu.VMEM((_gate_rows(B), 2 * LANES), F32)] * 2,
        compiler_params=pltpu.CompilerParams(
            dimension_semantics=("arbitrary",), vmem_limit_bytes=VMEM_LIMIT),
        name="lru_rec",
    )(*args)
    return res[0].reshape(nj, B, P, LANES), res[1], res[2:]


def _ffn(x1, m, g, wg_ref, wu_ref, wd_ref):
    h = [_modulated_norm(v, g, m[3], m[4]).astype(BF16) for v in x1]
    gu = [(jnp.dot(v, wg_ref[...], preferred_element_type=F32),
           jnp.dot(v, wu_ref[...], preferred_element_type=F32)) for v in h]
    dn = [_dot(jax.nn.silu(gate) * up, wd_ref[...]) for gate, up in gu]
    return [v + m[5] * d for v, d in zip(x1, dn)]


def _lru_out_kernel(tiles, xp_ref, xs_ref, hp_ref, hs_ref, mod_ref, gm_ref, gf_ref,
                    wy_ref, wo_ref, wg_ref, wu_ref, wd_ref, o_ref):
    subs = tiles.subs()

    def body(x_ref, hsum):
        m = tiles.mods(mod_ref)
        x = [x_ref[r, :] for r in subs]
        y = [jax.nn.gelu(_dot(_modulated_norm(v, gm_ref[...], m[0], m[1]), wy_ref[...]))
             for v in x]
        x1 = [v + m[2] * _dot(h * g, wo_ref[...]) for v, h, g in zip(x, hsum, y)]
        for r, v in zip(subs, _ffn(x1, m, gf_ref[...], wg_ref, wu_ref, wd_ref)):
            o_ref[r, :] = v

    nj = hp_ref.shape[0]

    def prompt_hsum(r):
        return jnp.concatenate(
            [hp_ref[j].reshape(tiles.tm, LANES)[r] for j in range(nj)], axis=1)

    def sample_hsum(r):
        return jnp.concatenate([hs_ref[j, r, :] for j in range(nj)], axis=1)

    tiles.branch(lambda: body(xp_ref, [prompt_hsum(r) for r in subs]),
                 lambda: body(xs_ref, [sample_hsum(r) for r in subs]))


def _lru_out(tiles, xp, xs, hp, hs, mods, g_mix, g_ffn, w_in, w_out, w_gate, w_up, w_down, layer):
    d = xp.shape[1]
    w = w_out.shape[0]
    f = w_gate.shape[1]
    w_y = pl.BlockSpec((d, w), lambda i: (0, 1), pipeline_mode=pl.Buffered(1))
    return pl.pallas_call(
        functools.partial(_lru_out_kernel, tiles),
        grid=(tiles.n,),
        in_specs=tiles.flat_pair(d) + tiles.slab_pair(w // LANES) + [
            tiles.mod(layer, mods), _const_spec((1, d)), _const_spec((1, d)), w_y,
            _const_spec((w, d)), _const_spec((d, f)), _const_spec((d, f)), _const_spec((f, d))],
        out_specs=tiles.combined(d),
        out_shape=jax.ShapeDtypeStruct((tiles.n * tiles.tm, d), F32),
        compiler_params=_TOKEN_PARAMS,
        name="lru_out_ffn",
    )(xp, xs, hp, hs, mods, g_mix, g_ffn, w_in, w_out, w_gate, w_up, w_down)


def _cm_kernel(tiles, x_ref, mod_ref, gm_ref, gf_ref, gfin_ref, wi_ref, bi_ref, lng_ref, lnb_ref,
               ws_ref, bs_ref, wo_ref, wg_ref, wu_ref, wd_ref, op_ref, os_ref):
    cw = lng_ref.shape[-1]
    n_groups = ws_ref.shape[0]
    gd = cw // n_groups

    def body():
        m = tiles.mods(mod_ref)
        xs = [x_ref[r, :] for r in tiles.subs()]
        pre = [_dot(_modulated_norm(x, gm_ref[...], m[0], m[1]), wi_ref[...]) + bi_ref[...]
               for x in xs]
        x1 = []
        for x, z in zip(xs, pre):
            uv = jax.nn.gelu(z)
            u = uv[:, :cw]
            v = uv[:, cw:]
            mu = jnp.mean(v, axis=-1, keepdims=True)
            var = jnp.mean(jnp.square(v - mu), axis=-1, keepdims=True)
            v = ((v - mu) * lax.rsqrt(var + EPS) * lng_ref[...] + lnb_ref[...]).astype(BF16)
            mixed_rows = []
            for c in range(SUB // CHUNK):
                vc = v[c * CHUNK:(c + 1) * CHUNK]
                cols = [jnp.dot(ws_ref[g], vc[:, g * gd:(g + 1) * gd],
                                preferred_element_type=F32) for g in range(n_groups)]
                mixed_rows.append(jnp.concatenate(cols, axis=1) + bs_ref[...])
            mixed = jnp.concatenate(mixed_rows, axis=0)
            x1.append(x + m[2] * _dot(u * mixed, wo_ref[...]))
        x2 = _ffn(x1, m, gf_ref[...], wg_ref, wu_ref, wd_ref)
        return jnp.concatenate([_rmsnorm(v, gfin_ref[...]) for v in x2], axis=0)

    out = body()

    def put(o_ref):
        o_ref[...] = out
    tiles.branch(lambda: put(op_ref), lambda: put(os_ref))


def _cm_layer(tiles, x, mods, g_mix, g_ffn, g_final, w_in, b_in, ln_g, ln_b, w_s, b_s, w_out,
              w_gate, w_up, w_down, layer):
    d = x.shape[1]
    cw = w_out.shape[0]
    f = w_gate.shape[1]
    return pl.pallas_call(
        functools.partial(_cm_kernel, tiles),
        grid=(tiles.n,),
        in_specs=[tiles.combined(d), tiles.mod(layer, mods),
                  _const_spec((1, d)), _const_spec((1, d)), _const_spec((1, d)),
                  _const_spec((d, 2 * cw)), _const_spec((1, 2 * cw)),
                  _const_spec((1, cw)), _const_spec((1, cw)),
                  _const_spec(w_s.shape), _const_spec(b_s.shape), _const_spec((cw, d)),
                  _const_spec((d, f)), _const_spec((d, f)), _const_spec((f, d))],
        out_specs=tiles.flat_pair(d),
        out_shape=[jax.ShapeDtypeStruct((tiles.npt * tiles.tm, d), F32),
                   jax.ShapeDtypeStruct((tiles.nst * tiles.tm, d), F32)],
        compiler_params=_TOKEN_PARAMS,
        name="cm_layer",
    )(x, mods, g_mix, g_ffn, g_final, w_in, b_in, ln_g, ln_b, w_s, b_s, w_out,
      w_gate, w_up, w_down)


def _gate_weights(ga_w, gx_w, ga_b, gx_b):
    _, heads, hd, _ = ga_w.shape
    per = LANES // hd
    nj = heads // per
    w4 = jnp.stack([ga_w[0], gx_w[0], ga_w[1], gx_w[1]]).reshape(4, nj, per, hd, hd)
    rows = []
    for h in range(per):
        blocks = [w4[:, :, h] if g == h else jnp.zeros_like(w4[:, :, h]) for g in range(per)]
        rows.append(jnp.concatenate(blocks, axis=-1))
    wg = jnp.concatenate(rows, axis=-2).astype(BF16)
    rest = 0.5 * jnp.stack([ga_b[0], gx_b[0], ga_b[1], gx_b[1]]).reshape(4, nj, 1, LANES)
    parts = []
    for _ in range(BIAS_PARTS):
        scaled = rest * (2.0 ** 16 + 1.0)
        part = scaled - (scaled - rest)
        parts.append(part)
        rest = rest - part
    parts.append(jnp.zeros((4, nj, BIAS_ROWS - BIAS_PARTS, LANES), F32))
    return wg, jnp.concatenate(parts, axis=-2).astype(BF16)


def kernel(x_prompt, x_sample, state_lru, c, c_ctx, mod_w, mod_b, norm_mix, norm_ffn, lru_w_in, lru_conv_w, lru_conv_b, lru_ga_w, lru_ga_b, lru_gx_w, lru_gx_b, lru_lambda, lru_w_out, cm_w_in, cm_b_in, cm_ln_g, cm_ln_b, cm_w_s, cm_b_s, cm_w_out, ffn_w_gate, ffn_w_up, ffn_w_down, final_norm):
    bp, sp, d = x_prompt.shape
    bs, ss, _ = x_sample.shape
    dims = ((bp, sp), (bs, ss))
    light, heavy = _Tiles(dims, TM_LIGHT), _Tiles(dims, TM_HEAVY)
    xp = x_prompt.reshape(bp * sp, d)
    xs = x_sample.reshape(bs * ss, d)

    cond = jnp.concatenate([c_ctx[None], c, jnp.zeros((COND_ROWS - 1 - bs, d), F32)], 0)
    mods = _adaln(cond, mod_w, mod_b)

    w = lru_w_out.shape[1]
    xbp, xbs = _lru_in(light, xp, xs, mods, norm_mix[0:1], lru_w_in[0, :, :w].astype(BF16), 0,
                       SUBLANES)
    wg, gb = _gate_weights(lru_ga_w[0], lru_gx_w[0], lru_ga_b[0], lru_gx_b[0])
    rec = (lru_conv_w[0], lru_conv_b[0:1], wg, gb, lru_lambda[0])
    hp, st, (wi0, wo0, wg0, wu0, wd0) = _lru_rec(
        xbp, *rec, None,
        [(lru_w_in, 0), (lru_w_out, 0), (ffn_w_gate, 0), (ffn_w_up, 0), (ffn_w_down, 0)])
    hs, _, (wi1, wo1, wg1, wu1, wd1) = _lru_rec(
        xbs, *rec, jnp.transpose(state_lru[:, 0], (1, 0, 2)),
        [(cm_w_in, 0), (cm_w_out, 0), (ffn_w_gate, 1), (ffn_w_up, 1), (ffn_w_down, 1)])
    x = _lru_out(heavy, xp, xs, hp, hs, mods, norm_mix[0:1], norm_ffn[0:1], wi0, wo0, wg0, wu0,
                 wd0, 0)

    n_groups = cm_w_s.shape[1]
    gd = cm_w_out.shape[1] // n_groups
    bs_tile = jnp.repeat(cm_b_s[0].T, gd, axis=1)
    yp, ys = _cm_layer(heavy, x, mods, norm_mix[1:2], norm_ffn[1:2], final_norm[None],
                       wi1, cm_b_in[0:1], cm_ln_g[0:1], cm_ln_b[0:1],
                       cm_w_s[0].astype(BF16), bs_tile, wo1, wg1, wu1, wd1, 1)

    new_state = jnp.transpose(st, (1, 0, 2))[:, None]
    return (yp.reshape(bp, sp, d), ys.reshape(bs, ss, d), new_state)
```

```python
import functools

import jax
import jax.numpy as jnp
from jax import lax
from jax.experimental import pallas as pl
from jax.experimental.pallas import tpu as pltpu

F32 = jnp.float32
BF16 = jnp.bfloat16

EPS = 1e-6
LRU_C = 8.0
LOG2_E = 1.4426950408889634
N_MOD = 6
COND_ROWS = 16
LANES = 128
SUBLANES = 8
CHUNK = 128
TM_LIGHT = 1024
TM_HEAVY = 512
SUB = 256
GATE_ROWS = 2048
GATE_STEPS = 64
BIAS_PARTS = 3
BIAS_ROWS = 16
VMEM_LIMIT = 56 * 1024 * 1024


def _rmsnorm(x, g):
    y = x * lax.rsqrt(jnp.mean(x * x, axis=-1, keepdims=True) + EPS)
    return y * g


def _modulated_norm(x, g, shift, scale):
    return _rmsnorm(x, g) * (1.0 + scale) + shift


def _dot(a, b):
    return jnp.dot(a.astype(BF16), b, preferred_element_type=F32)


def _adaln_kernel(c_ref, w_ref, b_ref, o_ref):
    s = jax.nn.silu(c_ref[...])
    bias = b_ref[pl.ds(pl.program_id(0), 1), :]
    o_ref[...] = _dot(s, w_ref[...].astype(BF16)) + bias


def _adaln(cond, mod_w, mod_b):
    depth, d, n = mod_w.shape
    tn = 2048
    return pl.pallas_call(
        _adaln_kernel,
        grid=(depth, n // tn),
        in_specs=[
            pl.BlockSpec((COND_ROWS, d), lambda l, j: (0, 0)),
            pl.BlockSpec((None, d, tn), lambda l, j: (l, 0, j)),
            pl.BlockSpec((depth, tn), lambda l, j: (0, j)),
        ],
        out_specs=pl.BlockSpec((None, COND_ROWS, tn), lambda l, j: (l, 0, j)),
        out_shape=jax.ShapeDtypeStruct((depth, COND_ROWS, n), F32),
        compiler_params=pltpu.CompilerParams(
            dimension_semantics=("arbitrary", "arbitrary"), vmem_limit_bytes=VMEM_LIMIT),
        name="adaln",
    )(cond, mod_w, mod_b)


class _Tiles:
    def __init__(self, dims, tm):
        (self.bp, self.sp), (self.bs, self.ss) = dims
        assert tm % self.sp == 0 and self.ss % tm == 0 and (self.bp * self.sp) % tm == 0
        assert tm % SUB == 0
        self.tm = tm
        self.npt = self.bp * self.sp // tm
        self.nst = self.bs * self.ss // tm
        self.n = self.npt + self.nst

    def subs(self):
        return [slice(s0, s0 + SUB) for s0 in range(0, self.tm, SUB)]

    def is_prompt(self):
        return pl.program_id(0) < self.npt

    def combined(self, cols):
        return pl.BlockSpec((self.tm, cols), lambda i: (i, 0))

    def _pi(self, i):
        return jnp.minimum(i, self.npt - 1)

    def _si(self, i):
        return jnp.maximum(i - self.npt, 0)

    def flat_pair(self, cols):
        return [pl.BlockSpec((self.tm, cols), lambda i: (self._pi(i), 0)),
                pl.BlockSpec((self.tm, cols), lambda i: (self._si(i), 0))]

    def slab_pair(self, nj):
        ts = self.ss // self.tm
        return [pl.BlockSpec((nj, self.tm // self.sp, self.sp, LANES),
                             lambda i: (0, self._pi(i), 0, 0)),
                pl.BlockSpec((nj, None, self.tm, LANES),
                             lambda i: (0, self._si(i) // ts, self._si(i) % ts, 0))]

    def slab_out_pair(self, nj, pad):
        ts = self.ss // self.tm
        return [pl.BlockSpec((nj, self.tm // self.sp, self.sp + pad, LANES),
                             lambda i: (0, self._pi(i), 0, 0)),
                pl.BlockSpec((nj, None, self.ss + pad, LANES),
                             lambda i: (0, self._si(i) // ts, 0, 0))]

    def branch(self, prompt_fn, sample_fn):
        pl.when(self.is_prompt())(prompt_fn)
        pl.when(jnp.logical_not(self.is_prompt()))(sample_fn)

    def write_slabs_prompt(self, p_ref, value):
        nj, nb, rows, _ = p_ref.shape
        for j in range(nj):
            p_ref[j, :, :self.sp, :] = value[:, j * LANES:(j + 1) * LANES].reshape(nb, self.sp, LANES)
        p_ref[:, :, self.sp:, :] = jnp.zeros((nj, nb, rows - self.sp, LANES), value.dtype)

    def write_slabs_sample(self, s_ref, value):
        nj, rows, _ = s_ref.shape
        part = self._si(pl.program_id(0)) % (self.ss // self.tm)
        dst = pl.ds(pl.multiple_of(part * self.tm, self.tm), self.tm)
        for j in range(nj):
            s_ref[j, dst, :] = value[:, j * LANES:(j + 1) * LANES]
        s_ref[:, self.ss:, :] = jnp.zeros((nj, rows - self.ss, LANES), value.dtype)

    def mod(self, layer, mods):
        return pl.BlockSpec((None,) + mods.shape[1:], lambda i: (layer, 0, 0))

    def mods(self, mod_ref):
        i = pl.program_id(0)
        row = jnp.where(i < self.npt, 0, 1 + self._si(i) * self.tm // self.ss)
        m = mod_ref[pl.ds(row, 1), :]
        d = m.shape[1] // N_MOD
        return [m[:, k * d:(k + 1) * d] for k in range(N_MOD)]


def _const_spec(shape):
    nd = len(shape)
    return pl.BlockSpec(shape, lambda i: (0,) * nd, pipeline_mode=pl.Buffered(1))


_TOKEN_PARAMS = pltpu.CompilerParams(
    dimension_semantics=("arbitrary",), vmem_limit_bytes=VMEM_LIMIT)


def _lru_in_kernel(tiles, xp_ref, xs_ref, mod_ref, g_ref, w_ref, xbp_ref, xbs_ref):
    def project(x_ref):
        m = tiles.mods(mod_ref)
        return jnp.concatenate(
            [_dot(_modulated_norm(x_ref[r, :], g_ref[...], m[0], m[1]), w_ref[...])
             for r in tiles.subs()], axis=0)

    tiles.branch(lambda: tiles.write_slabs_prompt(xbp_ref, project(xp_ref)),
                 lambda: tiles.write_slabs_sample(xbs_ref, project(xs_ref)))


def _lru_in(tiles, xp, xs, mods, g, w_x, layer, pad):
    d, w = w_x.shape
    return pl.pallas_call(
        functools.partial(_lru_in_kernel, tiles),
        grid=(tiles.n,),
        in_specs=tiles.flat_pair(d) + [tiles.mod(layer, mods), _const_spec((1, d)),
                                       _const_spec((d, w))],
        out_specs=tiles.slab_out_pair(w // LANES, pad),
        out_shape=[jax.ShapeDtypeStruct((w // LANES, tiles.bp, tiles.sp + pad, LANES), F32),
                   jax.ShapeDtypeStruct((w // LANES, tiles.bs, tiles.ss + pad, LANES), F32)],
        compiler_params=_TOKEN_PARAMS,
        name="lru_in",
    )(xp, xs, mods, g, w_x)


def _gate_rows(B):
    return max(GATE_ROWS, GATE_STEPS * B)


def _lru_rec_kernel(B, S, P, has_h0, n_cast, *refs):
    n_in = 6 + int(has_h0)
    x_ref, cw_ref, cb_ref, wg_ref, gb_ref, lam_ref = refs[:6]
    h0_ref = refs[6] if has_h0 else None
    cast_src = refs[n_in:n_in + n_cast]
    out_ref, st_ref = refs[n_in + n_cast:n_in + n_cast + 2]
    cast_dst = refs[n_in + n_cast + 2:n_in + 2 * n_cast + 2]
    xt, xhb, hfb, a_e, bx_e, a_o, bx_o, g_e, g_o = refs[n_in + 2 * n_cast + 2:]
    rows = B * S
    R = _gate_rows(B)
    n = rows // R
    steps = R // B
    assert n % 2 == 0 and n >= 4 and R % B == 0

    for src, dst in zip(cast_src, cast_dst):
        dst[...] = src[...].astype(BF16)

    for b in range(B):
        out_ref[pl.ds(b * P + S, P - S), :] = jnp.zeros((P - S, LANES), F32)

    xt[pl.ds(0, B), :] = jnp.zeros((B, LANES), F32)
    xt[pl.ds(rows + B, 2 * B), :] = jnp.zeros((2 * B, LANES), F32)

    def gather(t, _):
        xt[pl.ds(pl.multiple_of((t + 1) * B, B), B), :] = x_ref[pl.ds(t, B, stride=P), :]
        return 0
    lax.fori_loop(0, S, gather, 0, unroll=64)

    cw = 0.5 * cw_ref[...]
    cb = 0.5 * cb_ref[...]
    def gate_operand(k):
        pad = jnp.zeros((LANES - gb_ref.shape[1], LANES), BF16)
        return jnp.concatenate([wg_ref[k], gb_ref[k], pad], axis=0)
    wgz = [jnp.concatenate([gate_operand(2 * z), gate_operand(2 * z + 1)], axis=1)
           for z in range(2)]
    lane = lax.broadcasted_iota(jnp.int32, (R, LANES), 1)
    bias_cols = jnp.where(lane < BIAS_PARTS, 1.0, 0.0).astype(BF16)
    neg = -lam_ref[...]
    sp = jnp.maximum(neg, 0.0) + jnp.log1p(jnp.exp(-jnp.abs(neg)))
    c_pos = (0.5 * LRU_C) * sp
    c_exp2 = -LOG2_E * c_pos

    def preact(z, c, g_dst):
        r0 = pl.multiple_of(c * R, R)
        if z == 0:
            xh = (cw[0:1] * xt[pl.ds(r0, R), :]
                  + cw[1:2] * xt[pl.ds(r0 + B, R), :]
                  + cw[2:3] * xt[pl.ds(r0 + 2 * B, R), :]
                  + cw[3:4] * xt[pl.ds(r0 + 3 * B, R), :]) + cb
            xhb[pl.ds(r0, R), :] = xh
        else:
            xh = xhb[pl.ds(r0, R), :]
        lhs = jnp.concatenate([xh.astype(BF16), bias_cols], axis=1)
        g_dst[...] = jnp.dot(lhs, wgz[z], preferred_element_type=F32)

    def gates(z, c, g_src, a_dst, bx_dst):
        xh = xhb[pl.ds(pl.multiple_of(c * R, R), R), :]
        r2 = jnp.tanh(g_src[:, :LANES]) + 1.0
        i2 = jnp.tanh(g_src[:, LANES:]) + 1.0
        a = jnp.exp2(r2 * c_exp2[z:z + 1])
        om = jnp.tanh(r2 * c_pos[z:z + 1]) * (a * a + 1.0)
        mult = jnp.where(om > 0.0, om * lax.rsqrt(om), 0.0)
        a_dst[...] = a
        bx_dst[...] = mult * (i2 * xh)

    def scan_f(c, a_src, bx_src, h):
        for k in range(steps):
            h = a_src[k * B:(k + 1) * B, :] * h + bx_src[k * B:(k + 1) * B, :]
            hfb[pl.ds(pl.multiple_of(c * R + k * B, B), B), :] = h
        return h

    def scan_b(c, a_src, bx_src, h):
        for k in reversed(range(steps)):
            h = a_src[k * B:(k + 1) * B, :] * h + bx_src[k * B:(k + 1) * B, :]
            hf = hfb[pl.ds(pl.multiple_of(c * R + k * B, B), B), :]
            out_ref[pl.ds(c * steps + k, B, stride=P), :] = hf + h
        return h

    def run_pass(z, chunk, scan, h):
        even, odd = (g_e, a_e, bx_e), (g_o, a_o, bx_o)
        preact(z, chunk(0), g_e)
        gates(z, chunk(0), *even)
        preact(z, chunk(1), g_o)

        def body(k, h):
            i = 2 * k + 1
            preact(z, chunk(i + 1), g_e)
            gates(z, chunk(i), *odd)
            h = scan(chunk(i - 1), a_e, bx_e, h)
            preact(z, chunk(i + 2), g_o)
            gates(z, chunk(i + 1), *even)
            return scan(chunk(i), a_o, bx_o, h)
        h = lax.fori_loop(0, n // 2 - 1, body, h)
        gates(z, chunk(n - 1), *odd)
        h = scan(chunk(n - 2), a_e, bx_e, h)
        return scan(chunk(n - 1), a_o, bx_o, h)

    if has_h0:
        h0f, h0b = h0_ref[0], h0_ref[1]
    else:
        h0f = h0b = jnp.zeros((B, LANES), F32)
    st_ref[0] = run_pass(0, lambda i: i, scan_f, h0f)
    st_ref[1] = run_pass(1, lambda i: n - 1 - i, scan_b, h0b)


def _lru_rec(xb, conv_w, conv_b, wg, gb, lam, h0, casts):
    nj, B, P, _ = xb.shape
    S = P - SUBLANES
    rows = B * S
    w = nj * LANES
    has_h0 = h0 is not None
    slab = pl.BlockSpec((None, B * P, LANES), lambda j: (j, 0, 0))
    state = pl.BlockSpec((2, B, LANES), lambda j: (0, 0, j))
    in_specs = [
        slab,
        pl.BlockSpec((4, LANES), lambda j: (0, j)),
        pl.BlockSpec((1, LANES), lambda j: (0, j)),
        pl.BlockSpec((4, None, LANES, LANES), lambda j: (0, j, 0, 0)),
        pl.BlockSpec((4, None, BIAS_ROWS, LANES), lambda j: (0, j, 0, 0)),
        pl.BlockSpec((2, LANES), lambda j: (0, j)),
    ]
    args = [xb.reshape(nj, B * P, LANES), conv_w, conv_b, wg, gb, lam]
    if has_h0:
        in_specs.append(state)
        args.append(h0)
    out_specs = [slab, state]
    out_shape = [jax.ShapeDtypeStruct((nj, B * P, LANES), F32),
                 jax.ShapeDtypeStruct((2, B, w), F32)]
    for wt, layer in casts:
        _, r, c = wt.shape
        assert r % (nj * 2 * SUBLANES) == 0
        in_specs.append(pl.BlockSpec((None, r // nj, c), functools.partial(
            lambda j, layer: (layer, j, 0), layer=layer)))
        args.append(wt)
        out_specs.append(pl.BlockSpec((r // nj, c), lambda j: (j, 0)))
        out_shape.append(jax.ShapeDtypeStruct((r, c), BF16))
    res = pl.pallas_call(
        functools.partial(_lru_rec_kernel, B, S, P, has_h0, len(casts)),
        grid=(nj,),
        in_specs=in_specs,
        out_specs=out_specs,
        out_shape=out_shape,
        scratch_shapes=[pltpu.VMEM((rows + 3 * B, LANES), F32)]
        + [pltpu.VMEM((rows, LANES), F32)] * 2 + [pltpu.VMEM((_gate_rows(B), LANES), F32)] * 4
        + [pltpu.VMEM((_gate_rows(B), 2 * LANES), F32)] * 2,
        compiler_params=pltpu.CompilerParams(
            dimension_semantics=("arbitrary",), vmem_limit_bytes=VMEM_LIMIT),
        name="lru_rec",
    )(*args)
    return res[0].reshape(nj, B, P, LANES), res[1], res[2:]


def _ffn(x1, m, g, wg_ref, wu_ref, wd_ref):
    h = [_modulated_norm(v, g, m[3], m[4]).astype(BF16) for v in x1]
    gu = [(jnp.dot(v, wg_ref[...], preferred_element_type=F32),
           jnp.dot(v, wu_ref[...], preferred_element_type=F32)) for v in h]
    dn = [_dot(jax.nn.silu(gate) * up, wd_ref[...]) for gate, up in gu]
    return [v + m[5] * d for v, d in zip(x1, dn)]


def _lru_out_kernel(tiles, xp_ref, xs_ref, hp_ref, hs_ref, mod_ref, gm_ref, gf_ref,
                    wy_ref, wo_ref, wg_ref, wu_ref, wd_ref, o_ref):
    subs = tiles.subs()

    def body(x_ref, hsum):
        m = tiles.mods(mod_ref)
        x = [x_ref[r, :] for r in subs]
        y = [jax.nn.gelu(_dot(_modulated_norm(v, gm_ref[...], m[0], m[1]), wy_ref[...]))
             for v in x]
        x1 = [v + m[2] * _dot(h * g, wo_ref[...]) for v, h, g in zip(x, hsum, y)]
        for r, v in zip(subs, _ffn(x1, m, gf_ref[...], wg_ref, wu_ref, wd_ref)):
            o_ref[r, :] = v

    nj = hp_ref.shape[0]

    def prompt_hsum(r):
        return jnp.concatenate(
            [hp_ref[j].reshape(tiles.tm, LANES)[r] for j in range(nj)], axis=1)

    def sample_hsum(r):
        return jnp.concatenate([hs_ref[j, r, :] for j in range(nj)], axis=1)

    tiles.branch(lambda: body(xp_ref, [prompt_hsum(r) for r in subs]),
                 lambda: body(xs_ref, [sample_hsum(r) for r in subs]))


def _lru_out(tiles, xp, xs, hp, hs, mods, g_mix, g_ffn, w_in, w_out, w_gate, w_up, w_down, layer):
    d = xp.shape[1]
    w = w_out.shape[0]
    f = w_gate.shape[1]
    w_y = pl.BlockSpec((d, w), lambda i: (0, 1), pipeline_mode=pl.Buffered(1))
    return pl.pallas_call(
        functools.partial(_lru_out_kernel, tiles),
        grid=(tiles.n,),
        in_specs=tiles.flat_pair(d) + tiles.slab_pair(w // LANES) + [
            tiles.mod(layer, mods), _const_spec((1, d)), _const_spec((1, d)), w_y,
            _const_spec((w, d)), _const_spec((d, f)), _const_spec((d, f)), _const_spec((f, d))],
        out_specs=tiles.combined(d),
        out_shape=jax.ShapeDtypeStruct((tiles.n * tiles.tm, d), F32),
        compiler_params=_TOKEN_PARAMS,
        name="lru_out_ffn",
    )(xp, xs, hp, hs, mods, g_mix, g_ffn, w_in, w_out, w_gate, w_up, w_down)


def _cm_kernel(tiles, x_ref, mod_ref, gm_ref, gf_ref, gfin_ref, wi_ref, bi_ref, lng_ref, lnb_ref,
               ws_ref, bs_ref, wo_ref, wg_ref, wu_ref, wd_ref, op_ref, os_ref):
    cw = lng_ref.shape[-1]
    n_groups = ws_ref.shape[0]
    gd = cw // n_groups

    def body():
        m = tiles.mods(mod_ref)
        xs = [x_ref[r, :] for r in tiles.subs()]
        pre = [_dot(_modulated_norm(x, gm_ref[...], m[0], m[1]), wi_ref[...]) + bi_ref[...]
               for x in xs]
        x1 = []
        for x, z in zip(xs, pre):
            uv = jax.nn.gelu(z)
            u = uv[:, :cw]
            v = uv[:, cw:]
            mu = jnp.mean(v, axis=-1, keepdims=True)
            var = jnp.mean(jnp.square(v - mu), axis=-1, keepdims=True)
            v = ((v - mu) * lax.rsqrt(var + EPS) * lng_ref[...] + lnb_ref[...]).astype(BF16)
            mixed_rows = []
            for c in range(SUB // CHUNK):
                vc = v[c * CHUNK:(c + 1) * CHUNK]
                cols = [jnp.dot(ws_ref[g], vc[:, g * gd:(g + 1) * gd],
                                preferred_element_type=F32) for g in range(n_groups)]
                mixed_rows.append(jnp.concatenate(cols, axis=1) + bs_ref[...])
            mixed = jnp.concatenate(mixed_rows, axis=0)
            x1.append(x + m[2] * _dot(u * mixed, wo_ref[...]))
        x2 = _ffn(x1, m, gf_ref[...], wg_ref, wu_ref, wd_ref)
        return jnp.concatenate([_rmsnorm(v, gfin_ref[...]) for v in x2], axis=0)

    out = body()

    def put(o_ref):
        o_ref[...] = out
    tiles.branch(lambda: put(op_ref), lambda: put(os_ref))


def _cm_layer(tiles, x, mods, g_mix, g_ffn, g_final, w_in, b_in, ln_g, ln_b, w_s, b_s, w_out,
              w_gate, w_up, w_down, layer):
    d = x.shape[1]
    cw = w_out.shape[0]
    f = w_gate.shape[1]
    return pl.pallas_call(
        functools.partial(_cm_kernel, tiles),
        grid=(tiles.n,),
        in_specs=[tiles.combined(d), tiles.mod(layer, mods),
                  _const_spec((1, d)), _const_spec((1, d)), _const_spec((1, d)),
                  _const_spec((d, 2 * cw)), _const_spec((1, 2 * cw)),
                  _const_spec((1, cw)), _const_spec((1, cw)),
                  _const_spec(w_s.shape), _const_spec(b_s.shape), _const_spec((cw, d)),
                  _const_spec((d, f)), _const_spec((d, f)), _const_spec((f, d))],
        out_specs=tiles.flat_pair(d),
        out_shape=[jax.ShapeDtypeStruct((tiles.npt * tiles.tm, d), F32),
                   jax.ShapeDtypeStruct((tiles.nst * tiles.tm, d), F32)],
        compiler_params=_TOKEN_PARAMS,
        name="cm_layer",
    )(x, mods, g_mix, g_ffn, g_final, w_in, b_in, ln_g, ln_b, w_s, b_s, w_out,
      w_gate, w_up, w_down)


def _gate_weights(ga_w, gx_w, ga_b, gx_b):
    _, heads, hd, _ = ga_w.shape
    per = LANES // hd
    nj = heads // per
    w4 = jnp.stack([ga_w[0], gx_w[0], ga_w[1], gx_w[1]]).reshape(4, nj, per, hd, hd)
    rows = []
    for h in range(per):
        blocks = [w4[:, :, h] if g == h else jnp.zeros_like(w4[:, :, h]) for g in range(per)]
        rows.append(jnp.concatenate(blocks, axis=-1))
    wg = jnp.concatenate(rows, axis=-2).astype(BF16)
    rest = 0.5 * jnp.stack([ga_b[0], gx_b[0], ga_b[1], gx_b[1]]).reshape(4, nj, 1, LANES)
    parts = []
    for _ in range(BIAS_PARTS):
        scaled = rest * (2.0 ** 16 + 1.0)
        part = scaled - (scaled - rest)
        parts.append(part)
        rest = rest - part
    parts.append(jnp.zeros((4, nj, BIAS_ROWS - BIAS_PARTS, LANES), F32))
    return wg, jnp.concatenate(parts, axis=-2).astype(BF16)


def kernel(x_prompt, x_sample, state_lru, c, c_ctx, mod_w, mod_b, norm_mix, norm_ffn, lru_w_in, lru_conv_w, lru_conv_b, lru_ga_w, lru_ga_b, lru_gx_w, lru_gx_b, lru_lambda, lru_w_out, cm_w_in, cm_b_in, cm_ln_g, cm_ln_b, cm_w_s, cm_b_s, cm_w_out, ffn_w_gate, ffn_w_up, ffn_w_down, final_norm):
    bp, sp, d = x_prompt.shape
    bs, ss, _ = x_sample.shape
    dims = ((bp, sp), (bs, ss))
    light, heavy = _Tiles(dims, TM_LIGHT), _Tiles(dims, TM_HEAVY)
    xp = x_prompt.reshape(bp * sp, d)
    xs = x_sample.reshape(bs * ss, d)

    cond = jnp.concatenate([c_ctx[None], c, jnp.zeros((COND_ROWS - 1 - bs, d), F32)], 0)
    mods = _adaln(cond, mod_w, mod_b)

    w = lru_w_out.shape[1]
    xbp, xbs = _lru_in(light, xp, xs, mods, norm_mix[0:1], lru_w_in[0, :, :w].astype(BF16), 0,
                       SUBLANES)
    wg, gb = _gate_weights(lru_ga_w[0], lru_gx_w[0], lru_ga_b[0], lru_gx_b[0])
    rec = (lru_conv_w[0], lru_conv_b[0:1], wg, gb, lru_lambda[0])
    hp, st, (wi0, wo0, wg0, wu0, wd0) = _lru_rec(
        xbp, *rec, None,
        [(lru_w_in, 0), (lru_w_out, 0), (ffn_w_gate, 0), (ffn_w_up, 0), (ffn_w_down, 0)])
    hs, _, (wi1, wo1, wg1, wu1, wd1) = _lru_rec(
        xbs, *rec, jnp.transpose(state_lru[:, 0], (1, 0, 2)),
        [(cm_w_in, 0), (cm_w_out, 0), (ffn_w_gate, 1), (ffn_w_up, 1), (ffn_w_down, 1)])
    x = _lru_out(heavy, xp, xs, hp, hs, mods, norm_mix[0:1], norm_ffn[0:1], wi0, wo0, wg0, wu0,
                 wd0, 0)

    n_groups = cm_w_s.shape[1]
    gd = cm_w_out.shape[1] // n_groups
    bs_tile = jnp.repeat(cm_b_s[0].T, gd, axis=1)
    yp, ys = _cm_layer(heavy, x, mods, norm_mix[1:2], norm_ffn[1:2], final_norm[None],
                       wi1, cm_b_in[0:1], cm_ln_g[0:1], cm_ln_b[0:1],
                       cm_w_s[0].astype(BF16), bs_tile, wo1, wg1, wu1, wd1, 1)

    new_state = jnp.transpose(st, (1, 0, 2))[:, None]
    return (yp.reshape(bp, sp, d), ys.reshape(bs, ss, d), new_state)
```

```python
import functools

import jax
import jax.numpy as jnp
from jax import lax
from jax.experimental import pallas as pl
from jax.experimental.pallas import tpu as pltpu

F32 = jnp.float32
BF16 = jnp.bfloat16

EPS = 1e-6
LRU_C = 8.0
LOG2_E = 1.4426950408889634
N_MOD = 6
COND_ROWS = 16
LANES = 128
SUBLANES = 8
CHUNK = 128
TM_LIGHT = 1024
TM_HEAVY = 512
IN_RING = 3
SUB = 256
GATE_ROWS = 2048
GATE_STEPS = 64
BIAS_PARTS = 3
BIAS_ROWS = 16
VMEM_LIMIT = 56 * 1024 * 1024


def _rmsnorm(x, g):
    y = x * lax.rsqrt(jnp.mean(x * x, axis=-1, keepdims=True) + EPS)
    return y * g


def _modulated_norm(x, g, shift, scale):
    return _rmsnorm(x, g) * (1.0 + scale) + shift


def _dot(a, b):
    return jnp.dot(a.astype(BF16), b, preferred_element_type=F32)


def _adaln_kernel(c_ref, w_ref, b_ref, o_ref):
    s = jax.nn.silu(c_ref[...])
    bias = b_ref[pl.ds(pl.program_id(0), 1), :]
    o_ref[...] = _dot(s, w_ref[...].astype(BF16)) + bias


def _adaln(cond, mod_w, mod_b):
    depth, d, n = mod_w.shape
    tn = 2048
    return pl.pallas_call(
        _adaln_kernel,
        grid=(depth, n // tn),
        in_specs=[
            pl.BlockSpec((COND_ROWS, d), lambda l, j: (0, 0)),
            pl.BlockSpec((None, d, tn), lambda l, j: (l, 0, j)),
            pl.BlockSpec((depth, tn), lambda l, j: (0, j)),
        ],
        out_specs=pl.BlockSpec((None, COND_ROWS, tn), lambda l, j: (l, 0, j)),
        out_shape=jax.ShapeDtypeStruct((depth, COND_ROWS, n), F32),
        compiler_params=pltpu.CompilerParams(
            dimension_semantics=("arbitrary", "arbitrary"), vmem_limit_bytes=VMEM_LIMIT),
        name="adaln",
    )(cond, mod_w, mod_b)


class _Tiles:
    def __init__(self, dims, tm):
        (self.bp, self.sp), (self.bs, self.ss) = dims
        assert tm % self.sp == 0 and self.ss % tm == 0 and (self.bp * self.sp) % tm == 0
        assert tm % SUB == 0
        self.tm = tm
        self.npt = self.bp * self.sp // tm
        self.nst = self.bs * self.ss // tm
        self.n = self.npt + self.nst

    def subs(self):
        return [slice(s0, s0 + SUB) for s0 in range(0, self.tm, SUB)]

    def is_prompt(self):
        return pl.program_id(0) < self.npt

    def combined(self, cols):
        return pl.BlockSpec((self.tm, cols), lambda i: (i, 0))

    def _pi(self, i):
        return jnp.minimum(i, self.npt - 1)

    def _si(self, i):
        return jnp.maximum(i - self.npt, 0)

    def flat_pair(self, cols):
        return [pl.BlockSpec((self.tm, cols), lambda i: (self._pi(i), 0)),
                pl.BlockSpec((self.tm, cols), lambda i: (self._si(i), 0))]

    def slab_pair(self, nj):
        ts = self.ss // self.tm
        return [pl.BlockSpec((nj, self.tm // self.sp, self.sp, LANES),
                             lambda i: (0, self._pi(i), 0, 0)),
                pl.BlockSpec((nj, None, self.tm, LANES),
                             lambda i: (0, self._si(i) // ts, self._si(i) % ts, 0))]

    def slab_out_pair(self, nj, pad):
        ts = self.ss // self.tm
        return [pl.BlockSpec((nj, self.tm // self.sp, self.sp + pad, LANES),
                             lambda i: (0, self._pi(i), 0, 0)),
                pl.BlockSpec((nj, None, self.ss + pad, LANES),
                             lambda i: (0, self._si(i) // ts, 0, 0))]

    def branch(self, prompt_fn, sample_fn):
        pl.when(self.is_prompt())(prompt_fn)
        pl.when(jnp.logical_not(self.is_prompt()))(sample_fn)

    def write_slabs_prompt(self, p_ref, value):
        nj, nb, rows, _ = p_ref.shape
        for j in range(nj):
            p_ref[j, :, :self.sp, :] = value[:, j * LANES:(j + 1) * LANES].reshape(nb, self.sp, LANES)
        p_ref[:, :, self.sp:, :] = jnp.zeros((nj, nb, rows - self.sp, LANES), value.dtype)

    def write_slabs_sample(self, s_ref, value):
        nj, rows, _ = s_ref.shape
        part = self._si(pl.program_id(0)) % (self.ss // self.tm)
        dst = pl.ds(pl.multiple_of(part * self.tm, self.tm), self.tm)
        for j in range(nj):
            s_ref[j, dst, :] = value[:, j * LANES:(j + 1) * LANES]
        s_ref[:, self.ss:, :] = jnp.zeros((nj, rows - self.ss, LANES), value.dtype)

    def mod(self, layer, mods):
        return pl.BlockSpec((None,) + mods.shape[1:], lambda i: (layer, 0, 0))

    def mods(self, mod_ref):
        i = pl.program_id(0)
        row = jnp.where(i < self.npt, 0, 1 + self._si(i) * self.tm // self.ss)
        m = mod_ref[pl.ds(row, 1), :]
        d = m.shape[1] // N_MOD
        return [m[:, k * d:(k + 1) * d] for k in range(N_MOD)]


def _const_spec(shape):
    nd = len(shape)
    return pl.BlockSpec(shape, lambda i: (0,) * nd, pipeline_mode=pl.Buffered(1))


_TOKEN_PARAMS = pltpu.CompilerParams(
    dimension_semantics=("arbitrary",), vmem_limit_bytes=VMEM_LIMIT)


def _lru_in_kernel(tiles, xp_hbm, xs_hbm, mod_ref, g_ref, w_ref, xbp_ref, xbs_ref, ring, sem):
    step = pl.program_id(0)
    tm = tiles.tm

    def copy(src_hbm, first_row, slot):
        return pltpu.make_async_copy(src_hbm.at[pl.ds(first_row, tm), :], ring.at[slot],
                                     sem.at[slot])

    def start(s):
        slot = s % IN_RING
        pl.when(s < tiles.npt)(lambda: copy(xp_hbm, s * tm, slot).start())
        pl.when(jnp.logical_and(s >= tiles.npt, s < tiles.n))(
            lambda: copy(xs_hbm, (s - tiles.npt) * tm, slot).start())

    @pl.when(step == 0)
    def _():
        assert IN_RING - 1 <= tiles.npt
        for s in range(IN_RING - 1):
            copy(xp_hbm, s * tm, s).start()
    start(step + IN_RING - 1)
    slot = step % IN_RING
    copy(xp_hbm, 0, slot).wait()

    def project():
        m = tiles.mods(mod_ref)
        return jnp.concatenate(
            [_dot(_modulated_norm(ring[slot, r, :], g_ref[...], m[0], m[1]), w_ref[...])
             for r in tiles.subs()], axis=0)

    tiles.branch(lambda: tiles.write_slabs_prompt(xbp_ref, project()),
                 lambda: tiles.write_slabs_sample(xbs_ref, project()))


def _lru_in(tiles, xp, xs, mods, g, w_x, layer, pad):
    d, w = w_x.shape
    return pl.pallas_call(
        functools.partial(_lru_in_kernel, tiles),
        grid=(tiles.n,),
        in_specs=[pl.BlockSpec(memory_space=pl.ANY), pl.BlockSpec(memory_space=pl.ANY),
                  tiles.mod(layer, mods), _const_spec((1, d)), _const_spec((d, w))],
        out_specs=tiles.slab_out_pair(w // LANES, pad),
        out_shape=[jax.ShapeDtypeStruct((w // LANES, tiles.bp, tiles.sp + pad, LANES), F32),
                   jax.ShapeDtypeStruct((w // LANES, tiles.bs, tiles.ss + pad, LANES), F32)],
        scratch_shapes=[pltpu.VMEM((IN_RING, tiles.tm, d), F32),
                        pltpu.SemaphoreType.DMA((IN_RING,))],
        compiler_params=_TOKEN_PARAMS,
        name="lru_in",
    )(xp, xs, mods, g, w_x)


def _gate_rows(B):
    return max(GATE_ROWS, GATE_STEPS * B)


def _lru_rec_kernel(B, S, P, has_h0, n_cast, *refs):
    n_in = 6 + int(has_h0)
    x_ref, cw_ref, cb_ref, wg_ref, gb_ref, lam_ref = refs[:6]
    h0_ref = refs[6] if has_h0 else None
    cast_src = refs[n_in:n_in + n_cast]
    out_ref, st_ref = refs[n_in + n_cast:n_in + n_cast + 2]
    cast_dst = refs[n_in + n_cast + 2:n_in + 2 * n_cast + 2]
    xt, xhb, hfb, a_e, bx_e, a_o, bx_o, g_e, g_o = refs[n_in + 2 * n_cast + 2:]
    rows = B * S
    R = _gate_rows(B)
    n = rows // R
    steps = R // B
    assert n % 2 == 0 and n >= 4 and R % B == 0

    for src, dst in zip(cast_src, cast_dst):
        dst[...] = src[...].astype(BF16)

    for b in range(B):
        out_ref[pl.ds(b * P + S, P - S), :] = jnp.zeros((P - S, LANES), F32)

    xt[pl.ds(0, B), :] = jnp.zeros((B, LANES), F32)
    xt[pl.ds(rows + B, 2 * B), :] = jnp.zeros((2 * B, LANES), F32)

    def gather(t, _):
        xt[pl.ds(pl.multiple_of((t + 1) * B, B), B), :] = x_ref[pl.ds(t, B, stride=P), :]
        return 0
    lax.fori_loop(0, S, gather, 0, unroll=64)

    cw = 0.5 * cw_ref[...]
    cb = 0.5 * cb_ref[...]
    def gate_operand(k):
        pad = jnp.zeros((LANES - gb_ref.shape[1], LANES), BF16)
        return jnp.concatenate([wg_ref[k], gb_ref[k], pad], axis=0)
    wgz = [jnp.concatenate([gate_operand(2 * z), gate_operand(2 * z + 1)], axis=1)
           for z in range(2)]
    lane = lax.broadcasted_iota(jnp.int32, (R, LANES), 1)
    bias_cols = jnp.where(lane < BIAS_PARTS, 1.0, 0.0).astype(BF16)
    neg = -lam_ref[...]
    sp = jnp.maximum(neg, 0.0) + jnp.log1p(jnp.exp(-jnp.abs(neg)))
    c_pos = (0.5 * LRU_C) * sp
    c_exp2 = -LOG2_E * c_pos

    def preact(z, c, g_dst):
        r0 = pl.multiple_of(c * R, R)
        if z == 0:
            xh = (cw[0:1] * xt[pl.ds(r0, R), :]
                  + cw[1:2] * xt[pl.ds(r0 + B, R), :]
                  + cw[2:3] * xt[pl.ds(r0 + 2 * B, R), :]
                  + cw[3:4] * xt[pl.ds(r0 + 3 * B, R), :]) + cb
            xhb[pl.ds(r0, R), :] = xh
        else:
            xh = xhb[pl.ds(r0, R), :]
        lhs = jnp.concatenate([xh.astype(BF16), bias_cols], axis=1)
        g_dst[...] = jnp.dot(lhs, wgz[z], preferred_element_type=F32)

    def gates(z, c, g_src, a_dst, bx_dst):
        xh = xhb[pl.ds(pl.multiple_of(c * R, R), R), :]
        r2 = jnp.tanh(g_src[:, :LANES]) + 1.0
        i2 = jnp.tanh(g_src[:, LANES:]) + 1.0
        a = jnp.exp2(r2 * c_exp2[z:z + 1])
        om = jnp.tanh(r2 * c_pos[z:z + 1]) * (a * a + 1.0)
        mult = jnp.where(om > 0.0, om * lax.rsqrt(om), 0.0)
        a_dst[...] = a
        bx_dst[...] = mult * (i2 * xh)

    def scan_f(c, a_src, bx_src, h):
        for k in range(steps):
            h = a_src[k * B:(k + 1) * B, :] * h + bx_src[k * B:(k + 1) * B, :]
            hfb[pl.ds(pl.multiple_of(c * R + k * B, B), B), :] = h
        return h

    def scan_b(c, a_src, bx_src, h):
        for k in reversed(range(steps)):
            h = a_src[k * B:(k + 1) * B, :] * h + bx_src[k * B:(k + 1) * B, :]
            hf = hfb[pl.ds(pl.multiple_of(c * R + k * B, B), B), :]
            out_ref[pl.ds(c * steps + k, B, stride=P), :] = hf + h
        return h

    def run_pass(z, chunk, scan, h):
        even, odd = (g_e, a_e, bx_e), (g_o, a_o, bx_o)
        preact(z, chunk(0), g_e)
        gates(z, chunk(0), *even)
        preact(z, chunk(1), g_o)

        def body(k, h):
            i = 2 * k + 1
            preact(z, chunk(i + 1), g_e)
            gates(z, chunk(i), *odd)
            h = scan(chunk(i - 1), a_e, bx_e, h)
            preact(z, chunk(i + 2), g_o)
            gates(z, chunk(i + 1), *even)
            return scan(chunk(i), a_o, bx_o, h)
        h = lax.fori_loop(0, n // 2 - 1, body, h)
        gates(z, chunk(n - 1), *odd)
        h = scan(chunk(n - 2), a_e, bx_e, h)
        return scan(chunk(n - 1), a_o, bx_o, h)

    if has_h0:
        h0f, h0b = h0_ref[0], h0_ref[1]
    else:
        h0f = h0b = jnp.zeros((B, LANES), F32)
    st_ref[0] = run_pass(0, lambda i: i, scan_f, h0f)
    st_ref[1] = run_pass(1, lambda i: n - 1 - i, scan_b, h0b)


def _lru_rec(xb, conv_w, conv_b, wg, gb, lam, h0, casts):
    nj, B, P, _ = xb.shape
    S = P - SUBLANES
    rows = B * S
    w = nj * LANES
    has_h0 = h0 is not None
    slab = pl.BlockSpec((None, B * P, LANES), lambda j: (j, 0, 0))
    state = pl.BlockSpec((2, B, LANES), lambda j: (0, 0, j))
    in_specs = [
        slab,
        pl.BlockSpec((4, LANES), lambda j: (0, j)),
        pl.BlockSpec((1, LANES), lambda j: (0, j)),
        pl.BlockSpec((4, None, LANES, LANES), lambda j: (0, j, 0, 0)),
        pl.BlockSpec((4, None, BIAS_ROWS, LANES), lambda j: (0, j, 0, 0)),
        pl.BlockSpec((2, LANES), lambda j: (0, j)),
    ]
    args = [xb.reshape(nj, B * P, LANES), conv_w, conv_b, wg, gb, lam]
    if has_h0:
        in_specs.append(state)
        args.append(h0)
    out_specs = [slab, state]
    out_shape = [jax.ShapeDtypeStruct((nj, B * P, LANES), F32),
                 jax.ShapeDtypeStruct((2, B, w), F32)]
    for wt, layer in casts:
        _, r, c = wt.shape
        assert r % (nj * 2 * SUBLANES) == 0
        in_specs.append(pl.BlockSpec((None, r // nj, c), functools.partial(
            lambda j, layer: (layer, j, 0), layer=layer)))
        args.append(wt)
        out_specs.append(pl.BlockSpec((r // nj, c), lambda j: (j, 0)))
        out_shape.append(jax.ShapeDtypeStruct((r, c), BF16))
    res = pl.pallas_call(
        functools.partial(_lru_rec_kernel, B, S, P, has_h0, len(casts)),
        grid=(nj,),
        in_specs=in_specs,
        out_specs=out_specs,
        out_shape=out_shape,
        scratch_shapes=[pltpu.VMEM((rows + 3 * B, LANES), F32)]
        + [pltpu.VMEM((rows, LANES), F32)] * 2 + [pltpu.VMEM((_gate_rows(B), LANES), F32)] * 4
        + [pltpu.VMEM((_gate_rows(B), 2 * LANES), F32)] * 2,
        compiler_params=pltpu.CompilerParams(
            dimension_semantics=("arbitrary",), vmem_limit_bytes=VMEM_LIMIT),
        name="lru_rec",
    )(*args)
    return res[0].reshape(nj, B, P, LANES), res[1], res[2:]


def _ffn(x1, m, g, wg_ref, wu_ref, wd_ref):
    h = [_modulated_norm(v, g, m[3], m[4]).astype(BF16) for v in x1]
    gu = [(jnp.dot(v, wg_ref[...], preferred_element_type=F32),
           jnp.dot(v, wu_ref[...], preferred_element_type=F32)) for v in h]
    dn = [_dot(jax.nn.silu(gate) * up, wd_ref[...]) for gate, up in gu]
    return [v + m[5] * d for v, d in zip(x1, dn)]


def _lru_out_kernel(tiles, xp_ref, xs_ref, hp_ref, hs_ref, mod_ref, gm_ref, gf_ref,
                    wy_ref, wo_ref, wg_ref, wu_ref, wd_ref, o_ref):
    subs = tiles.subs()

    def body(x_ref, hsum):
        m = tiles.mods(mod_ref)
        x = [x_ref[r, :] for r in subs]
        y = [jax.nn.gelu(_dot(_modulated_norm(v, gm_ref[...], m[0], m[1]), wy_ref[...]))
             for v in x]
        x1 = [v + m[2] * _dot(h * g, wo_ref[...]) for v, h, g in zip(x, hsum, y)]
        for r, v in zip(subs, _ffn(x1, m, gf_ref[...], wg_ref, wu_ref, wd_ref)):
            o_ref[r, :] = v

    nj = hp_ref.shape[0]

    def prompt_hsum(r):
        return jnp.concatenate(
            [hp_ref[j].reshape(tiles.tm, LANES)[r] for j in range(nj)], axis=1)

    def sample_hsum(r):
        return jnp.concatenate([hs_ref[j, r, :] for j in range(nj)], axis=1)

    tiles.branch(lambda: body(xp_ref, [prompt_hsum(r) for r in subs]),
                 lambda: body(xs_ref, [sample_hsum(r) for r in subs]))


def _lru_out(tiles, xp, xs, hp, hs, mods, g_mix, g_ffn, w_in, w_out, w_gate, w_up, w_down, layer):
    d = xp.shape[1]
    w = w_out.shape[0]
    f = w_gate.shape[1]
    w_y = pl.BlockSpec((d, w), lambda i: (0, 1), pipeline_mode=pl.Buffered(1))
    return pl.pallas_call(
        functools.partial(_lru_out_kernel, tiles),
        grid=(tiles.n,),
        in_specs=tiles.flat_pair(d) + tiles.slab_pair(w // LANES) + [
            tiles.mod(layer, mods), _const_spec((1, d)), _const_spec((1, d)), w_y,
            _const_spec((w, d)), _const_spec((d, f)), _const_spec((d, f)), _const_spec((f, d))],
        out_specs=tiles.combined(d),
        out_shape=jax.ShapeDtypeStruct((tiles.n * tiles.tm, d), F32),
        compiler_params=_TOKEN_PARAMS,
        name="lru_out_ffn",
    )(xp, xs, hp, hs, mods, g_mix, g_ffn, w_in, w_out, w_gate, w_up, w_down)


def _cm_kernel(tiles, x_ref, mod_ref, gm_ref, gf_ref, gfin_ref, wi_ref, bi_ref, lng_ref, lnb_ref,
               ws_ref, bs_ref, wo_ref, wg_ref, wu_ref, wd_ref, op_ref, os_ref):
    cw = lng_ref.shape[-1]
    n_groups = ws_ref.shape[0]
    gd = cw // n_groups

    def body():
        m = tiles.mods(mod_ref)
        xs = [x_ref[r, :] for r in tiles.subs()]
        pre = [_dot(_modulated_norm(x, gm_ref[...], m[0], m[1]), wi_ref[...]) + bi_ref[...]
               for x in xs]
        x1 = []
        for x, z in zip(xs, pre):
            uv = jax.nn.gelu(z)
            u = uv[:, :cw]
            v = uv[:, cw:]
            mu = jnp.mean(v, axis=-1, keepdims=True)
            var = jnp.mean(jnp.square(v - mu), axis=-1, keepdims=True)
            v = ((v - mu) * lax.rsqrt(var + EPS) * lng_ref[...] + lnb_ref[...]).astype(BF16)
            mixed_rows = []
            for c in range(SUB // CHUNK):
                vc = v[c * CHUNK:(c + 1) * CHUNK]
                cols = [jnp.dot(ws_ref[g], vc[:, g * gd:(g + 1) * gd],
                                preferred_element_type=F32) for g in range(n_groups)]
                mixed_rows.append(jnp.concatenate(cols, axis=1) + bs_ref[...])
            mixed = jnp.concatenate(mixed_rows, axis=0)
            x1.append(x + m[2] * _dot(u * mixed, wo_ref[...]))
        x2 = _ffn(x1, m, gf_ref[...], wg_ref, wu_ref, wd_ref)
        return jnp.concatenate([_rmsnorm(v, gfin_ref[...]) for v in x2], axis=0)

    out = body()

    def put(o_ref):
        o_ref[...] = out
    tiles.branch(lambda: put(op_ref), lambda: put(os_ref))


def _cm_layer(tiles, x, mods, g_mix, g_ffn, g_final, w_in, b_in, ln_g, ln_b, w_s, b_s, w_out,
              w_gate, w_up, w_down, layer):
    d = x.shape[1]
    cw = w_out.shape[0]
    f = w_gate.shape[1]
    return pl.pallas_call(
        functools.partial(_cm_kernel, tiles),
        grid=(tiles.n,),
        in_specs=[tiles.combined(d), tiles.mod(layer, mods),
                  _const_spec((1, d)), _const_spec((1, d)), _const_spec((1, d)),
                  _const_spec((d, 2 * cw)), _const_spec((1, 2 * cw)),
                  _const_spec((1, cw)), _const_spec((1, cw)),
                  _const_spec(w_s.shape), _const_spec(b_s.shape), _const_spec((cw, d)),
                  _const_spec((d, f)), _const_spec((d, f)), _const_spec((f, d))],
        out_specs=tiles.flat_pair(d),
        out_shape=[jax.ShapeDtypeStruct((tiles.npt * tiles.tm, d), F32),
                   jax.ShapeDtypeStruct((tiles.nst * tiles.tm, d), F32)],
        compiler_params=_TOKEN_PARAMS,
        name="cm_layer",
    )(x, mods, g_mix, g_ffn, g_final, w_in, b_in, ln_g, ln_b, w_s, b_s, w_out,
      w_gate, w_up, w_down)


def _gate_weights(ga_w, gx_w, ga_b, gx_b):
    _, heads, hd, _ = ga_w.shape
    per = LANES // hd
    nj = heads // per
    w4 = jnp.stack([ga_w[0], gx_w[0], ga_w[1], gx_w[1]]).reshape(4, nj, per, hd, hd)
    rows = []
    for h in range(per):
        blocks = [w4[:, :, h] if g == h else jnp.zeros_like(w4[:, :, h]) for g in range(per)]
        rows.append(jnp.concatenate(blocks, axis=-1))
    wg = jnp.concatenate(rows, axis=-2).astype(BF16)
    rest = 0.5 * jnp.stack([ga_b[0], gx_b[0], ga_b[1], gx_b[1]]).reshape(4, nj, 1, LANES)
    parts = []
    for _ in range(BIAS_PARTS):
        scaled = rest * (2.0 ** 16 + 1.0)
        part = scaled - (scaled - rest)
        parts.append(part)
        rest = rest - part
    parts.append(jnp.zeros((4, nj, BIAS_ROWS - BIAS_PARTS, LANES), F32))
    return wg, jnp.concatenate(parts, axis=-2).astype(BF16)


def kernel(x_prompt, x_sample, state_lru, c, c_ctx, mod_w, mod_b, norm_mix, norm_ffn, lru_w_in, lru_conv_w, lru_conv_b, lru_ga_w, lru_ga_b, lru_gx_w, lru_gx_b, lru_lambda, lru_w_out, cm_w_in, cm_b_in, cm_ln_g, cm_ln_b, cm_w_s, cm_b_s, cm_w_out, ffn_w_gate, ffn_w_up, ffn_w_down, final_norm):
    bp, sp, d = x_prompt.shape
    bs, ss, _ = x_sample.shape
    dims = ((bp, sp), (bs, ss))
    light, heavy = _Tiles(dims, TM_LIGHT), _Tiles(dims, TM_HEAVY)
    xp = x_prompt.reshape(bp * sp, d)
    xs = x_sample.reshape(bs * ss, d)

    cond = jnp.concatenate([c_ctx[None], c, jnp.zeros((COND_ROWS - 1 - bs, d), F32)], 0)
    mods = _adaln(cond, mod_w, mod_b)

    w = lru_w_out.shape[1]
    xbp, xbs = _lru_in(light, xp, xs, mods, norm_mix[0:1], lru_w_in[0, :, :w].astype(BF16), 0,
                       SUBLANES)
    wg, gb = _gate_weights(lru_ga_w[0], lru_gx_w[0], lru_ga_b[0], lru_gx_b[0])
    rec = (lru_conv_w[0], lru_conv_b[0:1], wg, gb, lru_lambda[0])
    hp, st, (wi0, wo0, wg0, wu0, wd0) = _lru_rec(
        xbp, *rec, None,
        [(lru_w_in, 0), (lru_w_out, 0), (ffn_w_gate, 0), (ffn_w_up, 0), (ffn_w_down, 0)])
    hs, _, (wi1, wo1, wg1, wu1, wd1) = _lru_rec(
        xbs, *rec, jnp.transpose(state_lru[:, 0], (1, 0, 2)),
        [(cm_w_in, 0), (cm_w_out, 0), (ffn_w_gate, 1), (ffn_w_up, 1), (ffn_w_down, 1)])
    x = _lru_out(heavy, xp, xs, hp, hs, mods, norm_mix[0:1], norm_ffn[0:1], wi0, wo0, wg0, wu0,
                 wd0, 0)

    n_groups = cm_w_s.shape[1]
    gd = cm_w_out.shape[1] // n_groups
    bs_tile = jnp.repeat(cm_b_s[0].T, gd, axis=1)
    yp, ys = _cm_layer(heavy, x, mods, norm_mix[1:2], norm_ffn[1:2], final_norm[None],
                       wi1, cm_b_in[0:1], cm_ln_g[0:1], cm_ln_b[0:1],
                       cm_w_s[0].astype(BF16), bs_tile, wo1, wg1, wu1, wd1, 1)

    new_state = jnp.transpose(st, (1, 0, 2))[:, None]
    return (yp.reshape(bp, sp, d), ys.reshape(bs, ss, d), new_state)
```
